```python
import math
import jax, jax.numpy as jnp
from jax import lax
import numpy as np

D_MODEL = 2048
BATCH = 4
SEQ = 2048
DEPTH = 4
DEC_BATCH = 32
DEC_SEQ = 1
PAST_LEN = 16384
PAGE_SIZE = 128

N_EVEN = (DEPTH + 1) // 2
N_ODD = DEPTH // 2
N_SUB = 3
FFN_RES = 0.5
D_FF = 5632
DA_HEADS = 8
DA_KV_HEADS = 4
DA_REP = DA_HEADS // DA_KV_HEADS
DA_HEAD_DIM = 64
DA_Q_DIM = DA_HEADS * 2 * DA_HEAD_DIM
DA_KV_DIM = DA_KV_HEADS * 2 * DA_HEAD_DIM
DA_OUT = DA_HEADS * 2 * DA_HEAD_DIM
Q_BLOCK = 128
SSM_D_INNER = D_MODEL // 2
SSM_HEAD_DIM = 64
SSM_HEADS = SSM_D_INNER // SSM_HEAD_DIM
SSM_GROUPS = 2
SSM_REP = SSM_HEADS // SSM_GROUPS
SSM_STATE = 128
CONV_W = 4
CONV_DIM = SSM_D_INNER + 2 * SSM_GROUPS * SSM_STATE
SSD_CHUNK = 128
EVEN_IN = DA_Q_DIM + 2 * DA_KV_DIM + SSM_D_INNER + CONV_DIM + SSM_HEADS
EVEN_SPLITS = (DA_Q_DIM, DA_Q_DIM + DA_KV_DIM, DA_Q_DIM + 2 * DA_KV_DIM,
               DA_Q_DIM + 2 * DA_KV_DIM + SSM_D_INNER,
               DA_Q_DIM + 2 * DA_KV_DIM + SSM_D_INNER + CONV_DIM)
EVEN_MIX = DA_OUT + SSM_D_INNER
SW_HEAD_DIM = 64
SW_HEADS = D_MODEL // SW_HEAD_DIM
SW_KV_HEADS = SW_HEADS // 8
SW_REP = SW_HEADS // SW_KV_HEADS
SW_Q_DIM = SW_HEADS * SW_HEAD_DIM
SW_KV_DIM = SW_KV_HEADS * SW_HEAD_DIM
ODD_IN = SW_Q_DIM + 2 * SW_KV_DIM
WINDOW = 128
EPS = 1e-6
NEG_INF = -1e30

kernel_name = 'hybrid_diffattn_ssd_swa_macaron_step'


def rms_norm(x, g):
    xf = x.astype(jnp.float32)
    y = xf * lax.rsqrt(jnp.mean(xf * xf, axis=-1, keepdims=True) + EPS)
    return (y * g.astype(jnp.float32)).astype(x.dtype)


def swiglu(h, wg, wu, wd):
    return (jax.nn.silu(h @ wg) * (h @ wu)) @ wd


def modulated_pre(x, g, shift, scale):
    return rms_norm(x, g) * (1 + scale[:, None]) + shift[:, None]


def gated_post(x, y, g, gate, res_w):
    return x + res_w * gate[:, None] * rms_norm(y, g)


def pad_time(t, n):
    return jnp.pad(t, [(0, 0), (0, n)] + [(0, 0)] * (t.ndim - 2))


def diff_attention(q, segs, lam, qpos):
    b, lq = q.shape[:2]
    blk = min(Q_BLOCK, lq)
    nb = -(-lq // blk)
    pad = nb * blk - lq
    qb = jnp.moveaxis(pad_time(q, pad).reshape((b, nb, blk) + q.shape[2:]), 1, 0)
    pb = jnp.pad(qpos, (0, pad), mode='edge').reshape(nb, blk)
    sizes = [s[0].shape[1] for s in segs]

    def one(args):
        qi, pi = args
        scores = []
        for k, v, kpos in segs:
            s = jnp.einsum('bqgrmd,bkgmd->bgrmqk', qi, k).astype(jnp.float32)
            scores.append(jnp.where(kpos[None, :] <= pi[:, None], s, NEG_INF))
        p = jax.nn.softmax(jnp.concatenate(scores, axis=-1), axis=-1)
        a = p[:, :, :, 0] - lam * p[:, :, :, 1]
        off = 0
        parts = []
        for (k, v, kpos), n in zip(segs, sizes):
            parts.append(jnp.einsum('bgrqk,bkge->bqgre', a[..., off:off + n].astype(v.dtype), v))
            off += n
        out = parts[0]
        for pt in parts[1:]:
            out = out + pt
        return out

    o = lax.map(one, (qb, pb))
    o = jnp.moveaxis(o, 0, 1).reshape((b, nb * blk) + o.shape[3:])
    return o[:, :lq]


def segsum(a):
    t = a.shape[-1]
    cs = jnp.cumsum(a, axis=-1)
    d = cs[..., :, None] - cs[..., None, :]
    return jnp.where(jnp.tril(jnp.ones((t, t), dtype=bool)), d, -jnp.inf)


def ssd_scan(x, a, bm, cm, h0):
    b, L, G, R, P = x.shape
    N = bm.shape[-1]
    q = min(SSD_CHUNK, L)
    nc = -(-L // q)
    pad = nc * q - L
    x = pad_time(x, pad).reshape(b, nc, q, G, R, P)
    a = pad_time(a, pad).reshape(b, nc, q, G, R).transpose(0, 3, 4, 1, 2)
    bm = pad_time(bm, pad).reshape(b, nc, q, G, N)
    cm = pad_time(cm, pad).reshape(b, nc, q, G, N)
    a_cs = jnp.cumsum(a, axis=-1)
    lmat = jnp.exp(segsum(a))
    cb = jnp.einsum('bclgn,bcsgn->bgcls', cm, bm)
    y_diag = jnp.einsum('bgrcls,bcsgrp->bclgrp', cb[:, :, None] * lmat, x)
    decay_states = jnp.exp(a_cs[..., -1:] - a_cs)
    states = jnp.einsum('bclgn,bgrcl,bclgrp->bcgrpn', bm, decay_states, x)
    states = jnp.concatenate([h0[:, None], states], axis=1)
    chunk_tot = jnp.pad(a_cs[..., -1], [(0, 0), (0, 0), (0, 0), (1, 0)])
    decay_chunk = jnp.exp(segsum(chunk_tot))
    new_states = jnp.einsum('bgrzc,bcgrpn->bzgrpn', decay_chunk, states)
    y_off = jnp.einsum('bclgn,bcgrpn,bgrcl->bclgrp', cm, new_states[:, :-1], jnp.exp(a_cs))
    y = (y_diag + y_off).reshape(b, nc * q, G, R, P)[:, :L]
    return y, new_states[:, -1]


def mamba2_mixer(z, xbc, dt_raw, conv_buf, h0, conv_w, conv_b, dt_bias, a_log, d_skip, g_norm):
    b, L = z.shape[:2]
    xp = jnp.concatenate([conv_buf, xbc], axis=1)
    acc = xp[:, 0:L] * conv_w[0]
    for kk in range(1, CONV_W):
        acc = acc + xp[:, kk:kk + L] * conv_w[kk]
    new_buf = xp[:, L:]
    xbc = jax.nn.silu(acc + conv_b)
    xs, bm, cm = jnp.split(xbc, [SSM_D_INNER, SSM_D_INNER + SSM_GROUPS * SSM_STATE], axis=-1)
    f32 = jnp.float32
    dt = jax.nn.softplus(dt_raw.astype(f32) + dt_bias.astype(f32)).reshape(b, L, SSM_GROUPS, SSM_REP)
    a_neg = -jnp.exp(a_log.astype(f32)).reshape(SSM_GROUPS, SSM_REP)
    xs_h = xs.astype(f32).reshape(b, L, SSM_GROUPS, SSM_REP, SSM_HEAD_DIM)
    h0_g = h0.astype(f32).reshape(b, SSM_GROUPS, SSM_REP, SSM_HEAD_DIM, SSM_STATE)
    y, h_new = ssd_scan(xs_h * dt[..., None], dt * a_neg,
                        bm.astype(f32).reshape(b, L, SSM_GROUPS, SSM_STATE),
                        cm.astype(f32).reshape(b, L, SSM_GROUPS, SSM_STATE), h0_g)
    y = y + d_skip.astype(f32).reshape(SSM_GROUPS, SSM_REP)[:, :, None] * xs_h
    y = y.reshape(b, L, SSM_D_INNER) * jax.nn.silu(z.astype(f32))
    yg = y.reshape(b, L, SSM_GROUPS, SSM_D_INNER // SSM_GROUPS)
    yg = yg * lax.rsqrt(jnp.mean(yg * yg, axis=-1, keepdims=True) + EPS)
    y = yg.reshape(b, L, SSM_D_INNER) * g_norm.astype(f32)
    return y.astype(z.dtype), new_buf, h_new.reshape(b, SSM_HEADS, SSM_HEAD_DIM, SSM_STATE).astype(h0.dtype)


def window_attention(q, k, v, k_buf, v_buf, pos0, sinks):
    b, L = q.shape[:2]
    ke = jnp.concatenate([k_buf, k], axis=1)
    ve = jnp.concatenate([v_buf, v], axis=1)
    new_kb, new_vb = ke[:, -WINDOW:], ve[:, -WINDOW:]
    if L <= WINDOW:
        qblk = L
        qb, kb, vb = q[:, None], ke[:, None], ve[:, None]
        qpos = (pos0 + jnp.arange(L, dtype=jnp.int32))[None]
        kpos = (pos0 - WINDOW + jnp.arange(WINDOW + L, dtype=jnp.int32))[None]
    else:
        nb = -(-L // WINDOW)
        lp = nb * WINDOW
        qblk = WINDOW
        qb = pad_time(q, lp - L).reshape((b, nb, WINDOW) + q.shape[2:])
        kp = pad_time(ke, lp - L)
        vp = pad_time(ve, lp - L)
        bshape = (b, nb, WINDOW) + k.shape[2:]
        kb = jnp.concatenate([kp[:, :-WINDOW].reshape(bshape), kp[:, WINDOW:].reshape(bshape)], axis=2)
        vb = jnp.concatenate([vp[:, :-WINDOW].reshape(bshape), vp[:, WINDOW:].reshape(bshape)], axis=2)
        qpos = pos0 + jnp.arange(lp, dtype=jnp.int32).reshape(nb, WINDOW)
        allpos = pos0 - WINDOW + jnp.arange(WINDOW + lp, dtype=jnp.int32)
        kpos = jnp.concatenate([allpos[:-WINDOW].reshape(nb, WINDOW), allpos[WINDOW:].reshape(nb, WINDOW)], axis=1)
    s = jnp.einsum('bnqgrd,bnkgd->bngrqk', qb, kb).astype(jnp.float32)
    delta = qpos[:, :, None] - kpos[:, None, :]
    valid = (delta >= 0) & (delta < WINDOW) & (kpos[:, None, :] >= 0)
    s = jnp.where(valid[None, :, None, None], s, NEG_INF)
    sink = jnp.broadcast_to(sinks.astype(jnp.float32)[None, None, :, :, None, None], s.shape[:-1] + (1,))
    p = jax.nn.softmax(jnp.concatenate([s, sink], axis=-1), axis=-1)[..., :-1]
    o = jnp.einsum('bngrqk,bnkgd->bnqgrd', p.astype(vb.dtype), vb)
    o = o.reshape((b, o.shape[1] * qblk) + o.shape[3:])[:, :L]
    return o, new_kb, new_vb


def even_mixer(h, qpos, kv_pages, j, l, conv0, ssm0, w):
    b, L, _ = h.shape
    proj = h @ w['w_in_even'][j]
    q, k, v, z, xbc, dt = jnp.split(proj, list(EVEN_SPLITS), axis=-1)
    lam_init = 0.8 - 0.6 * math.exp(-0.3 * l)
    lp = w['lambda_qk'][j].astype(jnp.float32)
    lam = jnp.exp(jnp.sum(lp[0] * lp[1])) - jnp.exp(jnp.sum(lp[2] * lp[3])) + lam_init
    qh = q.reshape(b, L, DA_KV_HEADS, DA_REP, 2, DA_HEAD_DIM) * (DA_HEAD_DIM ** -0.5)
    k_rows = k.reshape(b, L, DA_KV_HEADS, 2 * DA_HEAD_DIM)
    v_rows = v.reshape(b, L, DA_KV_HEADS, 2 * DA_HEAD_DIM)
    segs = []
    if kv_pages is not None:
        cache_k, cache_v, page_table = kv_pages
        n_past = page_table.shape[1] * PAGE_SIZE
        kp = cache_k[j, page_table].reshape(b, n_past, DA_KV_HEADS, 2, DA_HEAD_DIM)
        vp = cache_v[j, page_table].reshape(b, n_past, DA_KV_HEADS, 2 * DA_HEAD_DIM)
        segs.append((kp, vp, jnp.arange(n_past, dtype=jnp.int32)))
    segs.append((k_rows.reshape(b, L, DA_KV_HEADS, 2, DA_HEAD_DIM), v_rows, qpos))
    oa = diff_attention(qh, segs, lam, qpos)
    oa = (rms_norm(oa, w['g_subln'][j]) * (1 - lam_init)).reshape(b, L, DA_OUT)
    ob, conv_new, ssm_new = mamba2_mixer(z, xbc, dt, conv0, ssm0, w['conv_w'][j], w['conv_b'][j],
                                         w['dt_bias'][j], w['a_log'][j], w['d_skip'][j], w['g_ssm_norm'][j])
    y = jnp.concatenate([oa, ob], axis=-1) @ w['w_out_even'][j]
    return y, k_rows, v_rows, conv_new, ssm_new


def odd_mixer(h, pos0, wk0, wv0, j, w):
    b, L, _ = h.shape
    proj = h @ w['w_in_odd'][j] + w['b_in_odd'][j]
    q, k, v = jnp.split(proj, [SW_Q_DIM, SW_Q_DIM + SW_KV_DIM], axis=-1)
    qh = q.reshape(b, L, SW_KV_HEADS, SW_REP, SW_HEAD_DIM) * (SW_HEAD_DIM ** -0.5)
    o, nkb, nvb = window_attention(qh, k.reshape(b, L, SW_KV_HEADS, SW_HEAD_DIM),
                                   v.reshape(b, L, SW_KV_HEADS, SW_HEAD_DIM), wk0, wv0, pos0,
                                   w['attn_sinks'][j].reshape(SW_KV_HEADS, SW_REP))
    y = o.reshape(b, L, SW_Q_DIM) @ w['w_out_odd'][j] + w['b_out_odd'][j]
    return y, nkb, nvb


def trunk(x, c, pos0, kv_pages, conv_state, ssm_state, win_k, win_v, w):
    b, L, _ = x.shape
    qpos = pos0 + jnp.arange(L, dtype=jnp.int32)
    sc = jax.nn.silu(c)
    ks, vs, convs, ssms, wks, wvs = [], [], [], [], [], []
    for l in range(DEPTH):
        j = l // 2
        mod = (sc @ w['w_mod'][l] + w['b_mod'][l]).reshape(b, N_SUB, 3, D_MODEL)
        g = w['g_norm'][l]
        h = modulated_pre(x, g[0], mod[:, 0, 0], mod[:, 0, 1])
        y = swiglu(h, w['w_ffn_gate'][l, 0], w['w_ffn_up'][l, 0], w['w_ffn_down'][l, 0])
        x = gated_post(x, y, g[1], mod[:, 0, 2], FFN_RES)
        h = modulated_pre(x, g[2], mod[:, 1, 0], mod[:, 1, 1])
        if l % 2 == 0:
            y, kr, vr, cn, sn = even_mixer(h, qpos, kv_pages, j, l, conv_state[j], ssm_state[j], w)
            ks.append(kr)
            vs.append(vr)
            convs.append(cn)
            ssms.append(sn)
        else:
            y, nkb, nvb = odd_mixer(h, pos0, win_k[j], win_v[j], j, w)
            wks.append(nkb)
            wvs.append(nvb)
        x = gated_post(x, y, g[3], mod[:, 1, 2], 1.0)
        h = modulated_pre(x, g[4], mod[:, 2, 0], mod[:, 2, 1])
        y = swiglu(h, w['w_ffn_gate'][l, 1], w['w_ffn_up'][l, 1], w['w_ffn_down'][l, 1])
        x = gated_post(x, y, g[5], mod[:, 2, 2], FFN_RES)
    return (x, jnp.stack(ks), jnp.stack(vs), jnp.stack(convs), jnp.stack(ssms), jnp.stack(wks), jnp.stack(wvs))


def setup_inputs(seed: int = 0) -> dict:
    key = jax.random.key(seed)
    ks = jax.random.split(key, 40)
    f32 = jnp.float32

    def nrm(i, shape, scale=1.0):
        return jax.random.normal(ks[i], shape, f32) * scale

    n_pages = PAST_LEN // PAGE_SIZE
    n_used = DEC_BATCH * n_pages
    n_pool = n_used + n_used // 4
    page_table = jax.random.permutation(ks[0], n_pool)[:n_used].reshape(DEC_BATCH, n_pages).astype(jnp.int32)
    dt0 = jnp.exp(jax.random.uniform(ks[1], (N_EVEN, SSM_HEADS), f32) * (math.log(0.1) - math.log(0.001)) + math.log(0.001))
    dt_bias = dt0 + jnp.log(-jnp.expm1(-dt0))
    a_log = jnp.log(jax.random.uniform(ks[2], (N_EVEN, SSM_HEADS), f32, minval=1.0, maxval=16.0))
    return {
        'x_prompt': nrm(3, (BATCH, SEQ, D_MODEL)),
        'x_sample': nrm(4, (DEC_BATCH, DEC_SEQ, D_MODEL)),
        'cache_k': nrm(5, (N_EVEN, n_pool, PAGE_SIZE, DA_KV_HEADS, 2 * DA_HEAD_DIM)),
        'cache_v': nrm(6, (N_EVEN, n_pool, PAGE_SIZE, DA_KV_HEADS, 2 * DA_HEAD_DIM)),
        'state_conv': nrm(7, (N_EVEN, DEC_BATCH, CONV_W - 1, CONV_DIM)),
        'state_ssm': nrm(8, (N_EVEN, DEC_BATCH, SSM_HEADS, SSM_HEAD_DIM, SSM_STATE), 0.1),
        'cache_win_k': nrm(9, (N_ODD, DEC_BATCH, WINDOW, SW_KV_HEADS, SW_HEAD_DIM)),
        'cache_win_v': nrm(10, (N_ODD, DEC_BATCH, WINDOW, SW_KV_HEADS, SW_HEAD_DIM)),
        'page_table': page_table,
        'c_prompt': nrm(11, (BATCH, D_MODEL)),
        'c_sample': nrm(12, (DEC_BATCH, D_MODEL)),
        'w_mod': nrm(13, (DEPTH, D_MODEL, N_SUB * 3 * D_MODEL), 0.5 * D_MODEL ** -0.5),
        'b_mod': nrm(14, (DEPTH, N_SUB * 3 * D_MODEL), 0.01),
        'g_norm': 1.0 + nrm(15, (DEPTH, 2 * N_SUB, D_MODEL), 0.05),
        'w_ffn_gate': nrm(16, (DEPTH, 2, D_MODEL, D_FF), D_MODEL ** -0.5),
        'w_ffn_up': nrm(17, (DEPTH, 2, D_MODEL, D_FF), D_MODEL ** -0.5),
        'w_ffn_down': nrm(18, (DEPTH, 2, D_FF, D_MODEL), D_FF ** -0.5),
        'w_in_even': nrm(19, (N_EVEN, D_MODEL, EVEN_IN), D_MODEL ** -0.5),
        'lambda_qk': nrm(20, (N_EVEN, 4, DA_HEAD_DIM), 0.1),
        'g_subln': 1.0 + nrm(21, (N_EVEN, 2 * DA_HEAD_DIM), 0.05),
        'conv_w': nrm(22, (N_EVEN, CONV_W, CONV_DIM), CONV_W ** -0.5),
        'conv_b': nrm(23, (N_EVEN, CONV_DIM), 0.01),
        'dt_bias': dt_bias,
        'a_log': a_log,
        'd_skip': 1.0 + nrm(24, (N_EVEN, SSM_HEADS), 0.1),
        'g_ssm_norm': 1.0 + nrm(25, (N_EVEN, SSM_D_INNER), 0.05),
        'w_out_even': nrm(26, (N_EVEN, EVEN_MIX, D_MODEL), EVEN_MIX ** -0.5),
        'w_in_odd': nrm(27, (N_ODD, D_MODEL, ODD_IN), D_MODEL ** -0.5),
        'b_in_odd': nrm(28, (N_ODD, ODD_IN), 0.01),
        'attn_sinks': nrm(29, (N_ODD, SW_HEADS), 0.5),
        'w_out_odd': nrm(30, (N_ODD, SW_Q_DIM, D_MODEL), SW_Q_DIM ** -0.5),
        'b_out_odd': nrm(31, (N_ODD, D_MODEL), 0.01),
    }


def reference(x_prompt, x_sample, cache_k, cache_v, state_conv, state_ssm, cache_win_k, cache_win_v, page_table,
              c_prompt, c_sample, w_mod, b_mod, g_norm, w_ffn_gate, w_ffn_up, w_ffn_down, w_in_even, lambda_qk,
              g_subln, conv_w, conv_b, dt_bias, a_log, d_skip, g_ssm_norm, w_out_even, w_in_odd, b_in_odd,
              attn_sinks, w_out_odd, b_out_odd):
    w = dict(w_mod=w_mod, b_mod=b_mod, g_norm=g_norm, w_ffn_gate=w_ffn_gate, w_ffn_up=w_ffn_up,
             w_ffn_down=w_ffn_down, w_in_even=w_in_even, lambda_qk=lambda_qk, g_subln=g_subln, conv_w=conv_w,
             conv_b=conv_b, dt_bias=dt_bias, a_log=a_log, d_skip=d_skip, g_ssm_norm=g_ssm_norm,
             w_out_even=w_out_even, w_in_odd=w_in_odd, b_in_odd=b_in_odd, attn_sinks=attn_sinks,
             w_out_odd=w_out_odd, b_out_odd=b_out_odd)
    bp = x_prompt.shape[0]
    dtp = x_prompt.dtype
    conv0 = jnp.zeros((N_EVEN, bp, CONV_W - 1, CONV_DIM), dtp)
    ssm0 = jnp.zeros((N_EVEN, bp, SSM_HEADS, SSM_HEAD_DIM, SSM_STATE), dtp)
    win0 = jnp.zeros((N_ODD, bp, WINDOW, SW_KV_HEADS, SW_HEAD_DIM), dtp)
    y_prompt, k_prompt, v_prompt, conv_prompt, ssm_prompt, wk_prompt, wv_prompt = trunk(
        x_prompt, c_prompt, 0, None, conv0, ssm0, win0, win0, w)
    y_sample, k_sample, v_sample, conv_sample, ssm_sample, wk_sample, wv_sample = trunk(
        x_sample, c_sample, PAST_LEN, (cache_k, cache_v, page_table), state_conv, state_ssm,
        cache_win_k, cache_win_v, w)
    return (y_prompt, y_sample, k_prompt, v_prompt, k_sample, v_sample, conv_prompt, conv_sample,
            ssm_prompt, ssm_sample, wk_prompt, wv_prompt, wk_sample, wv_sample)
```

```python
import functools
import math

import jax
import jax.numpy as jnp
from jax import lax
from jax.experimental import pallas as pl
from jax.experimental.pallas import tpu as pltpu

F32 = jnp.float32
BF16 = jnp.bfloat16

DEPTH = 4
N_SUB = 3
FFN_RES = 0.5
EPS = 1e-6
NEG = -1e30
PAGE = 128
DA_KV = 4
DA_REP = 2
DA_D = 64
DA_Q = DA_KV * DA_REP * 2 * DA_D
DA_KVD = DA_KV * 2 * DA_D
SSM_INNER = 1024
SSM_P = 64
SSM_H = 16
SSM_G = 2
SSM_N = 128
CONV_W = 4
CONV_DIM = SSM_INNER + 2 * SSM_G * SSM_N
CHUNK = 128
EVEN_MAIN = DA_Q + 2 * DA_KVD + SSM_INNER + CONV_DIM
SW_D = 64
SW_KV = 4
SW_REP = 8
SW_Q = SW_KV * SW_REP * SW_D
SW_KVD = SW_KV * SW_D
WINDOW = 128

VMEM_LIMIT = 56 * 1024 * 1024


def _cp(sem, limit=VMEM_LIMIT):
    return pltpu.CompilerParams(dimension_semantics=sem, vmem_limit_bytes=limit)


def _silu(x):
    return x / (1.0 + jnp.exp(-x))


def _softplus(x):
    return jnp.maximum(x, 0.0) + jnp.log(1.0 + jnp.exp(-jnp.abs(x)))


def _rms(x, g):
    return x * lax.rsqrt(jnp.mean(x * x, axis=-1, keepdims=True) + EPS) * g


def _dot(a, b):
    return jnp.dot(a.astype(BF16), b.astype(BF16), preferred_element_type=F32)


def _dot_nt(a, b):
    return lax.dot_general(a.astype(BF16), b.astype(BF16), (((1,), (1,)), ((), ())),
                           preferred_element_type=F32)


def _dot_tn(a, b):
    return lax.dot_general(a.astype(BF16), b.astype(BF16), (((0,), (0,)), ((), ())),
                           preferred_element_type=F32)


def _mod_spec(arr, tm, rows_per_mod):
    _, mrows, d = arr.shape
    return pl.BlockSpec((1, mrows, d), lambda i, j: ((i * tm) // rows_per_mod, 0, 0))


def _mod_kernel(c_ref, w_ref, b_ref, o_ref):
    h = _silu(c_ref[...])
    o_ref[0] = _dot(h, w_ref[0]) + b_ref[0]


def _modulation(c_all, w_mod, b_mod, tn=1024):
    rows, d = c_all.shape
    depth, _, n = w_mod.shape
    return pl.pallas_call(
        _mod_kernel,
        grid=(depth, n // tn),
        in_specs=[pl.BlockSpec((rows, d), lambda l, j: (0, 0)),
                  pl.BlockSpec((1, d, tn), lambda l, j: (l, 0, j)),
                  pl.BlockSpec((1, 1, tn), lambda l, j: (l, 0, j))],
        out_specs=pl.BlockSpec((1, rows, tn), lambda l, j: (l, 0, j)),
        out_shape=jax.ShapeDtypeStruct((depth, rows, n), F32),
        compiler_params=_cp(("parallel", "arbitrary")),
        name="modulation",
    )(c_all, w_mod, b_mod.reshape(depth, 1, n))


def _ffn_kernel(x_ref, sh_ref, sc_ref, gt_ref, gpre_ref, gpost_ref, wg_ref, wu_ref, wd_ref,
                o_ref, h_ref, *, nf, res_w):
    f = pl.program_id(1)

    @pl.when(f == 0)
    def _():
        xn = _rms(x_ref[...], gpre_ref[...])
        h_ref[...] = (xn * (1 + sc_ref[0]) + sh_ref[0]).astype(BF16)

    h = h_ref[...]
    g = jnp.dot(h, wg_ref[0, 0].astype(BF16), preferred_element_type=F32)
    u = jnp.dot(h, wu_ref[0, 0].astype(BF16), preferred_element_type=F32)
    part = _dot(_silu(g) * u, wd_ref[0, 0])

    @pl.when(f == 0)
    def _():
        o_ref[...] = part

    @pl.when(f > 0)
    def _():
        o_ref[...] += part

    @pl.when(f == nf - 1)
    def _():
        yn = _rms(o_ref[...], gpost_ref[...])
        o_ref[...] = x_ref[...] + (res_w * gt_ref[0]) * yn


def _ffn(x, shift, scale, gate, g_pre, g_post, wg, wu, wd, l, s, *, tm, tf, rows_per_mod):
    m, d = x.shape
    f_dim = wg.shape[-1]
    nf = f_dim // tf
    kern = functools.partial(_ffn_kernel, nf=nf, res_w=FFN_RES)
    return pl.pallas_call(
        kern,
        grid=(m // tm, nf),
        in_specs=[pl.BlockSpec((tm, d), lambda i, j: (i, 0)),
                  _mod_spec(shift, tm, rows_per_mod),
                  _mod_spec(scale, tm, rows_per_mod),
                  _mod_spec(gate, tm, rows_per_mod),
                  pl.BlockSpec((1, d), lambda i, j: (0, 0)),
                  pl.BlockSpec((1, d), lambda i, j: (0, 0)),
                  pl.BlockSpec((1, 1, d, tf), lambda i, j: (l, s, 0, j)),
                  pl.BlockSpec((1, 1, d, tf), lambda i, j: (l, s, 0, j)),
                  pl.BlockSpec((1, 1, tf, d), lambda i, j: (l, s, j, 0))],
        out_specs=pl.BlockSpec((tm, d), lambda i, j: (i, 0)),
        out_shape=jax.ShapeDtypeStruct((m, d), F32),
        scratch_shapes=[pltpu.VMEM((tm, d), BF16)],
        compiler_params=_cp(("parallel", "arbitrary")),
        name="ffn",
    )(x, shift, scale, gate, g_pre, g_post, wg, wu, wd)


def _inproj_kernel(x_ref, sh_ref, sc_ref, gpre_ref, w_ref, b_ref, o_ref, h_ref):
    @pl.when(pl.program_id(1) == 0)
    def _():
        xn = _rms(x_ref[...], gpre_ref[...])
        h_ref[...] = (xn * (1 + sc_ref[0]) + sh_ref[0]).astype(BF16)

    o_ref[...] = jnp.dot(h_ref[...], w_ref[0].astype(BF16), preferred_element_type=F32) + b_ref[...]


def _inproj(x, shift, scale, g_pre, w, b, layer, n, *, tm, tn, rows_per_mod):
    m, d = x.shape
    return pl.pallas_call(
        _inproj_kernel,
        grid=(m // tm, n // tn),
        in_specs=[pl.BlockSpec((tm, d), lambda i, j: (i, 0)),
                  _mod_spec(shift, tm, rows_per_mod),
                  _mod_spec(scale, tm, rows_per_mod),
                  pl.BlockSpec((1, d), lambda i, j: (0, 0)),
                  pl.BlockSpec((1, d, tn), lambda i, j: (layer, 0, j)),
                  pl.BlockSpec((1, tn), lambda i, j: (0, j))],
        out_specs=pl.BlockSpec((tm, tn), lambda i, j: (i, j)),
        out_shape=jax.ShapeDtypeStruct((m, n), F32),
        scratch_shapes=[pltpu.VMEM((tm, d), BF16)],
        compiler_params=_cp(("parallel", "arbitrary")),
        name="inproj",
    )(x, shift, scale, g_pre, w, b)


def _outproj_kernel(x_ref, y_ref, gt_ref, gpost_ref, w_ref, b_ref, o_ref, *, nk):
    k = pl.program_id(1)
    part = _dot(y_ref[...], w_ref[0])

    @pl.when(k == 0)
    def _():
        o_ref[...] = part

    @pl.when(k > 0)
    def _():
        o_ref[...] += part

    @pl.when(k == nk - 1)
    def _():
        yn = _rms(o_ref[...] + b_ref[...], gpost_ref[...])
        o_ref[...] = x_ref[...] + (1.0 * gt_ref[0]) * yn


def _outproj(x, y, gate, g_post, w, b, layer, *, tm, tk, rows_per_mod):
    m, d = x.shape
    kdim = w.shape[1]
    nk = kdim // tk
    return pl.pallas_call(
        functools.partial(_outproj_kernel, nk=nk),
        grid=(m // tm, nk),
        in_specs=[pl.BlockSpec((tm, d), lambda i, j: (i, 0)),
                  pl.BlockSpec((tm, tk), lambda i, j: (i, j)),
                  _mod_spec(gate, tm, rows_per_mod),
                  pl.BlockSpec((1, d), lambda i, j: (0, 0)),
                  pl.BlockSpec((1, tk, d), lambda i, j: (layer, j, 0)),
                  pl.BlockSpec((1, d), lambda i, j: (0, 0))],
        out_specs=pl.BlockSpec((tm, d), lambda i, j: (i, 0)),
        out_shape=jax.ShapeDtypeStruct((m, d), F32),
        compiler_params=_cp(("parallel", "arbitrary")),
        name="outproj",
    )(x, y, gate, g_post, w, b)


def _lambda(lp, lam_init):
    a = jnp.sum(lp[0:1] * lp[1:2], axis=-1, keepdims=True)
    b = jnp.sum(lp[2:3] * lp[3:4], axis=-1, keepdims=True)
    return jnp.exp(a) - jnp.exp(b) + lam_init


def _da_prompt_kernel(q_ref, k_ref, v_ref, lp_ref, gs_ref, o_ref, *, tq, lam_init):
    i = pl.program_id(2)
    lam = _lambda(lp_ref[...], lam_init)
    q = q_ref[...] * (DA_D ** -0.5)
    qs = (jnp.concatenate([q[:, 0:64], q[:, 128:192]], axis=0).astype(BF16),
          jnp.concatenate([q[:, 64:128], q[:, 192:256]], axis=0).astype(BF16))
    row = lax.broadcasted_iota(jnp.int32, (2 * tq, tq), 0)
    col = lax.broadcasted_iota(jnp.int32, (2 * tq, tq), 1)
    qloc = jnp.where(row >= tq, row - tq, row)

    def body(j, carry):
        off = pl.multiple_of(j * tq, tq)
        kb = k_ref[pl.ds(off, tq), :].astype(BF16)
        vb = v_ref[pl.ds(off, tq), :].astype(BF16)
        valid = (col + j * tq) <= (qloc + i * tq)
        out = []
        for mi in range(2):
            m_old, l_old, acc = carry[3 * mi:3 * mi + 3]
            s = lax.dot_general(qs[mi], kb[:, mi * 64:(mi + 1) * 64], (((1,), (1,)), ((), ())),
                                preferred_element_type=F32)
            s = jnp.where(valid, s, NEG)
            m_new = jnp.maximum(m_old, jnp.max(s, axis=-1, keepdims=True))
            alpha = jnp.exp(m_old - m_new)
            p = jnp.exp(s - m_new)
            l_new = alpha * l_old + jnp.sum(p, axis=-1, keepdims=True)
            acc = alpha * acc + jnp.dot(p.astype(BF16), vb, preferred_element_type=F32)
            out += [m_new, l_new, acc]
        return tuple(out)

    m0 = jnp.full((2 * tq, 1), NEG, F32)
    l0 = jnp.zeros((2 * tq, 1), F32)
    a0 = jnp.zeros((2 * tq, 2 * DA_D), F32)
    res = lax.fori_loop(0, i + 1, body, (m0, l0, a0, m0, l0, a0))
    a = res[2] / res[1] - lam * (res[5] / res[4])
    a = _rms(a, gs_ref[...]) * (1 - lam_init)
    o_ref[:, 0:128] = a[:tq]
    o_ref[:, 128:256] = a[tq:]


def _da_prompt(proj, lp, gs, nseq, seq, lam_init, tq=256):
    m = proj.shape[0]
    nq = seq // tq
    kcol = DA_Q // 128
    vcol = (DA_Q + DA_KVD) // 128
    return pl.pallas_call(
        functools.partial(_da_prompt_kernel, tq=tq, lam_init=lam_init),
        grid=(nseq, DA_KV, nq),
        in_specs=[pl.BlockSpec((tq, 256), lambda b, g, i: (b * nq + i, g)),
                  pl.BlockSpec((seq, 128), lambda b, g, i: (b, kcol + g)),
                  pl.BlockSpec((seq, 128), lambda b, g, i: (b, vcol + g)),
                  pl.BlockSpec((4, DA_D), lambda b, g, i: (0, 0)),
                  pl.BlockSpec((1, 2 * DA_D), lambda b, g, i: (0, 0))],
        out_specs=pl.BlockSpec((tq, 256), lambda b, g, i: (b * nq + i, g)),
        out_shape=jax.ShapeDtypeStruct((m, DA_Q), F32),
        compiler_params=_cp(("parallel", "parallel", "arbitrary")),
        name="diff_attn_prompt",
    )(proj, proj, proj, lp, gs)


def _da_decode_kernel(pt_ref, qm_ref, kn_ref, vn_ref, lp_ref, gs_ref, *refs, pc, nc, lam_init):
    del pt_ref
    k_refs = refs[:pc]
    v_refs = refs[pc:2 * pc]
    o_ref, m_ref, l_ref, acc_ref, kb_ref, vb_ref = refs[2 * pc:]
    c = pl.program_id(1)

    @pl.when(c == 0)
    def _():
        m_ref[...] = jnp.full(m_ref.shape, NEG, F32)
        l_ref[...] = jnp.zeros(l_ref.shape, F32)
        acc_ref[...] = jnp.zeros(acc_ref.shape, F32)

    rows = PAGE * DA_KV
    for t in range(pc):
        kb_ref[t * rows:(t + 1) * rows, :] = k_refs[t][0, 0].astype(BF16)
        vb_ref[t * rows:(t + 1) * rows, :] = v_refs[t][0, 0].astype(BF16)
    qm = qm_ref[0] * (DA_D ** -0.5)
    s = _dot_nt(qm, kb_ref[...])
    row_g = (lax.broadcasted_iota(jnp.int32, s.shape, 0) // DA_REP) & (DA_KV - 1)
    col_g = lax.broadcasted_iota(jnp.int32, s.shape, 1) & (DA_KV - 1)
    s = jnp.where(row_g == col_g, s, NEG)
    m_old = m_ref[...]
    m_new = jnp.maximum(m_old, jnp.max(s, axis=-1, keepdims=True))
    alpha = jnp.exp(m_old - m_new)
    p = jnp.exp(s - m_new)
    l_ref[...] = alpha * l_ref[...] + jnp.sum(p, axis=-1, keepdims=True)
    acc_ref[...] = alpha * acc_ref[...] + jnp.dot(p.astype(BF16), vb_ref[...], preferred_element_type=F32)
    m_ref[...] = m_new

    @pl.when(c == nc - 1)
    def _():
        lam = _lambda(lp_ref[...], lam_init)
        s_new = jnp.sum(qm * kn_ref[0], axis=-1, keepdims=True)
        m_old = m_ref[...]
        m_fin = jnp.maximum(m_old, s_new)
        alpha = jnp.exp(m_old - m_fin)
        p_new = jnp.exp(s_new - m_fin)
        l_fin = alpha * l_ref[...] + p_new
        acc = alpha * acc_ref[...] + p_new * vn_ref[0]
        o = acc / l_fin
        a = o[0:8] - lam * o[8:16]
        o_ref[0] = _rms(a, gs_ref[...]) * (1 - lam_init)


def _da_decode(q, k_new, v_new, lp, gs, cache_k, cache_v, page_table, j, lam_init, pc=16):
    nb = q.shape[0]
    n_pages = page_table.shape[1]
    nc = n_pages // pc
    rows = PAGE * DA_KV
    ck = cache_k.reshape(cache_k.shape[0], cache_k.shape[1], rows, 2 * DA_D)
    cv = cache_v.reshape(cache_v.shape[0], cache_v.shape[1], rows, 2 * DA_D)
    qt = q.reshape(nb, DA_KV, DA_REP, 2, DA_D).transpose(0, 3, 1, 2, 4)
    own_m = jnp.eye(2, dtype=bool)[None, :, None, None, :, None]
    qm = jnp.where(own_m, qt[:, :, :, :, None, :], 0.0).reshape(nb, 16, 2 * DA_D)
    rep = lambda a: jnp.broadcast_to(a.reshape(nb, 1, DA_KV, 1, 2 * DA_D),
                                     (nb, 2, DA_KV, DA_REP, 2 * DA_D)).reshape(nb, 16, 2 * DA_D)

    def page_spec(t):
        return pl.BlockSpec((1, 1, rows, 2 * DA_D), lambda b, c, pt: (j, pt[b, c * pc + t], 0, 0))

    row_spec = pl.BlockSpec((1, 16, 2 * DA_D), lambda b, c, pt: (b, 0, 0))
    grid_spec = pltpu.PrefetchScalarGridSpec(
        num_scalar_prefetch=1,
        grid=(nb, nc),
        in_specs=[row_spec, row_spec, row_spec,
                  pl.BlockSpec((4, DA_D), lambda b, c, pt: (0, 0)),
                  pl.BlockSpec((1, 2 * DA_D), lambda b, c, pt: (0, 0))]
                 + [page_spec(t) for t in range(pc)] + [page_spec(t) for t in range(pc)],
        out_specs=pl.BlockSpec((1, 8, 128), lambda b, c, pt: (b, 0, 0)),
        scratch_shapes=[pltpu.VMEM((16, 1), F32), pltpu.VMEM((16, 1), F32),
                        pltpu.VMEM((16, 2 * DA_D), F32),
                        pltpu.VMEM((pc * rows, 2 * DA_D), BF16), pltpu.VMEM((pc * rows, 2 * DA_D), BF16)])
    out = pl.pallas_call(
        functools.partial(_da_decode_kernel, pc=pc, nc=nc, lam_init=lam_init),
        grid_spec=grid_spec,
        out_shape=jax.ShapeDtypeStruct((nb, 8, 128), F32),
        compiler_params=_cp(("parallel", "arbitrary")),
        name="diff_attn_decode",
    )(page_table, qm, rep(k_new), rep(v_new), lp, gs, *([ck] * pc), *([cv] * pc))
    return out.reshape(nb, DA_Q)


def _split3(x):
    x1 = x.astype(BF16)
    r1 = x - x1.astype(F32)
    x2 = r1.astype(BF16)
    x3 = (r1 - x2.astype(F32)).astype(BF16)
    return x1, x2, x3


def _group_norm_gate(y, z, gn):
    yz = y * _silu(z)
    w = SSM_INNER // SSM_G
    parts = []
    for g in range(SSM_G):
        seg = yz[:, g * w:(g + 1) * w]
        parts.append(seg * lax.rsqrt(jnp.mean(seg * seg, axis=-1, keepdims=True) + EPS))
    return jnp.concatenate(parts, axis=-1) * gn


def _ssd_prompt_kernel(xbc_ref, z_ref, dt_ref, dtt_ref, cw_ref, cb_ref, dtb_ref, dtbt_ref,
                       alog_ref, alogt_ref, dskip_ref, gn_ref, y_ref, st_ref, ext_ref, h_ref, *, nchunk):
    c = pl.program_id(1)
    q = CHUNK

    @pl.when(c == 0)
    def _():
        ext_ref[0:8, :] = jnp.zeros((8, CONV_DIM), F32)
        h_ref[...] = jnp.zeros(h_ref.shape, F32)

    @pl.when(c > 0)
    def _():
        ext_ref[0:8, :] = ext_ref[q:q + 8, :]

    ext_ref[8:q + 8, :] = xbc_ref[...]
    cw = cw_ref[...]
    acc = ext_ref[5:q + 5, :] * cw[0:1]
    for kk in range(1, CONV_W):
        acc = acc + ext_ref[5 + kk:q + 5 + kk, :] * cw[kk:kk + 1]
    xa = _silu(acc + cb_ref[...])
    xs = xa[:, :SSM_INNER]
    bm = xa[:, SSM_INNER:SSM_INNER + SSM_G * SSM_N].astype(BF16)
    cm = xa[:, SSM_INNER + SSM_G * SSM_N:].astype(BF16)

    dt = _softplus(dt_ref[...] + dtb_ref[...])
    dtt = _softplus(dtt_ref[...] + dtbt_ref[...])
    a = dt * (-jnp.exp(alog_ref[...]))
    at = dtt * (-jnp.exp(alogt_ref[...]))
    ri = lax.broadcasted_iota(jnp.int32, (q, q), 0)
    ci = lax.broadcasted_iota(jnp.int32, (q, q), 1)
    lower = ri >= ci
    tril = jnp.where(lower, 1.0, 0.0).astype(BF16)
    triu = jnp.where(ri <= ci, 1.0, 0.0).astype(BF16)
    cs = sum(jnp.dot(tril, part, preferred_element_type=F32) for part in _split3(a))
    cst = sum(jnp.dot(part, triu, preferred_element_type=F32) for part in _split3(at))
    tot = cs[q - 1:q, :]
    e_cs = jnp.exp(cs)
    e_dec = jnp.exp(tot - cs)
    e_tot = jnp.exp(tot)
    dskip = dskip_ref[...]

    ys = []
    for g in range(SSM_G):
        bg = bm[:, g * SSM_N:(g + 1) * SSM_N]
        cg = cm[:, g * SSM_N:(g + 1) * SSM_N]
        cb = lax.dot_general(cg, bg, (((1,), (1,)), ((), ())), preferred_element_type=F32)
        for r in range(SSM_H // SSM_G):
            h = g * (SSM_H // SSM_G) + r
            lmat = jnp.exp(jnp.where(lower, cs[:, h:h + 1] - cst[h:h + 1, :], NEG))
            x_h = xs[:, h * SSM_P:(h + 1) * SSM_P]
            xdt = x_h * dt[:, h:h + 1]
            y_diag = _dot(cb * lmat, xdt)
            hp = h_ref[h]
            y_off = lax.dot_general(cg, hp.astype(BF16), (((1,), (1,)), ((), ())),
                                    preferred_element_type=F32) * e_cs[:, h:h + 1]
            upd = lax.dot_general((xdt * e_dec[:, h:h + 1]).astype(BF16), bg, (((0,), (0,)), ((), ())),
                                  preferred_element_type=F32)
            h_ref[h] = e_tot[:, h:h + 1] * hp + upd
            ys.append(y_diag + y_off + dskip[:, h:h + 1] * x_h)
    y = jnp.concatenate(ys, axis=-1)
    y_ref[...] = _group_norm_gate(y, z_ref[...], gn_ref[...])

    @pl.when(c == nchunk - 1)
    def _():
        st_ref[0] = h_ref[...]


def _ssd_prompt(proj, dt_raw, cw, cb, dtb, alog, dskip, gn, nseq, seq):
    m = proj.shape[0]
    nchunk = seq // CHUNK
    zcol = (DA_Q + 2 * DA_KVD) // SSM_INNER
    xcol = (DA_Q + 2 * DA_KVD + SSM_INNER) // CONV_DIM
    const = lambda b, c: (0, 0)
    y, st = pl.pallas_call(
        functools.partial(_ssd_prompt_kernel, nchunk=nchunk),
        grid=(nseq, nchunk),
        in_specs=[pl.BlockSpec((CHUNK, CONV_DIM), lambda b, c: (b * nchunk + c, xcol)),
                  pl.BlockSpec((CHUNK, SSM_INNER), lambda b, c: (b * nchunk + c, zcol)),
                  pl.BlockSpec((CHUNK, SSM_H), lambda b, c: (b * nchunk + c, 0)),
                  pl.BlockSpec((SSM_H, CHUNK), lambda b, c: (0, b * nchunk + c)),
                  pl.BlockSpec((CONV_W, CONV_DIM), const),
                  pl.BlockSpec((1, CONV_DIM), const),
                  pl.BlockSpec((1, SSM_H), const),
                  pl.BlockSpec((SSM_H, 1), const),
                  pl.BlockSpec((1, SSM_H), const),
                  pl.BlockSpec((SSM_H, 1), const),
                  pl.BlockSpec((1, SSM_H), const),
                  pl.BlockSpec((1, SSM_INNER), const)],
        out_specs=[pl.BlockSpec((CHUNK, SSM_INNER), lambda b, c: (b * nchunk + c, 0)),
                   pl.BlockSpec((1, SSM_H, SSM_P, SSM_N), lambda b, c: (b, 0, 0, 0))],
        out_shape=[jax.ShapeDtypeStruct((m, SSM_INNER), F32),
                   jax.ShapeDtypeStruct((nseq, SSM_H, SSM_P, SSM_N), F32)],
        scratch_shapes=[pltpu.VMEM((CHUNK + 8, CONV_DIM), F32),
                        pltpu.VMEM((SSM_H, SSM_P, SSM_N), F32)],
        compiler_params=_cp(("parallel", "arbitrary")),
        name="ssd_prompt",
    )(proj, proj, dt_raw, dt_raw.T, cw, cb.reshape(1, -1), dtb.reshape(1, -1), dtb.reshape(-1, 1),
      alog.reshape(1, -1), alog.reshape(-1, 1), dskip.reshape(1, -1), gn.reshape(1, -1))
    return y, st


def _ssd_dec_pre_kernel(xbc_ref, cst_ref, dt_ref, cw_ref, cb_ref, dtb_ref, xa_ref, dts_ref):
    cw = cw_ref[...]
    acc = cst_ref[0] * cw[0:1]
    for kk in range(1, CONV_W - 1):
        acc = acc + cst_ref[kk] * cw[kk:kk + 1]
    acc = acc + xbc_ref[...] * cw[CONV_W - 1:CONV_W]
    xa_ref[...] = _silu(acc + cb_ref[...])
    dts_ref[...] = _softplus(dt_ref[...] + dtb_ref[...])


def _ssd_dec_state_kernel(xt_ref, dts_ref, alog_ref, b_ref, c_ref, dskip_ref, h_ref, yt_ref, hn_ref):
    xt = xt_ref[0]
    dts = dts_ref[0]
    a = dts * (-jnp.exp(alog_ref[...]))
    e_a = jnp.exp(a)
    dskip = dskip_ref[...]
    cols = []
    for h in range(SSM_H):
        g = h // (SSM_H // SSM_G)
        xcol = xt[:, h:h + 1]
        brow = b_ref[0][:, g * SSM_N:(g + 1) * SSM_N]
        crow = c_ref[0][:, g * SSM_N:(g + 1) * SSM_N]
        hn = e_a[:, h:h + 1] * h_ref[0, h] + (xcol * dts[:, h:h + 1]) * brow
        hn_ref[0, h] = hn
        cols.append(jnp.sum(hn * crow, axis=-1, keepdims=True) + dskip[:, h:h + 1] * xcol)
    yt_ref[0] = jnp.concatenate(cols, axis=-1)


def _ssd_dec_post_kernel(y_ref, z_ref, gn_ref, o_ref):
    o_ref[...] = _group_norm_gate(y_ref[...], z_ref[...], gn_ref[...])


def _ssd_decode(proj, dt_raw, conv_state, ssm_state, cw, cb, dtb, alog, dskip, gn):
    nb = proj.shape[0]
    zcol = (DA_Q + 2 * DA_KVD) // SSM_INNER
    xcol = (DA_Q + 2 * DA_KVD + SSM_INNER) // CONV_DIM
    full = lambda shape: pl.BlockSpec(shape, lambda i: (0,) * len(shape))
    xa, dts = pl.pallas_call(
        _ssd_dec_pre_kernel,
        grid=(1,),
        in_specs=[pl.BlockSpec((nb, CONV_DIM), lambda i: (0, xcol)),
                  full((CONV_W - 1, nb, CONV_DIM)),
                  full((nb, SSM_H)),
                  full((CONV_W, CONV_DIM)),
                  full((1, CONV_DIM)),
                  full((1, SSM_H))],
        out_specs=[full((nb, CONV_DIM)), full((nb, SSM_H))],
        out_shape=[jax.ShapeDtypeStruct((nb, CONV_DIM), F32), jax.ShapeDtypeStruct((nb, SSM_H), F32)],
        compiler_params=_cp(("arbitrary",)),
        name="ssd_decode_pre",
    )(proj, conv_state.transpose(1, 0, 2), dt_raw, cw, cb.reshape(1, -1), dtb.reshape(1, -1))
    xt = xa[:, :SSM_INNER].reshape(nb, SSM_H, SSM_P).transpose(0, 2, 1)
    bmat = xa[:, SSM_INNER:SSM_INNER + SSM_G * SSM_N].reshape(nb, 1, SSM_G * SSM_N)
    cmat = xa[:, SSM_INNER + SSM_G * SSM_N:].reshape(nb, 1, SSM_G * SSM_N)
    yt, h_new = pl.pallas_call(
        _ssd_dec_state_kernel,
        grid=(nb,),
        in_specs=[pl.BlockSpec((1, SSM_P, SSM_H), lambda b: (b, 0, 0)),
                  pl.BlockSpec((1, 1, SSM_H), lambda b: (b, 0, 0)),
                  pl.BlockSpec((1, SSM_H), lambda b: (0, 0)),
                  pl.BlockSpec((1, 1, SSM_G * SSM_N), lambda b: (b, 0, 0)),
                  pl.BlockSpec((1, 1, SSM_G * SSM_N), lambda b: (b, 0, 0)),
                  pl.BlockSpec((1, SSM_H), lambda b: (0, 0)),
                  pl.BlockSpec((1, SSM_H, SSM_P, SSM_N), lambda b: (b, 0, 0, 0))],
        out_specs=[pl.BlockSpec((1, SSM_P, SSM_H), lambda b: (b, 0, 0)),
                   pl.BlockSpec((1, SSM_H, SSM_P, SSM_N), lambda b: (b, 0, 0, 0))],
        out_shape=[jax.ShapeDtypeStruct((nb, SSM_P, SSM_H), F32),
                   jax.ShapeDtypeStruct(ssm_state.shape, F32)],
        compiler_params=_cp(("parallel",)),
        name="ssd_decode_state",
    )(xt, dts.reshape(nb, 1, SSM_H), alog.reshape(1, -1), bmat, cmat, dskip.reshape(1, -1), ssm_state)
    y = yt.transpose(0, 2, 1).reshape(nb, SSM_INNER)
    ob = pl.pallas_call(
        _ssd_dec_post_kernel,
        grid=(1,),
        in_specs=[full((nb, SSM_INNER)),
                  pl.BlockSpec((nb, SSM_INNER), lambda i: (0, zcol)),
                  full((1, SSM_INNER))],
        out_specs=full((nb, SSM_INNER)),
        out_shape=jax.ShapeDtypeStruct((nb, SSM_INNER), F32),
        compiler_params=_cp(("arbitrary",)),
        name="ssd_decode_post",
    )(y, proj, gn.reshape(1, -1))
    return ob, h_new


def _swa_prompt_kernel(sink_ref, q_ref, kp_ref, kc_ref, vp_ref, vc_ref, o_ref):
    i = pl.program_id(1)
    w = WINDOW
    sinks = sink_ref[...]
    q = q_ref[...] * (SW_D ** -0.5)
    kk = jnp.concatenate([kp_ref[...], kc_ref[...]], axis=0).astype(BF16)
    vv = jnp.concatenate([vp_ref[...], vc_ref[...]], axis=0).astype(BF16)
    rows = SW_REP * w
    qloc = lax.broadcasted_iota(jnp.int32, (rows, 2 * w), 0) & (w - 1)
    col = lax.broadcasted_iota(jnp.int32, (rows, 2 * w), 1)
    t = col - qloc
    colmin = jnp.where(i > 0, 0, w)
    bias = jnp.where(t >= 1, jnp.where(t <= w, jnp.where(col >= colmin, 0.0, NEG), NEG), NEG)
    pieces = []
    for g in range(SW_KV):
        qg = jnp.concatenate([q[:, (g * SW_REP + r) * SW_D:(g * SW_REP + r + 1) * SW_D]
                              for r in range(SW_REP)], axis=0).astype(BF16)
        s = lax.dot_general(qg, kk[:, g * SW_D:(g + 1) * SW_D], (((1,), (1,)), ((), ())),
                            preferred_element_type=F32) + bias
        sink = jnp.concatenate([jnp.broadcast_to(sinks[:, g * SW_REP + r:g * SW_REP + r + 1], (w, 1))
                                for r in range(SW_REP)], axis=0)
        mx = jnp.maximum(jnp.max(s, axis=-1, keepdims=True), sink)
        p = jnp.exp(s - mx)
        den = jnp.sum(p, axis=-1, keepdims=True) + jnp.exp(sink - mx)
        o = jnp.dot(p.astype(BF16), vv[:, g * SW_D:(g + 1) * SW_D], preferred_element_type=F32) / den
        pieces += [o[r * w:(r + 1) * w] for r in range(SW_REP)]
    o_ref[...] = jnp.concatenate(pieces, axis=-1)


def _swa_prompt(proj, sinks, nseq, seq):
    m = proj.shape[0]
    nb = seq // WINDOW
    kcol = SW_Q // SW_KVD
    vcol = kcol + 1
    prev = lambda b, i: (b * nb + jnp.maximum(i - 1, 0), kcol)
    prev_v = lambda b, i: (b * nb + jnp.maximum(i - 1, 0), vcol)
    return pl.pallas_call(
        _swa_prompt_kernel,
        grid=(nseq, nb),
        in_specs=[pl.BlockSpec((1, SW_KV * SW_REP), lambda b, i: (0, 0)),
                  pl.BlockSpec((WINDOW, SW_Q), lambda b, i: (b * nb + i, 0)),
                  pl.BlockSpec((WINDOW, SW_KVD), prev),
                  pl.BlockSpec((WINDOW, SW_KVD), lambda b, i: (b * nb + i, kcol)),
                  pl.BlockSpec((WINDOW, SW_KVD), prev_v),
                  pl.BlockSpec((WINDOW, SW_KVD), lambda b, i: (b * nb + i, vcol))],
        out_specs=pl.BlockSpec((WINDOW, SW_Q), lambda b, i: (b * nb + i, 0)),
        out_shape=jax.ShapeDtypeStruct((m, SW_Q), F32),
        compiler_params=_cp(("parallel", "arbitrary")),
        name="swa_prompt",
    )(sinks.reshape(1, -1), proj, proj, proj, proj, proj)


def _swa_decode_kernel(q_ref, kn_ref, vn_ref, sink_ref, kb_ref, vb_ref, o_ref):
    q = q_ref[0] * (SW_D ** -0.5)
    nh = SW_KV * SW_REP
    q4 = jnp.concatenate([q] * SW_KV, axis=-1)
    rowg = lax.broadcasted_iota(jnp.int32, (nh, SW_KVD), 0) // SW_REP
    colg = lax.broadcasted_iota(jnp.int32, (nh, SW_KVD), 1) // SW_D
    own = rowg == colg
    qbd = jnp.where(own, q4, 0.0)
    s = _dot(qbd, kb_ref[0])
    col = lax.broadcasted_iota(jnp.int32, (nh, WINDOW), 1)
    s = jnp.where(col >= 1, s, NEG)
    s_new = jnp.sum(qbd * kn_ref[0], axis=-1, keepdims=True)
    sink = sink_ref[...]
    mx = jnp.maximum(jnp.maximum(jnp.max(s, axis=-1, keepdims=True), s_new), sink)
    p = jnp.exp(s - mx)
    p_new = jnp.exp(s_new - mx)
    den = jnp.sum(p, axis=-1, keepdims=True) + p_new + jnp.exp(sink - mx)
    o = (_dot_nt(p, vb_ref[0]) + p_new * vn_ref[0]) / den
    o = jnp.where(own, o, 0.0)
    out = o[:, 0:SW_D]
    for g in range(1, SW_KV):
        out = out + o[:, g * SW_D:(g + 1) * SW_D]
    o_ref[0] = out


def _swa_decode(q, k_new, v_new, sinks, win_k, win_v):
    nb = q.shape[0]
    nh = SW_KV * SW_REP
    out = pl.pallas_call(
        _swa_decode_kernel,
        grid=(nb,),
        in_specs=[pl.BlockSpec((1, nh, SW_D), lambda b: (b, 0, 0)),
                  pl.BlockSpec((1, 1, SW_KVD), lambda b: (b, 0, 0)),
                  pl.BlockSpec((1, 1, SW_KVD), lambda b: (b, 0, 0)),
                  pl.BlockSpec((nh, 1), lambda b: (0, 0)),
                  pl.BlockSpec((1, SW_KVD, WINDOW), lambda b: (b, 0, 0)),
                  pl.BlockSpec((1, SW_KVD, WINDOW), lambda b: (b, 0, 0))],
        out_specs=pl.BlockSpec((1, nh, SW_D), lambda b: (b, 0, 0)),
        out_shape=jax.ShapeDtypeStruct((nb, nh, SW_D), F32),
        compiler_params=_cp(("parallel",)),
        name="swa_decode",
    )(q.reshape(nb, nh, SW_D), k_new.reshape(nb, 1, SW_KVD), v_new.reshape(nb, 1, SW_KVD),
      sinks.reshape(nh, 1), win_k.transpose(0, 2, 3, 1).reshape(nb, SW_KVD, WINDOW),
      win_v.transpose(0, 2, 3, 1).reshape(nb, SW_KVD, WINDOW))
    return out.reshape(nb, SW_Q)


def _trunk(x, mod, w, *, nseq, seq, prompt, caches):
    d = x.shape[-1]
    if prompt:
        tm_ffn, tm_in, tm_out = min(512, seq), min(1024, seq), min(512, seq)
        rows_per_mod = seq
        mshape = lambda a: a[:, None, :]
    else:
        tm_ffn = tm_in = tm_out = nseq
        rows_per_mod = nseq
        mshape = lambda a: a[None]
    tf, tn, tk = 512, 512, 512
    ks, vs, convs, ssms, wks, wvs = [], [], [], [], [], []
    for l in range(DEPTH):
        j = l // 2
        g = w['g_norm'][l]
        md = lambda s, t: mshape(mod[l, :, s, t])
        row = lambda v: v.reshape(1, -1)
        x = _ffn(x, md(0, 0), md(0, 1), md(0, 2), row(g[0]), row(g[1]),
                 w['w_ffn_gate'], w['w_ffn_up'], w['w_ffn_down'], l, 0,
                 tm=tm_ffn, tf=tf, rows_per_mod=rows_per_mod)
        if l % 2 == 0:
            lam_init = 0.8 - 0.6 * math.exp(-0.3 * l)
            w_in = w['w_in_even']
            proj = _inproj(x, md(1, 0), md(1, 1), row(g[2]), w_in, jnp.zeros((1, EVEN_MAIN), F32),
                           j, EVEN_MAIN, tm=tm_in, tn=tn, rows_per_mod=rows_per_mod)
            dt_raw = _inproj(x, md(1, 0), md(1, 1), row(g[2]), w_in[:, :, EVEN_MAIN:],
                             jnp.zeros((1, SSM_H), F32), j, SSM_H, tm=tm_in, tn=SSM_H,
                             rows_per_mod=rows_per_mod)
            lp = w['lambda_qk'][j]
            gs = row(w['g_subln'][j])
            ssm_w = (w['conv_w'][j], w['conv_b'][j], w['dt_bias'][j], w['a_log'][j], w['d_skip'][j],
                     w['g_ssm_norm'][j])
            k_rows = proj[:, DA_Q:DA_Q + DA_KVD]
            v_rows = proj[:, DA_Q + DA_KVD:DA_Q + 2 * DA_KVD]
            xbc = proj[:, EVEN_MAIN - CONV_DIM:]
            if prompt:
                oa = _da_prompt(proj, lp, gs, nseq, seq, lam_init)
                ob, ssm_new = _ssd_prompt(proj, dt_raw, *ssm_w, nseq, seq)
                conv_new = xbc.reshape(nseq, seq, CONV_DIM)[:, seq - (CONV_W - 1):]
            else:
                cache_k, cache_v, page_table, conv_state, ssm_state = caches['even']
                oa = _da_decode(proj[:, :DA_Q], k_rows, v_rows, lp, gs, cache_k, cache_v, page_table, j,
                                lam_init)
                ob, ssm_new = _ssd_decode(proj, dt_raw, conv_state[j], ssm_state[j], *ssm_w)
                conv_new = jnp.concatenate([conv_state[j][:, 1:], xbc[:, None]], axis=1)
            ks.append(k_rows.reshape(nseq, seq, DA_KV, 2 * DA_D))
            vs.append(v_rows.reshape(nseq, seq, DA_KV, 2 * DA_D))
            convs.append(conv_new)
            ssms.append(ssm_new)
            y = jnp.concatenate([oa, ob], axis=-1)
            w_out, b_out = w['w_out_even'], jnp.zeros((1, d), F32)
        else:
            proj = _inproj(x, md(1, 0), md(1, 1), row(g[2]), w['w_in_odd'], row(w['b_in_odd'][j]),
                           j, SW_Q + 2 * SW_KVD, tm=tm_in, tn=tn, rows_per_mod=rows_per_mod)
            k_rows = proj[:, SW_Q:SW_Q + SW_KVD]
            v_rows = proj[:, SW_Q + SW_KVD:]
            if prompt:
                y = _swa_prompt(proj, w['attn_sinks'][j], nseq, seq)
                tail = proj.reshape(nseq, seq, -1)[:, seq - WINDOW:]
                wk_new = tail[:, :, SW_Q:SW_Q + SW_KVD].reshape(nseq, WINDOW, SW_KV, SW_D)
                wv_new = tail[:, :, SW_Q + SW_KVD:].reshape(nseq, WINDOW, SW_KV, SW_D)
            else:
                win_k, win_v = caches['odd']
                y = _swa_decode(proj[:, :SW_Q], k_rows, v_rows, w['attn_sinks'][j], win_k[j], win_v[j])
                wk_new = jnp.concatenate([win_k[j][:, 1:], k_rows.reshape(nseq, 1, SW_KV, SW_D)], axis=1)
                wv_new = jnp.concatenate([win_v[j][:, 1:], v_rows.reshape(nseq, 1, SW_KV, SW_D)], axis=1)
            wks.append(wk_new)
            wvs.append(wv_new)
            w_out, b_out = w['w_out_odd'], row(w['b_out_odd'][j])
        x = _outproj(x, y, md(1, 2), row(g[3]), w_out, b_out, j, tm=tm_out, tk=tk,
                     rows_per_mod=rows_per_mod)
        x = _ffn(x, md(2, 0), md(2, 1), md(2, 2), row(g[4]), row(g[5]),
                 w['w_ffn_gate'], w['w_ffn_up'], w['w_ffn_down'], l, 1,
                 tm=tm_ffn, tf=tf, rows_per_mod=rows_per_mod)
    return (x, jnp.stack(ks), jnp.stack(vs), jnp.stack(convs), jnp.stack(ssms), jnp.stack(wks), jnp.stack(wvs))


def kernel(x_prompt, x_sample, cache_k, cache_v, state_conv, state_ssm, cache_win_k, cache_win_v, page_table,
           c_prompt, c_sample, w_mod, b_mod, g_norm, w_ffn_gate, w_ffn_up, w_ffn_down, w_in_even, lambda_qk,
           g_subln, conv_w, conv_b, dt_bias, a_log, d_skip, g_ssm_norm, w_out_even, w_in_odd, b_in_odd,
           attn_sinks, w_out_odd, b_out_odd):
    w = dict(g_norm=g_norm, w_ffn_gate=w_ffn_gate, w_ffn_up=w_ffn_up, w_ffn_down=w_ffn_down,
             w_in_even=w_in_even, lambda_qk=lambda_qk, g_subln=g_subln, conv_w=conv_w, conv_b=conv_b,
             dt_bias=dt_bias, a_log=a_log, d_skip=d_skip, g_ssm_norm=g_ssm_norm, w_out_even=w_out_even,
             w_in_odd=w_in_odd, b_in_odd=b_in_odd, attn_sinks=attn_sinks, w_out_odd=w_out_odd,
             b_out_odd=b_out_odd)
    bp, seq, d = x_prompt.shape
    bs, dec_seq, _ = x_sample.shape
    assert dec_seq == 1
    n_c = bp + bs
    pad = (-n_c) % 16
    c_all = jnp.concatenate([c_prompt, c_sample, jnp.zeros((pad, d), F32)], axis=0)
    mod = _modulation(c_all, w_mod, b_mod)
    mod_p = mod[:, :bp].reshape(DEPTH, bp, N_SUB, 3, d)
    mod_s = mod[:, bp:n_c].reshape(DEPTH, bs, N_SUB, 3, d)

    yp, kp, vp, cvp, ssp, wkp, wvp = _trunk(x_prompt.reshape(bp * seq, d), mod_p, w, nseq=bp, seq=seq,
                                            prompt=True, caches=None)
    caches = dict(even=(cache_k, cache_v, page_table, state_conv, state_ssm), odd=(cache_win_k, cache_win_v))
    ys, ksm, vsm, cvs, sss, wks, wvs = _trunk(x_sample.reshape(bs, d), mod_s, w, nseq=bs, seq=1,
                                              prompt=False, caches=caches)
    return (yp.reshape(bp, seq, d), ys.reshape(bs, 1, d), kp, vp, ksm, vsm, cvp, cvs, ssp, sss,
            wkp, wvp, wks, wvs)
```

```python
import functools
import math

import jax
import jax.numpy as jnp
from jax import lax
from jax.experimental import pallas as pl
from jax.experimental.pallas import tpu as pltpu

F32 = jnp.float32
BF16 = jnp.bfloat16

DEPTH = 4
N_SUB = 3
FFN_RES = 0.5
EPS = 1e-6
NEG = -1e30
PAGE = 128
DA_KV = 4
DA_REP = 2
DA_D = 64
DA_Q = DA_KV * DA_REP * 2 * DA_D
DA_KVD = DA_KV * 2 * DA_D
SSM_INNER = 1024
SSM_P = 64
SSM_H = 16
SSM_G = 2
SSM_N = 128
CONV_W = 4
CONV_DIM = SSM_INNER + 2 * SSM_G * SSM_N
CHUNK = 128
EVEN_MAIN = DA_Q + 2 * DA_KVD + SSM_INNER + CONV_DIM
SW_D = 64
SW_KV = 4
SW_REP = 8
SW_Q = SW_KV * SW_REP * SW_D
SW_KVD = SW_KV * SW_D
WINDOW = 128

SUBLANES = 8
VMEM_LIMIT = 56 * 1024 * 1024


def _cp(sem, limit=VMEM_LIMIT):
    return pltpu.CompilerParams(dimension_semantics=sem, vmem_limit_bytes=limit)


def _silu(x):
    return x / (1.0 + jnp.exp(-x))


def _softplus(x):
    return jnp.maximum(x, 0.0) + jnp.log(1.0 + jnp.exp(-jnp.abs(x)))


def _rms(x, g):
    return x * lax.rsqrt(jnp.mean(x * x, axis=-1, keepdims=True) + EPS) * g


def _dot(a, b):
    return jnp.dot(a.astype(BF16), b.astype(BF16), preferred_element_type=F32)


def _dot_nt(a, b):
    return lax.dot_general(a.astype(BF16), b.astype(BF16), (((1,), (1,)), ((), ())),
                           preferred_element_type=F32)


def _dot_tn(a, b):
    return lax.dot_general(a.astype(BF16), b.astype(BF16), (((0,), (0,)), ((), ())),
                           preferred_element_type=F32)


def _mod_kernel(c_ref, w_ref, b_ref, o_ref):
    h = _silu(c_ref[...])
    o_ref[0] = _dot(h, w_ref[0]) + b_ref[0]


def _modulation(c_all, w_mod, b_mod, tn=1024):
    rows, d = c_all.shape
    depth, _, n = w_mod.shape
    return pl.pallas_call(
        _mod_kernel,
        grid=(depth, n // tn),
        in_specs=[pl.BlockSpec((rows, d), lambda l, j: (0, 0)),
                  pl.BlockSpec((1, d, tn), lambda l, j: (l, 0, j)),
                  pl.BlockSpec((1, 1, tn), lambda l, j: (l, 0, j))],
        out_specs=pl.BlockSpec((1, rows, tn), lambda l, j: (l, 0, j)),
        out_shape=jax.ShapeDtypeStruct((depth, rows, n), F32),
        compiler_params=_cp(("parallel", "arbitrary")),
        name="modulation",
    )(c_all, w_mod, b_mod.reshape(depth, 1, n))


class _Group:
    def __init__(self, row0, nrows, seq):
        assert row0 % SUBLANES == 0
        self.row0, self.nrows, self.seq = row0, nrows, seq
        self.block_rows = nrows if seq is None else SUBLANES
        assert seq is not None or nrows % SUBLANES == 0
        assert seq is None or nrows <= SUBLANES

    def spec(self, layer, col, d):
        rb = self.row0 // self.block_rows
        return pl.BlockSpec((1, self.block_rows, d), lambda i, j: (layer, rb, col))

    def rows(self, ref, tm):
        if self.seq is None:
            return ref[0]
        return ref[0, pl.ds((pl.program_id(0) * tm) // self.seq, 1), :]


def _gn_spec(layer, d):
    return pl.BlockSpec((1, 2 * N_SUB, d), lambda i, j: (layer, 0, 0))


def _ffn_kernel(x_ref, sh_ref, sc_ref, gt_ref, gn_ref, wg_ref, wu_ref, wd_ref, o_ref, h_ref,
                *, nf, res_w, kpre, grp, tm):
    f = pl.program_id(1)

    @pl.when(f == 0)
    def _():
        xn = _rms(x_ref[...], gn_ref[0, kpre:kpre + 1, :])
        h_ref[...] = (xn * (1 + grp.rows(sc_ref, tm)) + grp.rows(sh_ref, tm)).astype(BF16)
        o_ref[...] = jnp.zeros(o_ref.shape, F32)

    h = h_ref[...]
    g = jnp.dot(h, wg_ref[0, 0].astype(BF16), preferred_element_type=F32)
    u = jnp.dot(h, wu_ref[0, 0].astype(BF16), preferred_element_type=F32)
    o_ref[...] += _dot(_silu(g) * u, wd_ref[0, 0])

    @pl.when(f == nf - 1)
    def _():
        yn = _rms(o_ref[...], gn_ref[0, kpre + 1:kpre + 2, :])
        o_ref[...] = x_ref[...] + (res_w * grp.rows(gt_ref, tm)) * yn


def _ffn(x, mod, g_norm, wg, wu, wd, l, s, sub, grp, *, tm, tf, single_buffer_x):
    m, d = x.shape
    nf = wg.shape[-1] // tf
    kern = functools.partial(_ffn_kernel, nf=nf, res_w=FFN_RES, kpre=2 * sub, grp=grp, tm=tm)
    x_kw = dict(pipeline_mode=pl.Buffered(1)) if single_buffer_x else {}
    return pl.pallas_call(
        kern,
        grid=(m // tm, nf),
        in_specs=[pl.BlockSpec((tm, d), lambda i, j: (i, 0), **x_kw),
                  grp.spec(l, 3 * sub + 0, d), grp.spec(l, 3 * sub + 1, d), grp.spec(l, 3 * sub + 2, d),
                  _gn_spec(l, d),
                  pl.BlockSpec((1, 1, d, tf), lambda i, j: (l, s, 0, j)),
                  pl.BlockSpec((1, 1, d, tf), lambda i, j: (l, s, 0, j)),
                  pl.BlockSpec((1, 1, tf, d), lambda i, j: (l, s, j, 0))],
        out_specs=pl.BlockSpec((tm, d), lambda i, j: (i, 0)),
        out_shape=jax.ShapeDtypeStruct((m, d), F32),
        scratch_shapes=[pltpu.VMEM((tm, d), BF16)],
        compiler_params=_cp(("parallel", "arbitrary")),
        name="ffn",
    )(x, mod, mod, mod, g_norm, wg, wu, wd)


def _inproj_kernel(*refs, transposed, has_bias, has_dt, grp, tm):
    refs = list(refs)
    x_ref, sh_ref, sc_ref, gn_ref, w_ref = refs[:5]
    rest = refs[5:]
    b_ref = rest.pop(0) if has_bias else None
    wdt_ref = rest.pop(0) if has_dt else None
    o_ref = rest.pop(0)
    dt_ref = rest.pop(0) if has_dt else None
    h_ref, wb_ref = rest
    i = pl.program_id(0)
    j = pl.program_id(1)

    @pl.when(j == 0)
    def _():
        xn = _rms(x_ref[...], gn_ref[0, 2:3, :])
        h = (xn * (1 + grp.rows(sc_ref, tm)) + grp.rows(sh_ref, tm)).astype(BF16)
        h_ref[...] = h
        if has_dt:
            dt_ref[...] = _dot_nt(h, wdt_ref[0])

    @pl.when(i == 0)
    def _():
        wb_ref[j] = w_ref[0].astype(BF16)

    w = wb_ref[j]
    if transposed:
        o = lax.dot_general(h_ref[...], w, (((1,), (1,)), ((), ())), preferred_element_type=F32)
    else:
        o = jnp.dot(h_ref[...], w, preferred_element_type=F32)
    o_ref[...] = o + b_ref[...] if has_bias else o


def _inproj(x, mod, g_norm, w, b, l, layer, n, grp, *, tm, tn, transposed, n_dt=0, single_buffer_x=False):
    m, d = x.shape
    nj = n // tn
    has_bias = b is not None
    has_dt = n_dt > 0
    wcol = lambda i, j: jnp.where(i == 0, j, nj - 1)
    if transposed:
        w_spec = pl.BlockSpec((1, tn, d), lambda i, j: (layer, wcol(i, j), 0))
        wb_shape = (nj, tn, d)
    else:
        w_spec = pl.BlockSpec((1, d, tn), lambda i, j: (layer, 0, wcol(i, j)))
        wb_shape = (nj, d, tn)
    x_kw = dict(pipeline_mode=pl.Buffered(1)) if single_buffer_x else {}
    in_specs = [pl.BlockSpec((tm, d), lambda i, j: (i, 0), **x_kw),
                grp.spec(l, 3, d), grp.spec(l, 4, d), _gn_spec(l, d), w_spec]
    args = [x, mod, mod, g_norm, w]
    if has_bias:
        in_specs.append(pl.BlockSpec((1, tn), lambda i, j: (0, j)))
        args.append(b)
    out_specs = [pl.BlockSpec((tm, tn), lambda i, j: (i, j))]
    out_shape = [jax.ShapeDtypeStruct((m, n), F32)]
    if has_dt:
        assert transposed and n % n_dt == 0
        in_specs.append(pl.BlockSpec((1, n_dt, d), lambda i, j: (layer, n // n_dt, 0)))
        args.append(w)
        out_specs.append(pl.BlockSpec((tm, n_dt), lambda i, j: (i, 0)))
        out_shape.append(jax.ShapeDtypeStruct((m, n_dt), F32))
    out = pl.pallas_call(
        functools.partial(_inproj_kernel, transposed=transposed, has_bias=has_bias, has_dt=has_dt,
                          grp=grp, tm=tm),
        grid=(m // tm, nj),
        in_specs=in_specs,
        out_specs=out_specs,
        out_shape=out_shape,
        scratch_shapes=[pltpu.VMEM((tm, d), BF16), pltpu.VMEM(wb_shape, BF16)],
        compiler_params=_cp(("arbitrary", "arbitrary")),
        name="inproj",
    )(*args)
    return out if has_dt else out[0]


def _outproj_kernel(*refs, bounds, has_bias, grp, tm):
    refs = list(refs)
    x_ref = refs.pop(0)
    y_refs = [refs.pop(0) for _ in bounds]
    gt_ref, gn_ref, w_ref = refs[:3]
    rest = refs[3:]
    b_ref = rest.pop(0) if has_bias else None
    o_ref, wb_ref = rest
    nk = bounds[-1][1]
    i = pl.program_id(0)
    k = pl.program_id(1)

    @pl.when(i == 0)
    def _():
        wb_ref[k] = w_ref[0].astype(BF16)

    @pl.when(k == 0)
    def _():
        o_ref[...] = jnp.zeros(o_ref.shape, F32)

    for y_ref, (lo, hi) in zip(y_refs, bounds):
        @pl.when(jnp.logical_and(k >= lo, k < hi))
        def _(y_ref=y_ref):
            o_ref[...] += jnp.dot(y_ref[...].astype(BF16), wb_ref[k], preferred_element_type=F32)

    @pl.when(k == nk - 1)
    def _():
        y = o_ref[...] + b_ref[...] if has_bias else o_ref[...]
        o_ref[...] = x_ref[...] + grp.rows(gt_ref, tm) * _rms(y, gn_ref[0, 3:4, :])


def _outproj(x, ys, mod, g_norm, w, b, l, layer, grp, *, tm, tk):
    m, d = x.shape
    has_bias = b is not None
    bounds, lo = [], 0
    for y in ys:
        bounds.append((lo, lo + y.shape[1] // tk))
        lo = bounds[-1][1]
    nk = lo
    assert nk * tk == w.shape[1]

    def y_spec(lo, hi):
        return pl.BlockSpec((tm, tk), lambda i, k: (i, jnp.clip(k - lo, 0, hi - lo - 1)))

    in_specs = ([pl.BlockSpec((tm, d), lambda i, k: (i, 0))]
                + [y_spec(lo, hi) for lo, hi in bounds]
                + [grp.spec(l, 5, d), _gn_spec(l, d),
                   pl.BlockSpec((1, tk, d), lambda i, k: (layer, jnp.where(i == 0, k, nk - 1), 0))])
    args = [x, *ys, mod, g_norm, w]
    if has_bias:
        in_specs.append(pl.BlockSpec((1, d), lambda i, k: (0, 0)))
        args.append(b)
    return pl.pallas_call(
        functools.partial(_outproj_kernel, bounds=tuple(bounds), has_bias=has_bias, grp=grp, tm=tm),
        grid=(m // tm, nk),
        in_specs=in_specs,
        out_specs=pl.BlockSpec((tm, d), lambda i, k: (i, 0)),
        out_shape=jax.ShapeDtypeStruct((m, d), F32),
        scratch_shapes=[pltpu.VMEM((nk, tk, d), BF16)],
        compiler_params=_cp(("arbitrary", "arbitrary")),
        name="outproj",
    )(*args)


def _lambda(lp, lam_init):
    a = jnp.sum(lp[0:1] * lp[1:2], axis=-1, keepdims=True)
    b = jnp.sum(lp[2:3] * lp[3:4], axis=-1, keepdims=True)
    return jnp.exp(a) - jnp.exp(b) + lam_init


def _da_prompt_kernel(q_ref, k_ref, v_ref, lp_ref, gs_ref, o_ref, *, tq, lam_init):
    i = pl.program_id(2)
    lam = _lambda(lp_ref[...], lam_init)
    q = q_ref[...] * (DA_D ** -0.5)
    low = lax.broadcasted_iota(jnp.int32, (tq, 2 * DA_D), 1) < DA_D
    heads = (q[:, :2 * DA_D], q[:, 2 * DA_D:])
    qp = (jnp.concatenate([jnp.where(low, h, 0.0) for h in heads], axis=0).astype(BF16),
          jnp.concatenate([jnp.where(low, 0.0, h) for h in heads], axis=0).astype(BF16))
    key = lax.broadcasted_iota(jnp.int32, (tq, 2 * tq), 0)
    col = lax.broadcasted_iota(jnp.int32, (tq, 2 * tq), 1)
    causal_bias = jnp.where(key <= jnp.where(col >= tq, col - tq, col), 0.0, NEG)

    def step(j, carry, diagonal):
        off = pl.multiple_of(j * tq, tq)
        kb = k_ref[pl.ds(off, tq), :].astype(BF16)
        vb = v_ref[pl.ds(off, tq), :].astype(BF16)
        out = []
        for mi in range(2):
            m_old, l_old, acc = carry[3 * mi:3 * mi + 3]
            s = lax.dot_general(kb, qp[mi], (((1,), (1,)), ((), ())), preferred_element_type=F32)
            if diagonal:
                s = s + causal_bias
            m_new = jnp.maximum(m_old, jnp.max(s, axis=0, keepdims=True))
            alpha = jnp.exp(m_old - m_new)
            p = jnp.exp(s - m_new)
            l_new = alpha * l_old + jnp.sum(p, axis=0, keepdims=True)
            acc = alpha * acc + lax.dot_general(vb, p.astype(BF16), (((0,), (0,)), ((), ())),
                                                preferred_element_type=F32)
            out += [m_new, l_new, acc]
        return tuple(out)

    m0 = jnp.full((1, 2 * tq), NEG, F32)
    l0 = jnp.zeros((1, 2 * tq), F32)
    a0 = jnp.zeros((2 * DA_D, 2 * tq), F32)
    res = lax.fori_loop(0, i, lambda j, c: step(j, c, False), (m0, l0, a0, m0, l0, a0))
    res = step(i, res, True)
    a = res[2] / res[1] - lam * (res[5] / res[4])
    a = a * lax.rsqrt(jnp.mean(a * a, axis=0, keepdims=True) + EPS) * gs_ref[...] * (1 - lam_init)
    a = a.T
    o_ref[:, 0:128] = a[:tq]
    o_ref[:, 128:256] = a[tq:]


def _da_prompt(proj, lp, gs, nseq, seq, lam_init, tq=256):
    m = proj.shape[0]
    tq = min(tq, seq)
    nq = seq // tq
    kcol = DA_Q // 128
    vcol = (DA_Q + DA_KVD) // 128
    return pl.pallas_call(
        functools.partial(_da_prompt_kernel, tq=tq, lam_init=lam_init),
        grid=(nseq, DA_KV, nq),
        in_specs=[pl.BlockSpec((tq, 256), lambda b, g, i: (b * nq + i, g)),
                  pl.BlockSpec((seq, 128), lambda b, g, i: (b, kcol + g)),
                  pl.BlockSpec((seq, 128), lambda b, g, i: (b, vcol + g)),
                  pl.BlockSpec((4, DA_D), lambda b, g, i: (0, 0)),
                  pl.BlockSpec((2 * DA_D, 1), lambda b, g, i: (0, 0))],
        out_specs=pl.BlockSpec((tq, 256), lambda b, g, i: (b * nq + i, g)),
        out_shape=jax.ShapeDtypeStruct((m, DA_Q), F32),
        compiler_params=_cp(("parallel", "parallel", "arbitrary")),
        name="diff_attn_prompt",
    )(proj, proj, proj, lp, gs.reshape(-1, 1))


def _da_decode_kernel(pt_ref, qm_ref, kn_ref, vn_ref, lp_ref, gs_ref, *refs, pc, nc, lam_init):
    del pt_ref
    k_refs = refs[:pc]
    v_refs = refs[pc:2 * pc]
    o_ref, m_ref, l_ref, acc_ref, kb_ref, vb_ref = refs[2 * pc:]
    c = pl.program_id(1)

    @pl.when(c == 0)
    def _():
        m_ref[...] = jnp.full(m_ref.shape, NEG, F32)
        l_ref[...] = jnp.zeros(l_ref.shape, F32)
        acc_ref[...] = jnp.zeros(acc_ref.shape, F32)

    rows = PAGE * DA_KV
    for t in range(pc):
        kb_ref[t * rows:(t + 1) * rows, :] = k_refs[t][0, 0].astype(BF16)
        vb_ref[t * rows:(t + 1) * rows, :] = v_refs[t][0, 0].astype(BF16)
    qm = qm_ref[0] * (DA_D ** -0.5)
    s = _dot_nt(qm, kb_ref[...])
    row_g = (lax.broadcasted_iota(jnp.int32, s.shape, 0) // DA_REP) & (DA_KV - 1)
    col_g = lax.broadcasted_iota(jnp.int32, s.shape, 1) & (DA_KV - 1)
    s = jnp.where(row_g == col_g, s, NEG)
    m_old = m_ref[...]
    m_new = jnp.maximum(m_old, jnp.max(s, axis=-1, keepdims=True))
    alpha = jnp.exp(m_old - m_new)
    p = jnp.exp(s - m_new)
    l_ref[...] = alpha * l_ref[...] + jnp.sum(p, axis=-1, keepdims=True)
    acc_ref[...] = alpha * acc_ref[...] + jnp.dot(p.astype(BF16), vb_ref[...], preferred_element_type=F32)
    m_ref[...] = m_new

    @pl.when(c == nc - 1)
    def _():
        lam = _lambda(lp_ref[...], lam_init)
        s_new = jnp.sum(qm * kn_ref[0], axis=-1, keepdims=True)
        m_old = m_ref[...]
        m_fin = jnp.maximum(m_old, s_new)
        alpha = jnp.exp(m_old - m_fin)
        p_new = jnp.exp(s_new - m_fin)
        l_fin = alpha * l_ref[...] + p_new
        acc = alpha * acc_ref[...] + p_new * vn_ref[0]
        o = acc / l_fin
        a = o[0:8] - lam * o[8:16]
        o_ref[0] = _rms(a, gs_ref[...]) * (1 - lam_init)


def _da_decode(q, k_new, v_new, lp, gs, cache_k, cache_v, page_table, j, lam_init, pc=16):
    nb = q.shape[0]
    n_pages = page_table.shape[1]
    nc = n_pages // pc
    rows = PAGE * DA_KV
    ck = cache_k.reshape(cache_k.shape[0], cache_k.shape[1], rows, 2 * DA_D)
    cv = cache_v.reshape(cache_v.shape[0], cache_v.shape[1], rows, 2 * DA_D)
    qt = q.reshape(nb, DA_KV, DA_REP, 2, DA_D).transpose(0, 3, 1, 2, 4)
    own_m = jnp.eye(2, dtype=bool)[None, :, None, None, :, None]
    qm = jnp.where(own_m, qt[:, :, :, :, None, :], 0.0).reshape(nb, 16, 2 * DA_D)
    rep = lambda a: jnp.broadcast_to(a.reshape(nb, 1, DA_KV, 1, 2 * DA_D),
                                     (nb, 2, DA_KV, DA_REP, 2 * DA_D)).reshape(nb, 16, 2 * DA_D)

    def page_spec(t):
        return pl.BlockSpec((1, 1, rows, 2 * DA_D), lambda b, c, pt: (j, pt[b, c * pc + t], 0, 0))

    row_spec = pl.BlockSpec((1, 16, 2 * DA_D), lambda b, c, pt: (b, 0, 0))
    grid_spec = pltpu.PrefetchScalarGridSpec(
        num_scalar_prefetch=1,
        grid=(nb, nc),
        in_specs=[row_spec, row_spec, row_spec,
                  pl.BlockSpec((4, DA_D), lambda b, c, pt: (0, 0)),
                  pl.BlockSpec((1, 2 * DA_D), lambda b, c, pt: (0, 0))]
                 + [page_spec(t) for t in range(pc)] + [page_spec(t) for t in range(pc)],
        out_specs=pl.BlockSpec((1, 8, 128), lambda b, c, pt: (b, 0, 0)),
        scratch_shapes=[pltpu.VMEM((16, 1), F32), pltpu.VMEM((16, 1), F32),
                        pltpu.VMEM((16, 2 * DA_D), F32),
                        pltpu.VMEM((pc * rows, 2 * DA_D), BF16), pltpu.VMEM((pc * rows, 2 * DA_D), BF16)])
    out = pl.pallas_call(
        functools.partial(_da_decode_kernel, pc=pc, nc=nc, lam_init=lam_init),
        grid_spec=grid_spec,
        out_shape=jax.ShapeDtypeStruct((nb, 8, 128), F32),
        compiler_params=_cp(("parallel", "arbitrary")),
        name="diff_attn_decode",
    )(page_table, qm, rep(k_new), rep(v_new), lp, gs.reshape(1, -1), *([ck] * pc), *([cv] * pc))
    return out.reshape(nb, DA_Q)


def _split3(x):
    x1 = x.astype(BF16)
    r1 = x - x1.astype(F32)
    x2 = r1.astype(BF16)
    x3 = (r1 - x2.astype(F32)).astype(BF16)
    return x1, x2, x3


def _group_norm_gate(y, z, gn):
    yz = y * _silu(z)
    w = SSM_INNER // SSM_G
    parts = []
    for g in range(SSM_G):
        seg = yz[:, g * w:(g + 1) * w]
        parts.append(seg * lax.rsqrt(jnp.mean(seg * seg, axis=-1, keepdims=True) + EPS))
    return jnp.concatenate(parts, axis=-1) * gn


def _ssd_prompt_kernel(xbc_ref, z_ref, dt_ref, dtt_ref, cw_ref, cb_ref, dtb_ref, dtbt_ref,
                       alog_ref, alogt_ref, dskip_ref, gn_ref, y_ref, st_ref, ext_ref, h_ref, *, nchunk):
    c = pl.program_id(1)
    q = CHUNK

    @pl.when(c == 0)
    def _():
        ext_ref[0:8, :] = jnp.zeros((8, CONV_DIM), F32)
        h_ref[...] = jnp.zeros(h_ref.shape, F32)

    @pl.when(c > 0)
    def _():
        ext_ref[0:8, :] = ext_ref[q:q + 8, :]

    ext_ref[8:q + 8, :] = xbc_ref[...]
    cw = cw_ref[...]
    acc = ext_ref[5:q + 5, :] * cw[0:1]
    for kk in range(1, CONV_W):
        acc = acc + ext_ref[5 + kk:q + 5 + kk, :] * cw[kk:kk + 1]
    xa = _silu(acc + cb_ref[...])
    xs = xa[:, :SSM_INNER]
    bm = xa[:, SSM_INNER:SSM_INNER + SSM_G * SSM_N].astype(BF16)
    cm = xa[:, SSM_INNER + SSM_G * SSM_N:].astype(BF16)

    dt = _softplus(dt_ref[...] + dtb_ref[...])
    dtt = _softplus(dtt_ref[...] + dtbt_ref[...])
    a = dt * (-jnp.exp(alog_ref[...]))
    at = dtt * (-jnp.exp(alogt_ref[...]))
    ri = lax.broadcasted_iota(jnp.int32, (q, q), 0)
    ci = lax.broadcasted_iota(jnp.int32, (q, q), 1)
    lower = ri >= ci
    tril = jnp.where(lower, 1.0, 0.0).astype(BF16)
    triu = jnp.where(ri <= ci, 1.0, 0.0).astype(BF16)
    cs = sum(jnp.dot(tril, part, preferred_element_type=F32) for part in _split3(a))
    cst = sum(jnp.dot(part, triu, preferred_element_type=F32) for part in _split3(at))
    tot = cs[q - 1:q, :]
    e_cs = jnp.exp(cs)
    e_dec = jnp.exp(tot - cs)
    e_tot = jnp.exp(tot)
    dskip = dskip_ref[...]

    ys = []
    for g in range(SSM_G):
        bg = bm[:, g * SSM_N:(g + 1) * SSM_N]
        cg = cm[:, g * SSM_N:(g + 1) * SSM_N]
        cb = lax.dot_general(cg, bg, (((1,), (1,)), ((), ())), preferred_element_type=F32)
        for r in range(SSM_H // SSM_G):
            h = g * (SSM_H // SSM_G) + r
            lmat = jnp.exp(jnp.where(lower, cs[:, h:h + 1] - cst[h:h + 1, :], NEG))
            x_h = xs[:, h * SSM_P:(h + 1) * SSM_P]
            xdt = x_h * dt[:, h:h + 1]
            y_diag = _dot(cb * lmat, xdt)
            hp = h_ref[h]
            y_off = lax.dot_general(cg, hp.astype(BF16), (((1,), (1,)), ((), ())),
                                    preferred_element_type=F32) * e_cs[:, h:h + 1]
            upd = lax.dot_general((xdt * e_dec[:, h:h + 1]).astype(BF16), bg, (((0,), (0,)), ((), ())),
                                  preferred_element_type=F32)
            h_ref[h] = e_tot[:, h:h + 1] * hp + upd
            ys.append(y_diag + y_off + dskip[:, h:h + 1] * x_h)
    y = jnp.concatenate(ys, axis=-1)
    y_ref[...] = _group_norm_gate(y, z_ref[...], gn_ref[...])

    @pl.when(c == nchunk - 1)
    def _():
        st_ref[0] = h_ref[...]


def _ssd_prompt(proj, dt_raw, cw, cb, dtb, alog, dskip, gn, nseq, seq):
    m = proj.shape[0]
    nchunk = seq // CHUNK
    zcol = (DA_Q + 2 * DA_KVD) // SSM_INNER
    xcol = (DA_Q + 2 * DA_KVD + SSM_INNER) // CONV_DIM
    const = lambda b, c: (0, 0)
    y, st = pl.pallas_call(
        functools.partial(_ssd_prompt_kernel, nchunk=nchunk),
        grid=(nseq, nchunk),
        in_specs=[pl.BlockSpec((CHUNK, CONV_DIM), lambda b, c: (b * nchunk + c, xcol)),
                  pl.BlockSpec((CHUNK, SSM_INNER), lambda b, c: (b * nchunk + c, zcol)),
                  pl.BlockSpec((CHUNK, SSM_H), lambda b, c: (b * nchunk + c, 0)),
                  pl.BlockSpec((SSM_H, CHUNK), lambda b, c: (0, b * nchunk + c)),
                  pl.BlockSpec((CONV_W, CONV_DIM), const),
                  pl.BlockSpec((1, CONV_DIM), const),
                  pl.BlockSpec((1, SSM_H), const),
                  pl.BlockSpec((SSM_H, 1), const),
                  pl.BlockSpec((1, SSM_H), const),
                  pl.BlockSpec((SSM_H, 1), const),
                  pl.BlockSpec((1, SSM_H), const),
                  pl.BlockSpec((1, SSM_INNER), const)],
        out_specs=[pl.BlockSpec((CHUNK, SSM_INNER), lambda b, c: (b * nchunk + c, 0)),
                   pl.BlockSpec((1, SSM_H, SSM_P, SSM_N), lambda b, c: (b, 0, 0, 0))],
        out_shape=[jax.ShapeDtypeStruct((m, SSM_INNER), F32),
                   jax.ShapeDtypeStruct((nseq, SSM_H, SSM_P, SSM_N), F32)],
        scratch_shapes=[pltpu.VMEM((CHUNK + 8, CONV_DIM), F32),
                        pltpu.VMEM((SSM_H, SSM_P, SSM_N), F32)],
        compiler_params=_cp(("parallel", "arbitrary")),
        name="ssd_prompt",
    )(proj, proj, dt_raw, dt_raw.T, cw, cb.reshape(1, -1), dtb.reshape(1, -1), dtb.reshape(-1, 1),
      alog.reshape(1, -1), alog.reshape(-1, 1), dskip.reshape(1, -1), gn.reshape(1, -1))
    return y, st


def _ssd_dec_pre_kernel(xbc_ref, cst_ref, dt_ref, cw_ref, cb_ref, dtb_ref, xa_ref, dts_ref):
    cw = cw_ref[...]
    acc = cst_ref[0] * cw[0:1]
    for kk in range(1, CONV_W - 1):
        acc = acc + cst_ref[kk] * cw[kk:kk + 1]
    acc = acc + xbc_ref[...] * cw[CONV_W - 1:CONV_W]
    xa_ref[...] = _silu(acc + cb_ref[...])
    dts_ref[...] = _softplus(dt_ref[...] + dtb_ref[...])


def _ssd_dec_state_kernel(xt_ref, dts_ref, alog_ref, b_ref, c_ref, dskip_ref, h_ref, yt_ref, hn_ref):
    xt = xt_ref[0]
    dts = dts_ref[0]
    a = dts * (-jnp.exp(alog_ref[...]))
    e_a = jnp.exp(a)
    dskip = dskip_ref[...]
    cols = []
    for h in range(SSM_H):
        g = h // (SSM_H // SSM_G)
        xcol = xt[:, h:h + 1]
        brow = b_ref[0][:, g * SSM_N:(g + 1) * SSM_N]
        crow = c_ref[0][:, g * SSM_N:(g + 1) * SSM_N]
        hn = e_a[:, h:h + 1] * h_ref[0, h] + (xcol * dts[:, h:h + 1]) * brow
        hn_ref[0, h] = hn
        cols.append(jnp.sum(hn * crow, axis=-1, keepdims=True) + dskip[:, h:h + 1] * xcol)
    yt_ref[0] = jnp.concatenate(cols, axis=-1)


def _ssd_dec_post_kernel(y_ref, z_ref, gn_ref, o_ref):
    o_ref[...] = _group_norm_gate(y_ref[...], z_ref[...], gn_ref[...])


def _ssd_decode(proj, dt_raw, conv_state, ssm_state, cw, cb, dtb, alog, dskip, gn):
    nb = proj.shape[0]
    zcol = (DA_Q + 2 * DA_KVD) // SSM_INNER
    xcol = (DA_Q + 2 * DA_KVD + SSM_INNER) // CONV_DIM
    full = lambda shape: pl.BlockSpec(shape, lambda i: (0,) * len(shape))
    xa, dts = pl.pallas_call(
        _ssd_dec_pre_kernel,
        grid=(1,),
        in_specs=[pl.BlockSpec((nb, CONV_DIM), lambda i: (0, xcol)),
                  full((CONV_W - 1, nb, CONV_DIM)),
                  full((nb, SSM_H)),
                  full((CONV_W, CONV_DIM)),
                  full((1, CONV_DIM)),
                  full((1, SSM_H))],
        out_specs=[full((nb, CONV_DIM)), full((nb, SSM_H))],
        out_shape=[jax.ShapeDtypeStruct((nb, CONV_DIM), F32), jax.ShapeDtypeStruct((nb, SSM_H), F32)],
        compiler_params=_cp(("arbitrary",)),
        name="ssd_decode_pre",
    )(proj, conv_state.transpose(1, 0, 2), dt_raw, cw, cb.reshape(1, -1), dtb.reshape(1, -1))
    xt = xa[:, :SSM_INNER].reshape(nb, SSM_H, SSM_P).transpose(0, 2, 1)
    bmat = xa[:, SSM_INNER:SSM_INNER + SSM_G * SSM_N].reshape(nb, 1, SSM_G * SSM_N)
    cmat = xa[:, SSM_INNER + SSM_G * SSM_N:].reshape(nb, 1, SSM_G * SSM_N)
    yt, h_new = pl.pallas_call(
        _ssd_dec_state_kernel,
        grid=(nb,),
        in_specs=[pl.BlockSpec((1, SSM_P, SSM_H), lambda b: (b, 0, 0)),
                  pl.BlockSpec((1, 1, SSM_H), lambda b: (b, 0, 0)),
                  pl.BlockSpec((1, SSM_H), lambda b: (0, 0)),
                  pl.BlockSpec((1, 1, SSM_G * SSM_N), lambda b: (b, 0, 0)),
                  pl.BlockSpec((1, 1, SSM_G * SSM_N), lambda b: (b, 0, 0)),
                  pl.BlockSpec((1, SSM_H), lambda b: (0, 0)),
                  pl.BlockSpec((1, SSM_H, SSM_P, SSM_N), lambda b: (b, 0, 0, 0))],
        out_specs=[pl.BlockSpec((1, SSM_P, SSM_H), lambda b: (b, 0, 0)),
                   pl.BlockSpec((1, SSM_H, SSM_P, SSM_N), lambda b: (b, 0, 0, 0))],
        out_shape=[jax.ShapeDtypeStruct((nb, SSM_P, SSM_H), F32),
                   jax.ShapeDtypeStruct(ssm_state.shape, F32)],
        compiler_params=_cp(("parallel",)),
        name="ssd_decode_state",
    )(xt, dts.reshape(nb, 1, SSM_H), alog.reshape(1, -1), bmat, cmat, dskip.reshape(1, -1), ssm_state)
    y = yt.transpose(0, 2, 1).reshape(nb, SSM_INNER)
    ob = pl.pallas_call(
        _ssd_dec_post_kernel,
        grid=(1,),
        in_specs=[full((nb, SSM_INNER)),
                  pl.BlockSpec((nb, SSM_INNER), lambda i: (0, zcol)),
                  full((1, SSM_INNER))],
        out_specs=full((nb, SSM_INNER)),
        out_shape=jax.ShapeDtypeStruct((nb, SSM_INNER), F32),
        compiler_params=_cp(("arbitrary",)),
        name="ssd_decode_post",
    )(y, proj, gn.reshape(1, -1))
    return ob, h_new


def _swa_prompt_kernel(sink_ref, q_ref, kp_ref, kc_ref, vp_ref, vc_ref, o_ref):
    i = pl.program_id(1)
    w = WINDOW
    pair_w = 2 * SW_D
    sinks = sink_ref[...]
    q = q_ref[...] * (SW_D ** -0.5)
    kk = jnp.concatenate([kp_ref[...], kc_ref[...]], axis=0)
    vv = jnp.concatenate([vp_ref[...], vc_ref[...]], axis=0).astype(BF16)
    low_k = lax.broadcasted_iota(jnp.int32, (2 * w, pair_w), 1) < SW_D
    low_q = lax.broadcasted_iota(jnp.int32, (w, pair_w), 1) < SW_D
    key = lax.broadcasted_iota(jnp.int32, (2 * w, w), 0)
    t = key - lax.broadcasted_iota(jnp.int32, (2 * w, w), 1)
    keymin = jnp.where(i > 0, 0, w)
    bias1 = jnp.where(t >= 1, jnp.where(t <= w, jnp.where(key >= keymin, 0.0, NEG), NEG), NEG)
    bias = jnp.concatenate([bias1] * SW_REP, axis=1)
    outs = []
    for g in range(SW_KV):
        pair = g // 2
        k2 = kk[:, pair * pair_w:(pair + 1) * pair_w]
        k2r = pltpu.roll(k2, SW_D, axis=1)
        kdup = (jnp.where(low_k, k2, k2r) if g % 2 == 0 else jnp.where(low_k, k2r, k2)).astype(BF16)
        qs = []
        for r in range(SW_REP):
            h = g * SW_REP + r
            q2 = q[:, (h // 2) * pair_w:(h // 2 + 1) * pair_w]
            qs.append(jnp.where(low_q, q2, 0.0) if h % 2 == 0 else jnp.where(low_q, 0.0, q2))
        qst = jnp.concatenate(qs, axis=0).astype(BF16)
        s = lax.dot_general(kdup, qst, (((1,), (1,)), ((), ())), preferred_element_type=F32) + bias
        sink = jnp.concatenate([jnp.broadcast_to(sinks[:, g * SW_REP + r:g * SW_REP + r + 1], (1, w))
                                for r in range(SW_REP)], axis=1)
        mx = jnp.maximum(jnp.max(s, axis=0, keepdims=True), sink)
        p = jnp.exp(s - mx)
        den = jnp.sum(p, axis=0, keepdims=True) + jnp.exp(sink - mx)
        ot = lax.dot_general(vv[:, pair * pair_w:(pair + 1) * pair_w], p.astype(BF16),
                             (((0,), (0,)), ((), ())), preferred_element_type=F32)
        ot = ot[(g % 2) * SW_D:(g % 2 + 1) * SW_D, :] / den
        for pr in range(SW_REP // 2):
            blk = jnp.concatenate([ot[:, (2 * pr) * w:(2 * pr + 1) * w],
                                   ot[:, (2 * pr + 1) * w:(2 * pr + 2) * w]], axis=0)
            outs.append(blk.T)
    o_ref[...] = jnp.concatenate(outs, axis=1)


def _swa_prompt(proj, sinks, nseq, seq):
    m = proj.shape[0]
    nb = seq // WINDOW
    kcol = SW_Q // SW_KVD
    vcol = kcol + 1
    prev = lambda b, i: (b * nb + jnp.maximum(i - 1, 0), kcol)
    prev_v = lambda b, i: (b * nb + jnp.maximum(i - 1, 0), vcol)
    return pl.pallas_call(
        _swa_prompt_kernel,
        grid=(nseq, nb),
        in_specs=[pl.BlockSpec((1, SW_KV * SW_REP), lambda b, i: (0, 0)),
                  pl.BlockSpec((WINDOW, SW_Q), lambda b, i: (b * nb + i, 0)),
                  pl.BlockSpec((WINDOW, SW_KVD), prev),
                  pl.BlockSpec((WINDOW, SW_KVD), lambda b, i: (b * nb + i, kcol)),
                  pl.BlockSpec((WINDOW, SW_KVD), prev_v),
                  pl.BlockSpec((WINDOW, SW_KVD), lambda b, i: (b * nb + i, vcol))],
        out_specs=pl.BlockSpec((WINDOW, SW_Q), lambda b, i: (b * nb + i, 0)),
        out_shape=jax.ShapeDtypeStruct((m, SW_Q), F32),
        compiler_params=_cp(("parallel", "arbitrary")),
        name="swa_prompt",
    )(sinks.reshape(1, -1), proj, proj, proj, proj, proj)


def _swa_decode_kernel(q_ref, kn_ref, vn_ref, sink_ref, kb_ref, vb_ref, o_ref):
    q = q_ref[0] * (SW_D ** -0.5)
    nh = SW_KV * SW_REP
    q4 = jnp.concatenate([q] * SW_KV, axis=-1)
    rowg = lax.broadcasted_iota(jnp.int32, (nh, SW_KVD), 0) // SW_REP
    colg = lax.broadcasted_iota(jnp.int32, (nh, SW_KVD), 1) // SW_D
    own = rowg == colg
    qbd = jnp.where(own, q4, 0.0)
    s = _dot(qbd, kb_ref[0])
    col = lax.broadcasted_iota(jnp.int32, (nh, WINDOW), 1)
    s = jnp.where(col >= 1, s, NEG)
    s_new = jnp.sum(qbd * kn_ref[0], axis=-1, keepdims=True)
    sink = sink_ref[...]
    mx = jnp.maximum(jnp.maximum(jnp.max(s, axis=-1, keepdims=True), s_new), sink)
    p = jnp.exp(s - mx)
    p_new = jnp.exp(s_new - mx)
    den = jnp.sum(p, axis=-1, keepdims=True) + p_new + jnp.exp(sink - mx)
    o = (_dot_nt(p, vb_ref[0]) + p_new * vn_ref[0]) / den
    o = jnp.where(own, o, 0.0)
    out = o[:, 0:SW_D]
    for g in range(1, SW_KV):
        out = out + o[:, g * SW_D:(g + 1) * SW_D]
    o_ref[0] = out


def _swa_decode(q, k_new, v_new, sinks, win_k, win_v):
    nb = q.shape[0]
    nh = SW_KV * SW_REP
    out = pl.pallas_call(
        _swa_decode_kernel,
        grid=(nb,),
        in_specs=[pl.BlockSpec((1, nh, SW_D), lambda b: (b, 0, 0)),
                  pl.BlockSpec((1, 1, SW_KVD), lambda b: (b, 0, 0)),
                  pl.BlockSpec((1, 1, SW_KVD), lambda b: (b, 0, 0)),
                  pl.BlockSpec((nh, 1), lambda b: (0, 0)),
                  pl.BlockSpec((1, SW_KVD, WINDOW), lambda b: (b, 0, 0)),
                  pl.BlockSpec((1, SW_KVD, WINDOW), lambda b: (b, 0, 0))],
        out_specs=pl.BlockSpec((1, nh, SW_D), lambda b: (b, 0, 0)),
        out_shape=jax.ShapeDtypeStruct((nb, nh, SW_D), F32),
        compiler_params=_cp(("parallel",)),
        name="swa_decode",
    )(q.reshape(nb, nh, SW_D), k_new.reshape(nb, 1, SW_KVD), v_new.reshape(nb, 1, SW_KVD),
      sinks.reshape(nh, 1), win_k.transpose(0, 2, 3, 1).reshape(nb, SW_KVD, WINDOW),
      win_v.transpose(0, 2, 3, 1).reshape(nb, SW_KVD, WINDOW))
    return out.reshape(nb, SW_Q)


def _trunk(x, mod, w, grp, *, nseq, seq, caches):
    prompt = caches is None
    if prompt:
        tm_ffn, tm_in, tm_out, tf = min(1024, seq), min(1024, seq), min(512, seq), 256
    else:
        tm_ffn = tm_in = tm_out = nseq
        tf = 512
    tn, tk = 512, 512
    g_norm = w['g_norm']
    ffn_w = (w['w_ffn_gate'], w['w_ffn_up'], w['w_ffn_down'])
    ks, vs, convs, ssms, wks, wvs = [], [], [], [], [], []
    for l in range(DEPTH):
        j = l // 2
        x = _ffn(x, mod, g_norm, *ffn_w, l, 0, 0, grp, tm=tm_ffn, tf=tf, single_buffer_x=prompt)
        if l % 2 == 0:
            lam_init = 0.8 - 0.6 * math.exp(-0.3 * l)
            w_in_t = w['w_in_even'].transpose(0, 2, 1)
            proj, dt_raw = _inproj(x, mod, g_norm, w_in_t, None, l, j, EVEN_MAIN, grp, tm=tm_in, tn=tn,
                                   transposed=True, n_dt=SSM_H, single_buffer_x=prompt)
            lp = w['lambda_qk'][j]
            gs = w['g_subln'][j]
            ssm_w = (w['conv_w'][j], w['conv_b'][j], w['dt_bias'][j], w['a_log'][j], w['d_skip'][j],
                     w['g_ssm_norm'][j])
            p3 = proj.reshape(nseq, seq, EVEN_MAIN)
            if prompt:
                oa = _da_prompt(proj, lp, gs, nseq, seq, lam_init)
                ob, ssm_new = _ssd_prompt(proj, dt_raw, *ssm_w, nseq, seq)
                conv_new = p3[:, seq - (CONV_W - 1):, EVEN_MAIN - CONV_DIM:]
            else:
                cache_k, cache_v, page_table, conv_state, ssm_state = caches['even']
                oa = _da_decode(proj[:, :DA_Q], proj[:, DA_Q:DA_Q + DA_KVD],
                                proj[:, DA_Q + DA_KVD:DA_Q + 2 * DA_KVD], lp, gs, cache_k, cache_v,
                                page_table, j, lam_init)
                ob, ssm_new = _ssd_decode(proj, dt_raw, conv_state[j], ssm_state[j], *ssm_w)
                conv_new = jnp.concatenate([conv_state[j][:, 1:], p3[:, :, EVEN_MAIN - CONV_DIM:]], axis=1)
            ks.append(p3[:, :, DA_Q:DA_Q + DA_KVD].reshape(nseq, seq, DA_KV, 2 * DA_D))
            vs.append(p3[:, :, DA_Q + DA_KVD:DA_Q + 2 * DA_KVD].reshape(nseq, seq, DA_KV, 2 * DA_D))
            convs.append(conv_new)
            ssms.append(ssm_new)
            ys, w_out, b_out = [oa, ob], w['w_out_even'], None
        else:
            n_odd = SW_Q + 2 * SW_KVD
            proj = _inproj(x, mod, g_norm, w['w_in_odd'], w['b_in_odd'][j].reshape(1, -1), l, j, n_odd,
                           grp, tm=tm_in, tn=tn, transposed=False, single_buffer_x=prompt)
            p3 = proj.reshape(nseq, seq, n_odd)
            if prompt:
                y = _swa_prompt(proj, w['attn_sinks'][j], nseq, seq)
                tail = p3[:, seq - WINDOW:]
                wk_new = tail[:, :, SW_Q:SW_Q + SW_KVD].reshape(nseq, WINDOW, SW_KV, SW_D)
                wv_new = tail[:, :, SW_Q + SW_KVD:].reshape(nseq, WINDOW, SW_KV, SW_D)
            else:
                win_k, win_v = caches['odd']
                k_rows = proj[:, SW_Q:SW_Q + SW_KVD]
                v_rows = proj[:, SW_Q + SW_KVD:]
                y = _swa_decode(proj[:, :SW_Q], k_rows, v_rows, w['attn_sinks'][j], win_k[j], win_v[j])
                wk_new = jnp.concatenate([win_k[j][:, 1:], k_rows.reshape(nseq, 1, SW_KV, SW_D)], axis=1)
                wv_new = jnp.concatenate([win_v[j][:, 1:], v_rows.reshape(nseq, 1, SW_KV, SW_D)], axis=1)
            wks.append(wk_new)
            wvs.append(wv_new)
            ys, w_out, b_out = [y], w['w_out_odd'], w['b_out_odd'][j].reshape(1, -1)
        x = _outproj(x, ys, mod, g_norm, w_out, b_out, l, j, grp, tm=tm_out, tk=tk)
        x = _ffn(x, mod, g_norm, *ffn_w, l, 1, 2, grp, tm=tm_ffn, tf=tf, single_buffer_x=prompt)
    return (x, jnp.stack(ks), jnp.stack(vs), jnp.stack(convs), jnp.stack(ssms), jnp.stack(wks), jnp.stack(wvs))


def kernel(x_prompt, x_sample, cache_k, cache_v, state_conv, state_ssm, cache_win_k, cache_win_v, page_table,
           c_prompt, c_sample, w_mod, b_mod, g_norm, w_ffn_gate, w_ffn_up, w_ffn_down, w_in_even, lambda_qk,
           g_subln, conv_w, conv_b, dt_bias, a_log, d_skip, g_ssm_norm, w_out_even, w_in_odd, b_in_odd,
           attn_sinks, w_out_odd, b_out_odd):
    w = dict(g_norm=g_norm, w_ffn_gate=w_ffn_gate, w_ffn_up=w_ffn_up, w_ffn_down=w_ffn_down,
             w_in_even=w_in_even, lambda_qk=lambda_qk, g_subln=g_subln, conv_w=conv_w, conv_b=conv_b,
             dt_bias=dt_bias, a_log=a_log, d_skip=d_skip, g_ssm_norm=g_ssm_norm, w_out_even=w_out_even,
             w_in_odd=w_in_odd, b_in_odd=b_in_odd, attn_sinks=attn_sinks, w_out_odd=w_out_odd,
             b_out_odd=b_out_odd)
    bp, seq, d = x_prompt.shape
    bs, dec_seq, _ = x_sample.shape
    assert dec_seq == 1
    pad = (-(bs + bp)) % 16
    c_all = jnp.concatenate([c_sample, c_prompt, jnp.zeros((pad, d), F32)], axis=0)
    mod = _modulation(c_all, w_mod, b_mod)
    grp_s = _Group(0, bs, None)
    grp_p = _Group(bs, bp, seq)

    yp, kp, vp, cvp, ssp, wkp, wvp = _trunk(x_prompt.reshape(bp * seq, d), mod, w, grp_p, nseq=bp, seq=seq,
                                            caches=None)
    caches = dict(even=(cache_k, cache_v, page_table, state_conv, state_ssm), odd=(cache_win_k, cache_win_v))
    ys, ksm, vsm, cvs, sss, wks, wvs = _trunk(x_sample.reshape(bs, d), mod, w, grp_s, nseq=bs, seq=1,
                                              caches=caches)
    return (yp.reshape(bp, seq, d), ys.reshape(bs, 1, d), kp, vp, ksm, vsm, cvp, cvs, ssp, sss,
            wkp, wvp, wks, wvs)
```

```python
import functools
import math

import jax
import jax.numpy as jnp
from jax import lax
from jax.experimental import pallas as pl
from jax.experimental.pallas import tpu as pltpu

F32 = jnp.float32
BF16 = jnp.bfloat16

DEPTH = 4
N_SUB = 3
FFN_RES = 0.5
EPS = 1e-6
NEG = -1e30
PAGE = 128
DA_KV = 4
DA_REP = 2
DA_D = 64
DA_Q = DA_KV * DA_REP * 2 * DA_D
DA_KVD = DA_KV * 2 * DA_D
SSM_INNER = 1024
SSM_P = 64
SSM_H = 16
SSM_G = 2
SSM_N = 128
CONV_W = 4
CONV_DIM = SSM_INNER + 2 * SSM_G * SSM_N
CHUNK = 128
EVEN_MAIN = DA_Q + 2 * DA_KVD + SSM_INNER + CONV_DIM
SW_D = 64
SW_KV = 4
SW_REP = 8
SW_Q = SW_KV * SW_REP * SW_D
SW_KVD = SW_KV * SW_D
WINDOW = 128

SUBLANES = 8
VMEM_LIMIT = 56 * 1024 * 1024


def _cp(sem, limit=VMEM_LIMIT):
    return pltpu.CompilerParams(dimension_semantics=sem, vmem_limit_bytes=limit)


def _silu(x):
    return x / (1.0 + jnp.exp(-x))


def _softplus(x):
    return jnp.maximum(x, 0.0) + jnp.log(1.0 + jnp.exp(-jnp.abs(x)))


def _rms(x, g):
    return x * lax.rsqrt(jnp.mean(x * x, axis=-1, keepdims=True) + EPS) * g


def _dot(a, b):
    return jnp.dot(a.astype(BF16), b.astype(BF16), preferred_element_type=F32)


def _dot_nt(a, b):
    return lax.dot_general(a.astype(BF16), b.astype(BF16), (((1,), (1,)), ((), ())),
                           preferred_element_type=F32)


def _dot_tn(a, b):
    return lax.dot_general(a.astype(BF16), b.astype(BF16), (((0,), (0,)), ((), ())),
                           preferred_element_type=F32)


def _mod_kernel(c_ref, w_ref, b_ref, o_ref):
    h = _silu(c_ref[...])
    o_ref[0] = _dot(h, w_ref[0]) + b_ref[0]


def _modulation(c_all, w_mod, b_mod, tn=1024):
    rows, d = c_all.shape
    depth, _, n = w_mod.shape
    return pl.pallas_call(
        _mod_kernel,
        grid=(depth, n // tn),
        in_specs=[pl.BlockSpec((rows, d), lambda l, j: (0, 0)),
                  pl.BlockSpec((1, d, tn), lambda l, j: (l, 0, j)),
                  pl.BlockSpec((1, 1, tn), lambda l, j: (l, 0, j))],
        out_specs=pl.BlockSpec((1, rows, tn), lambda l, j: (l, 0, j)),
        out_shape=jax.ShapeDtypeStruct((depth, rows, n), F32),
        compiler_params=_cp(("parallel", "arbitrary")),
        name="modulation",
    )(c_all, w_mod, b_mod.reshape(depth, 1, n))


class _Group:
    def __init__(self, row0, nrows, seq):
        assert row0 % SUBLANES == 0
        self.row0, self.nrows, self.seq = row0, nrows, seq
        self.block_rows = nrows if seq is None else SUBLANES
        assert seq is not None or nrows % SUBLANES == 0
        assert seq is None or nrows <= SUBLANES

    def spec(self, layer, col, d):
        rb = self.row0 // self.block_rows
        return pl.BlockSpec((1, self.block_rows, d), lambda i, j: (layer, rb, col))

    def rows(self, ref, tm):
        if self.seq is None:
            return ref[0]
        return ref[0, pl.ds((pl.program_id(0) * tm) // self.seq, 1), :]


def _gn_spec(layer, d):
    return pl.BlockSpec((1, 2 * N_SUB, d), lambda i, j: (layer, 0, 0))


def _ffn_kernel(*refs, nf, res_w, kpre, grp, tm, emit):
    x_ref, sh_ref, sc_ref, gt_ref, gn_ref, wg_ref, wu_ref, wd_ref, o_ref = refs[:9]
    h_ref = refs[-1]
    f = pl.program_id(1)

    @pl.when(f == 0)
    def _():
        xn = _rms(x_ref[...], gn_ref[0, kpre:kpre + 1, :])
        h_ref[...] = (xn * (1 + grp.rows(sc_ref, tm)) + grp.rows(sh_ref, tm)).astype(BF16)
        o_ref[...] = jnp.zeros(o_ref.shape, F32)

    if emit:
        wg, wu, wd = (r[0, 0].astype(BF16) for r in (wg_ref, wu_ref, wd_ref))
        for out_ref, tile in zip(refs[9:12], (wg, wu, wd)):
            out_ref[0] = tile
    else:
        wg, wu, wd = wg_ref[0], wu_ref[0], wd_ref[0]
    h = h_ref[...]
    g = jnp.dot(h, wg, preferred_element_type=F32)
    u = jnp.dot(h, wu, preferred_element_type=F32)
    o_ref[...] += jnp.dot((_silu(g) * u).astype(BF16), wd, preferred_element_type=F32)

    @pl.when(f == nf - 1)
    def _():
        yn = _rms(o_ref[...], gn_ref[0, kpre + 1:kpre + 2, :])
        o_ref[...] = x_ref[...] + (res_w * grp.rows(gt_ref, tm)) * yn


def _ffn(x, mod, g_norm, weights, l, s, sub, grp, *, tm, tf, tiles=None):
    m, d = x.shape
    emit = tiles is None
    if emit:
        nf = weights[0].shape[-1] // tf
        w_specs = [pl.BlockSpec((1, 1, d, tf), lambda i, j: (l, s, 0, j)),
                   pl.BlockSpec((1, 1, d, tf), lambda i, j: (l, s, 0, j)),
                   pl.BlockSpec((1, 1, tf, d), lambda i, j: (l, s, j, 0))]
        w_args = weights
        x_kw = {}
    else:
        nf = tiles[0].shape[0]
        w_specs = [pl.BlockSpec((1, d, tf), lambda i, j: (j, 0, 0)),
                   pl.BlockSpec((1, d, tf), lambda i, j: (j, 0, 0)),
                   pl.BlockSpec((1, tf, d), lambda i, j: (j, 0, 0))]
        w_args = tiles
        x_kw = dict(pipeline_mode=pl.Buffered(1))
    out_specs = [pl.BlockSpec((tm, d), lambda i, j: (i, 0))]
    out_shape = [jax.ShapeDtypeStruct((m, d), F32)]
    if emit:
        assert m == tm
        out_specs += [pl.BlockSpec((1, d, tf), lambda i, j: (j, 0, 0)),
                      pl.BlockSpec((1, d, tf), lambda i, j: (j, 0, 0)),
                      pl.BlockSpec((1, tf, d), lambda i, j: (j, 0, 0))]
        out_shape += [jax.ShapeDtypeStruct((nf, d, tf), BF16), jax.ShapeDtypeStruct((nf, d, tf), BF16),
                      jax.ShapeDtypeStruct((nf, tf, d), BF16)]
    kern = functools.partial(_ffn_kernel, nf=nf, res_w=FFN_RES, kpre=2 * sub, grp=grp, tm=tm, emit=emit)
    out = pl.pallas_call(
        kern,
        grid=(m // tm, nf),
        in_specs=[pl.BlockSpec((tm, d), lambda i, j: (i, 0), **x_kw),
                  grp.spec(l, 3 * sub + 0, d), grp.spec(l, 3 * sub + 1, d), grp.spec(l, 3 * sub + 2, d),
                  _gn_spec(l, d)] + w_specs,
        out_specs=out_specs,
        out_shape=out_shape,
        scratch_shapes=[pltpu.VMEM((tm, d), BF16)],
        compiler_params=_cp(("parallel", "arbitrary")),
        name="ffn_emit" if emit else "ffn",
    )(x, mod, mod, mod, g_norm, *w_args)
    return (out[0], tuple(out[1:])) if emit else out[0]


def _inproj_kernel(*refs, transposed, has_bias, has_dt, grp, tm):
    refs = list(refs)
    x_ref, sh_ref, sc_ref, gn_ref, w_ref = refs[:5]
    rest = refs[5:]
    b_ref = rest.pop(0) if has_bias else None
    wdt_ref = rest.pop(0) if has_dt else None
    o_ref = rest.pop(0)
    dt_ref = rest.pop(0) if has_dt else None
    wbo_ref, h_ref, wb_ref = rest
    i = pl.program_id(0)
    j = pl.program_id(1)

    @pl.when(j == 0)
    def _():
        xn = _rms(x_ref[...], gn_ref[0, 2:3, :])
        h = (xn * (1 + grp.rows(sc_ref, tm)) + grp.rows(sh_ref, tm)).astype(BF16)
        h_ref[...] = h
        if has_dt:
            dt_ref[...] = _dot_nt(h, wdt_ref[0])

    @pl.when(i == 0)
    def _():
        wf = w_ref[0]
        wb_ref[j] = wf.astype(BF16)
        wbo_ref[...] = (wf.T if transposed else wf).astype(BF16)

    w = wb_ref[j]
    if transposed:
        o = lax.dot_general(h_ref[...], w, (((1,), (1,)), ((), ())), preferred_element_type=F32)
    else:
        o = jnp.dot(h_ref[...], w, preferred_element_type=F32)
    o_ref[...] = o + b_ref[...] if has_bias else o


def _inproj(x, mod, g_norm, w, b, l, layer, n, grp, *, tm, tn, transposed, n_dt=0):
    m, d = x.shape
    assert m == tm
    nj = n // tn
    has_bias = b is not None
    has_dt = n_dt > 0
    wcol = lambda i, j: jnp.where(i == 0, j, nj - 1)
    if transposed:
        w_spec = pl.BlockSpec((1, tn, d), lambda i, j: (layer, wcol(i, j), 0))
        wb_shape = (nj, tn, d)
    else:
        w_spec = pl.BlockSpec((1, d, tn), lambda i, j: (layer, 0, wcol(i, j)))
        wb_shape = (nj, d, tn)
    in_specs = [pl.BlockSpec((tm, d), lambda i, j: (i, 0)),
                grp.spec(l, 3, d), grp.spec(l, 4, d), _gn_spec(l, d), w_spec]
    args = [x, mod, mod, g_norm, w]
    if has_bias:
        in_specs.append(pl.BlockSpec((1, tn), lambda i, j: (0, j)))
        args.append(b)
    out_specs = [pl.BlockSpec((tm, tn), lambda i, j: (i, j))]
    out_shape = [jax.ShapeDtypeStruct((m, n), F32)]
    if has_dt:
        assert transposed and n % n_dt == 0
        in_specs.append(pl.BlockSpec((1, n_dt, d), lambda i, j: (layer, n // n_dt, 0)))
        args.append(w)
        out_specs.append(pl.BlockSpec((tm, n_dt), lambda i, j: (i, 0)))
        out_shape.append(jax.ShapeDtypeStruct((m, n_dt), F32))
    out_specs.append(pl.BlockSpec((d, tn), lambda i, j: (0, j)))
    out_shape.append(jax.ShapeDtypeStruct((d, n), BF16))
    return pl.pallas_call(
        functools.partial(_inproj_kernel, transposed=transposed, has_bias=has_bias, has_dt=has_dt,
                          grp=grp, tm=tm),
        grid=(m // tm, nj),
        in_specs=in_specs,
        out_specs=out_specs,
        out_shape=out_shape,
        scratch_shapes=[pltpu.VMEM((tm, d), BF16), pltpu.VMEM(wb_shape, BF16)],
        compiler_params=_cp(("arbitrary", "arbitrary")),
        name="inproj_emit",
    )(*args)


def _inproj_res_kernel(*refs, n, chunk, has_bias, has_dt, grp, tm):
    refs = list(refs)
    x_ref, sh_ref, sc_ref, gn_ref, wb_ref = refs[:5]
    rest = refs[5:]
    b_ref = rest.pop(0) if has_bias else None
    wdt_ref = rest.pop(0) if has_dt else None
    o_ref = rest.pop(0)
    xn = _rms(x_ref[...], gn_ref[0, 2:3, :])
    h = (xn * (1 + grp.rows(sc_ref, tm)) + grp.rows(sh_ref, tm)).astype(BF16)
    if has_dt:
        rest[0][...] = _dot_nt(h, wdt_ref[0])
    for c in range(n // chunk):
        cols = slice(c * chunk, (c + 1) * chunk)
        o = jnp.dot(h, wb_ref[:, cols], preferred_element_type=F32)
        o_ref[:, cols] = o + b_ref[:, cols] if has_bias else o


def _inproj_res(x, mod, g_norm, wb, b, w_dt, l, layer, grp, *, tm, n_dt=0, chunk=512):
    m, d = x.shape
    n = wb.shape[1]
    has_bias = b is not None
    has_dt = n_dt > 0
    once = dict(pipeline_mode=pl.Buffered(1))
    in_specs = [pl.BlockSpec((tm, d), lambda i, j: (i, 0)),
                grp.spec(l, 3, d), grp.spec(l, 4, d), _gn_spec(l, d),
                pl.BlockSpec((d, n), lambda i, j: (0, 0), **once)]
    args = [x, mod, mod, g_norm, wb]
    if has_bias:
        in_specs.append(pl.BlockSpec((1, n), lambda i, j: (0, 0)))
        args.append(b)
    out_specs = [pl.BlockSpec((tm, n), lambda i, j: (i, 0))]
    out_shape = [jax.ShapeDtypeStruct((m, n), F32)]
    if has_dt:
        in_specs.append(pl.BlockSpec((1, n_dt, d), lambda i, j: (layer, n // n_dt, 0)))
        args.append(w_dt)
        out_specs.append(pl.BlockSpec((tm, n_dt), lambda i, j: (i, 0)))
        out_shape.append(jax.ShapeDtypeStruct((m, n_dt), F32))
    out = pl.pallas_call(
        functools.partial(_inproj_res_kernel, n=n, chunk=chunk, has_bias=has_bias, has_dt=has_dt,
                          grp=grp, tm=tm),
        grid=(m // tm, 1),
        in_specs=in_specs,
        out_specs=out_specs,
        out_shape=out_shape,
        compiler_params=_cp(("parallel", "arbitrary")),
        name="inproj",
    )(*args)
    return out if has_dt else out[0]


def _outproj_kernel(*refs, bounds, has_bias, grp, tm):
    refs = list(refs)
    x_ref = refs.pop(0)
    y_refs = [refs.pop(0) for _ in bounds]
    gt_ref, gn_ref, w_ref = refs[:3]
    rest = refs[3:]
    b_ref = rest.pop(0) if has_bias else None
    o_ref, wbo_ref, wb_ref = rest
    nk = bounds[-1][1]
    i = pl.program_id(0)
    k = pl.program_id(1)

    @pl.when(i == 0)
    def _():
        wb = w_ref[0].astype(BF16)
        wb_ref[k] = wb
        wbo_ref[...] = wb

    @pl.when(k == 0)
    def _():
        o_ref[...] = jnp.zeros(o_ref.shape, F32)

    for y_ref, (lo, hi) in zip(y_refs, bounds):
        @pl.when(jnp.logical_and(k >= lo, k < hi))
        def _(y_ref=y_ref):
            o_ref[...] += jnp.dot(y_ref[...].astype(BF16), wb_ref[k], preferred_element_type=F32)

    @pl.when(k == nk - 1)
    def _():
        y = o_ref[...] + b_ref[...] if has_bias else o_ref[...]
        o_ref[...] = x_ref[...] + grp.rows(gt_ref, tm) * _rms(y, gn_ref[0, 3:4, :])


def _outproj(x, ys, mod, g_norm, w, b, l, layer, grp, *, tm, tk):
    m, d = x.shape
    has_bias = b is not None
    bounds, lo = [], 0
    for y in ys:
        bounds.append((lo, lo + y.shape[1] // tk))
        lo = bounds[-1][1]
    nk = lo
    assert nk * tk == w.shape[1]

    def y_spec(lo, hi):
        return pl.BlockSpec((tm, tk), lambda i, k: (i, jnp.clip(k - lo, 0, hi - lo - 1)))

    in_specs = ([pl.BlockSpec((tm, d), lambda i, k: (i, 0))]
                + [y_spec(lo, hi) for lo, hi in bounds]
                + [grp.spec(l, 5, d), _gn_spec(l, d),
                   pl.BlockSpec((1, tk, d), lambda i, k: (layer, jnp.where(i == 0, k, nk - 1), 0))])
    args = [x, *ys, mod, g_norm, w]
    if has_bias:
        in_specs.append(pl.BlockSpec((1, d), lambda i, k: (0, 0)))
        args.append(b)
    assert m == tm
    return pl.pallas_call(
        functools.partial(_outproj_kernel, bounds=tuple(bounds), has_bias=has_bias, grp=grp, tm=tm),
        grid=(m // tm, nk),
        in_specs=in_specs,
        out_specs=[pl.BlockSpec((tm, d), lambda i, k: (i, 0)), pl.BlockSpec((tk, d), lambda i, k: (k, 0))],
        out_shape=[jax.ShapeDtypeStruct((m, d), F32), jax.ShapeDtypeStruct((nk * tk, d), BF16)],
        scratch_shapes=[pltpu.VMEM((nk, tk, d), BF16)],
        compiler_params=_cp(("arbitrary", "arbitrary")),
        name="outproj_emit",
    )(*args)


def _outproj_res_kernel(*refs, n_y, has_bias, grp, tm):
    x_ref = refs[0]
    y_refs = refs[1:1 + n_y]
    gt_ref, gn_ref, wb_ref = refs[1 + n_y:4 + n_y]
    b_ref = refs[4 + n_y] if has_bias else None
    o_ref = refs[-1]
    ycat = jnp.concatenate([r[...].astype(BF16) for r in y_refs], axis=-1)
    y = jnp.dot(ycat, wb_ref[...], preferred_element_type=F32)
    if has_bias:
        y = y + b_ref[...]
    o_ref[...] = x_ref[...] + grp.rows(gt_ref, tm) * _rms(y, gn_ref[0, 3:4, :])


def _outproj_res(x, ys, mod, g_norm, wb, b, l, grp, *, tm):
    m, d = x.shape
    has_bias = b is not None
    in_specs = ([pl.BlockSpec((tm, d), lambda i, k: (i, 0))]
                + [pl.BlockSpec((tm, y.shape[1]), lambda i, k: (i, 0)) for y in ys]
                + [grp.spec(l, 5, d), _gn_spec(l, d),
                   pl.BlockSpec(wb.shape, lambda i, k: (0, 0), pipeline_mode=pl.Buffered(1))])
    args = [x, *ys, mod, g_norm, wb]
    if has_bias:
        in_specs.append(pl.BlockSpec((1, d), lambda i, k: (0, 0)))
        args.append(b)
    return pl.pallas_call(
        functools.partial(_outproj_res_kernel, n_y=len(ys), has_bias=has_bias, grp=grp, tm=tm),
        grid=(m // tm, 1),
        in_specs=in_specs,
        out_specs=pl.BlockSpec((tm, d), lambda i, k: (i, 0)),
        out_shape=jax.ShapeDtypeStruct((m, d), F32),
        compiler_params=_cp(("parallel", "arbitrary")),
        name="outproj",
    )(*args)


def _lambda(lp, lam_init):
    a = jnp.sum(lp[0:1] * lp[1:2], axis=-1, keepdims=True)
    b = jnp.sum(lp[2:3] * lp[3:4], axis=-1, keepdims=True)
    return jnp.exp(a) - jnp.exp(b) + lam_init


def _da_prompt_kernel(q_ref, k_ref, v_ref, lp_ref, gs_ref, o_ref, *, tq, lam_init):
    i = pl.program_id(2)
    lam = _lambda(lp_ref[...], lam_init)
    q = q_ref[...] * (DA_D ** -0.5)
    low = lax.broadcasted_iota(jnp.int32, (tq, 2 * DA_D), 1) < DA_D
    heads = (q[:, :2 * DA_D], q[:, 2 * DA_D:])
    qp = (jnp.concatenate([jnp.where(low, h, 0.0) for h in heads], axis=0).astype(BF16),
          jnp.concatenate([jnp.where(low, 0.0, h) for h in heads], axis=0).astype(BF16))
    key = lax.broadcasted_iota(jnp.int32, (tq, 2 * tq), 0)
    col = lax.broadcasted_iota(jnp.int32, (tq, 2 * tq), 1)
    causal_bias = jnp.where(key <= jnp.where(col >= tq, col - tq, col), 0.0, NEG)

    def step(j, carry, diagonal):
        off = pl.multiple_of(j * tq, tq)
        kb = k_ref[pl.ds(off, tq), :].astype(BF16)
        vb = v_ref[pl.ds(off, tq), :].astype(BF16)
        out = []
        for mi in range(2):
            m_old, l_old, acc = carry[3 * mi:3 * mi + 3]
            s = lax.dot_general(kb, qp[mi], (((1,), (1,)), ((), ())), preferred_element_type=F32)
            if diagonal:
                s = s + causal_bias
            m_new = jnp.maximum(m_old, jnp.max(s, axis=0, keepdims=True))
            alpha = jnp.exp(m_old - m_new)
            p = jnp.exp(s - m_new)
            l_new = alpha * l_old + jnp.sum(p, axis=0, keepdims=True)
            acc = alpha * acc + lax.dot_general(vb, p.astype(BF16), (((0,), (0,)), ((), ())),
                                                preferred_element_type=F32)
            out += [m_new, l_new, acc]
        return tuple(out)

    m0 = jnp.full((1, 2 * tq), NEG, F32)
    l0 = jnp.zeros((1, 2 * tq), F32)
    a0 = jnp.zeros((2 * DA_D, 2 * tq), F32)
    res = lax.fori_loop(0, i, lambda j, c: step(j, c, False), (m0, l0, a0, m0, l0, a0))
    res = step(i, res, True)
    a = res[2] / res[1] - lam * (res[5] / res[4])
    a = a * lax.rsqrt(jnp.mean(a * a, axis=0, keepdims=True) + EPS) * gs_ref[...] * (1 - lam_init)
    a = a.T
    o_ref[:, 0:128] = a[:tq]
    o_ref[:, 128:256] = a[tq:]


def _da_prompt(proj, lp, gs, nseq, seq, lam_init, tq=512):
    m = proj.shape[0]
    tq = min(tq, seq)
    nq = seq // tq
    kcol = DA_Q // 128
    vcol = (DA_Q + DA_KVD) // 128
    return pl.pallas_call(
        functools.partial(_da_prompt_kernel, tq=tq, lam_init=lam_init),
        grid=(nseq, DA_KV, nq),
        in_specs=[pl.BlockSpec((tq, 256), lambda b, g, i: (b * nq + i, g)),
                  pl.BlockSpec((seq, 128), lambda b, g, i: (b, kcol + g)),
                  pl.BlockSpec((seq, 128), lambda b, g, i: (b, vcol + g)),
                  pl.BlockSpec((4, DA_D), lambda b, g, i: (0, 0)),
                  pl.BlockSpec((2 * DA_D, 1), lambda b, g, i: (0, 0))],
        out_specs=pl.BlockSpec((tq, 256), lambda b, g, i: (b * nq + i, g)),
        out_shape=jax.ShapeDtypeStruct((m, DA_Q), F32),
        compiler_params=_cp(("parallel", "parallel", "arbitrary")),
        name="diff_attn_prompt",
    )(proj, proj, proj, lp, gs.reshape(-1, 1))


def _da_decode_kernel(pt_ref, qm_ref, kn_ref, vn_ref, lp_ref, gs_ref, *refs, pc, nc, lam_init):
    del pt_ref
    k_refs = refs[:pc]
    v_refs = refs[pc:2 * pc]
    o_ref, m_ref, l_ref, acc_ref, kb_ref, vb_ref = refs[2 * pc:]
    c = pl.program_id(1)

    @pl.when(c == 0)
    def _():
        m_ref[...] = jnp.full(m_ref.shape, NEG, F32)
        l_ref[...] = jnp.zeros(l_ref.shape, F32)
        acc_ref[...] = jnp.zeros(acc_ref.shape, F32)

    rows = PAGE * DA_KV
    for t in range(pc):
        kb_ref[t * rows:(t + 1) * rows, :] = k_refs[t][0, 0].astype(BF16)
        vb_ref[t * rows:(t + 1) * rows, :] = v_refs[t][0, 0].astype(BF16)
    qm = qm_ref[0] * (DA_D ** -0.5)
    s = _dot_nt(qm, kb_ref[...])
    row_g = (lax.broadcasted_iota(jnp.int32, s.shape, 0) // DA_REP) & (DA_KV - 1)
    col_g = lax.broadcasted_iota(jnp.int32, s.shape, 1) & (DA_KV - 1)
    s = jnp.where(row_g == col_g, s, NEG)
    m_old = m_ref[...]
    m_new = jnp.maximum(m_old, jnp.max(s, axis=-1, keepdims=True))
    alpha = jnp.exp(m_old - m_new)
    p = jnp.exp(s - m_new)
    l_ref[...] = alpha * l_ref[...] + jnp.sum(p, axis=-1, keepdims=True)
    acc_ref[...] = alpha * acc_ref[...] + jnp.dot(p.astype(BF16), vb_ref[...], preferred_element_type=F32)
    m_ref[...] = m_new

    @pl.when(c == nc - 1)
    def _():
        lam = _lambda(lp_ref[...], lam_init)
        s_new = jnp.sum(qm * kn_ref[0], axis=-1, keepdims=True)
        m_old = m_ref[...]
        m_fin = jnp.maximum(m_old, s_new)
        alpha = jnp.exp(m_old - m_fin)
        p_new = jnp.exp(s_new - m_fin)
        l_fin = alpha * l_ref[...] + p_new
        acc = alpha * acc_ref[...] + p_new * vn_ref[0]
        o = acc / l_fin
        a = o[0:8] - lam * o[8:16]
        o_ref[0] = _rms(a, gs_ref[...]) * (1 - lam_init)


def _da_decode(q, k_new, v_new, lp, gs, cache_k, cache_v, page_table, j, lam_init, pc=32):
    nb = q.shape[0]
    n_pages = page_table.shape[1]
    pc = min(pc, n_pages)
    nc = n_pages // pc
    rows = PAGE * DA_KV
    ck = cache_k.reshape(cache_k.shape[0], cache_k.shape[1], rows, 2 * DA_D)
    cv = cache_v.reshape(cache_v.shape[0], cache_v.shape[1], rows, 2 * DA_D)
    qt = q.reshape(nb, DA_KV, DA_REP, 2, DA_D).transpose(0, 3, 1, 2, 4)
    own_m = jnp.eye(2, dtype=bool)[None, :, None, None, :, None]
    qm = jnp.where(own_m, qt[:, :, :, :, None, :], 0.0).reshape(nb, 16, 2 * DA_D)
    rep = lambda a: jnp.broadcast_to(a.reshape(nb, 1, DA_KV, 1, 2 * DA_D),
                                     (nb, 2, DA_KV, DA_REP, 2 * DA_D)).reshape(nb, 16, 2 * DA_D)

    def page_spec(t):
        return pl.BlockSpec((1, 1, rows, 2 * DA_D), lambda b, c, pt: (j, pt[b, c * pc + t], 0, 0))

    row_spec = pl.BlockSpec((1, 16, 2 * DA_D), lambda b, c, pt: (b, 0, 0))
    grid_spec = pltpu.PrefetchScalarGridSpec(
        num_scalar_prefetch=1,
        grid=(nb, nc),
        in_specs=[row_spec, row_spec, row_spec,
                  pl.BlockSpec((4, DA_D), lambda b, c, pt: (0, 0)),
                  pl.BlockSpec((1, 2 * DA_D), lambda b, c, pt: (0, 0))]
                 + [page_spec(t) for t in range(pc)] + [page_spec(t) for t in range(pc)],
        out_specs=pl.BlockSpec((1, 8, 128), lambda b, c, pt: (b, 0, 0)),
        scratch_shapes=[pltpu.VMEM((16, 1), F32), pltpu.VMEM((16, 1), F32),
                        pltpu.VMEM((16, 2 * DA_D), F32),
                        pltpu.VMEM((pc * rows, 2 * DA_D), BF16), pltpu.VMEM((pc * rows, 2 * DA_D), BF16)])
    out = pl.pallas_call(
        functools.partial(_da_decode_kernel, pc=pc, nc=nc, lam_init=lam_init),
        grid_spec=grid_spec,
        out_shape=jax.ShapeDtypeStruct((nb, 8, 128), F32),
        compiler_params=_cp(("parallel", "arbitrary")),
        name="diff_attn_decode",
    )(page_table, qm, rep(k_new), rep(v_new), lp, gs.reshape(1, -1), *([ck] * pc), *([cv] * pc))
    return out.reshape(nb, DA_Q)


def _split3(x):
    x1 = x.astype(BF16)
    r1 = x - x1.astype(F32)
    x2 = r1.astype(BF16)
    x3 = (r1 - x2.astype(F32)).astype(BF16)
    return x1, x2, x3


def _group_norm_gate(y, z, gn):
    yz = y * _silu(z)
    w = SSM_INNER // SSM_G
    parts = []
    for g in range(SSM_G):
        seg = yz[:, g * w:(g + 1) * w]
        parts.append(seg * lax.rsqrt(jnp.mean(seg * seg, axis=-1, keepdims=True) + EPS))
    return jnp.concatenate(parts, axis=-1) * gn


def _ssd_prompt_kernel(xbc_ref, z_ref, dt_ref, dtt_ref, cw_ref, cb_ref, dtb_ref, dtbt_ref,
                       alog_ref, alogt_ref, dskip_ref, gn_ref, y_ref, st_ref, ext_ref, h_ref, *, nchunk):
    c = pl.program_id(1)
    q = CHUNK

    @pl.when(c == 0)
    def _():
        ext_ref[0:8, :] = jnp.zeros((8, CONV_DIM), F32)
        h_ref[...] = jnp.zeros(h_ref.shape, F32)

    @pl.when(c > 0)
    def _():
        ext_ref[0:8, :] = ext_ref[q:q + 8, :]

    ext_ref[8:q + 8, :] = xbc_ref[...]
    cw = cw_ref[...]
    acc = ext_ref[5:q + 5, :] * cw[0:1]
    for kk in range(1, CONV_W):
        acc = acc + ext_ref[5 + kk:q + 5 + kk, :] * cw[kk:kk + 1]
    xa = _silu(acc + cb_ref[...])
    xs = xa[:, :SSM_INNER]
    bm = xa[:, SSM_INNER:SSM_INNER + SSM_G * SSM_N].astype(BF16)
    cm = xa[:, SSM_INNER + SSM_G * SSM_N:].astype(BF16)

    dt = _softplus(dt_ref[...] + dtb_ref[...])
    dtt = _softplus(dtt_ref[...] + dtbt_ref[...])
    a = dt * (-jnp.exp(alog_ref[...]))
    at = dtt * (-jnp.exp(alogt_ref[...]))
    ri = lax.broadcasted_iota(jnp.int32, (q, q), 0)
    ci = lax.broadcasted_iota(jnp.int32, (q, q), 1)
    lower = ri >= ci
    tril = jnp.where(lower, 1.0, 0.0).astype(BF16)
    triu = jnp.where(ri <= ci, 1.0, 0.0).astype(BF16)
    cs = sum(jnp.dot(tril, part, preferred_element_type=F32) for part in _split3(a))
    cst = sum(jnp.dot(part, triu, preferred_element_type=F32) for part in _split3(at))
    tot = cs[q - 1:q, :]
    e_cs = jnp.exp(cs)
    e_dec = jnp.exp(tot - cs)
    e_tot = jnp.exp(tot)
    dskip = dskip_ref[...]

    ys = []
    for g in range(SSM_G):
        bg = bm[:, g * SSM_N:(g + 1) * SSM_N]
        cg = cm[:, g * SSM_N:(g + 1) * SSM_N]
        cb = lax.dot_general(cg, bg, (((1,), (1,)), ((), ())), preferred_element_type=F32)
        for r in range(SSM_H // SSM_G):
            h = g * (SSM_H // SSM_G) + r
            lmat = jnp.exp(jnp.where(lower, cs[:, h:h + 1] - cst[h:h + 1, :], NEG))
            x_h = xs[:, h * SSM_P:(h + 1) * SSM_P]
            xdt = x_h * dt[:, h:h + 1]
            y_diag = _dot(cb * lmat, xdt)
            hp = h_ref[h]
            y_off = lax.dot_general(cg, hp.astype(BF16), (((1,), (1,)), ((), ())),
                                    preferred_element_type=F32) * e_cs[:, h:h + 1]
            upd = lax.dot_general((xdt * e_dec[:, h:h + 1]).astype(BF16), bg, (((0,), (0,)), ((), ())),
                                  preferred_element_type=F32)
            h_ref[h] = e_tot[:, h:h + 1] * hp + upd
            ys.append(y_diag + y_off + dskip[:, h:h + 1] * x_h)
    y = jnp.concatenate(ys, axis=-1)
    y_ref[...] = _group_norm_gate(y, z_ref[...], gn_ref[...])

    @pl.when(c == nchunk - 1)
    def _():
        st_ref[0] = h_ref[...]


def _ssd_prompt(proj, dt_raw, cw, cb, dtb, alog, dskip, gn, nseq, seq):
    m = proj.shape[0]
    nchunk = seq // CHUNK
    zcol = (DA_Q + 2 * DA_KVD) // SSM_INNER
    xcol = (DA_Q + 2 * DA_KVD + SSM_INNER) // CONV_DIM
    const = lambda b, c: (0, 0)
    y, st = pl.pallas_call(
        functools.partial(_ssd_prompt_kernel, nchunk=nchunk),
        grid=(nseq, nchunk),
        in_specs=[pl.BlockSpec((CHUNK, CONV_DIM), lambda b, c: (b * nchunk + c, xcol)),
                  pl.BlockSpec((CHUNK, SSM_INNER), lambda b, c: (b * nchunk + c, zcol)),
                  pl.BlockSpec((CHUNK, SSM_H), lambda b, c: (b * nchunk + c, 0)),
                  pl.BlockSpec((SSM_H, CHUNK), lambda b, c: (0, b * nchunk + c)),
                  pl.BlockSpec((CONV_W, CONV_DIM), const),
                  pl.BlockSpec((1, CONV_DIM), const),
                  pl.BlockSpec((1, SSM_H), const),
                  pl.BlockSpec((SSM_H, 1), const),
                  pl.BlockSpec((1, SSM_H), const),
                  pl.BlockSpec((SSM_H, 1), const),
                  pl.BlockSpec((1, SSM_H), const),
                  pl.BlockSpec((1, SSM_INNER), const)],
        out_specs=[pl.BlockSpec((CHUNK, SSM_INNER), lambda b, c: (b * nchunk + c, 0)),
                   pl.BlockSpec((1, SSM_H, SSM_P, SSM_N), lambda b, c: (b, 0, 0, 0))],
        out_shape=[jax.ShapeDtypeStruct((m, SSM_INNER), F32),
                   jax.ShapeDtypeStruct((nseq, SSM_H, SSM_P, SSM_N), F32)],
        scratch_shapes=[pltpu.VMEM((CHUNK + 8, CONV_DIM), F32),
                        pltpu.VMEM((SSM_H, SSM_P, SSM_N), F32)],
        compiler_params=_cp(("parallel", "arbitrary")),
        name="ssd_prompt",
    )(proj, proj, dt_raw, dt_raw.T, cw, cb.reshape(1, -1), dtb.reshape(1, -1), dtb.reshape(-1, 1),
      alog.reshape(1, -1), alog.reshape(-1, 1), dskip.reshape(1, -1), gn.reshape(1, -1))
    return y, st


def _ssd_dec_pre_kernel(xbc_ref, cst_ref, dt_ref, cw_ref, cb_ref, dtb_ref, xa_ref, dts_ref):
    cw = cw_ref[...]
    acc = cst_ref[0] * cw[0:1]
    for kk in range(1, CONV_W - 1):
        acc = acc + cst_ref[kk] * cw[kk:kk + 1]
    acc = acc + xbc_ref[...] * cw[CONV_W - 1:CONV_W]
    xa_ref[...] = _silu(acc + cb_ref[...])
    dts_ref[...] = _softplus(dt_ref[...] + dtb_ref[...])


def _ssd_dec_state_kernel(xt_ref, dts_ref, alog_ref, b_ref, c_ref, dskip_ref, h_ref, yt_ref, hn_ref):
    xt = xt_ref[0]
    dts = dts_ref[0]
    a = dts * (-jnp.exp(alog_ref[...]))
    e_a = jnp.exp(a)
    dskip = dskip_ref[...]
    cols = []
    for h in range(SSM_H):
        g = h // (SSM_H // SSM_G)
        xcol = xt[:, h:h + 1]
        brow = b_ref[0][:, g * SSM_N:(g + 1) * SSM_N]
        crow = c_ref[0][:, g * SSM_N:(g + 1) * SSM_N]
        hn = e_a[:, h:h + 1] * h_ref[0, h] + (xcol * dts[:, h:h + 1]) * brow
        hn_ref[0, h] = hn
        cols.append(jnp.sum(hn * crow, axis=-1, keepdims=True) + dskip[:, h:h + 1] * xcol)
    yt_ref[0] = jnp.concatenate(cols, axis=-1)


def _ssd_dec_post_kernel(y_ref, z_ref, gn_ref, o_ref):
    o_ref[...] = _group_norm_gate(y_ref[...], z_ref[...], gn_ref[...])


def _ssd_decode(proj, dt_raw, conv_state, ssm_state, cw, cb, dtb, alog, dskip, gn):
    nb = proj.shape[0]
    zcol = (DA_Q + 2 * DA_KVD) // SSM_INNER
    xcol = (DA_Q + 2 * DA_KVD + SSM_INNER) // CONV_DIM
    full = lambda shape: pl.BlockSpec(shape, lambda i: (0,) * len(shape))
    xa, dts = pl.pallas_call(
        _ssd_dec_pre_kernel,
        grid=(1,),
        in_specs=[pl.BlockSpec((nb, CONV_DIM), lambda i: (0, xcol)),
                  full((CONV_W - 1, nb, CONV_DIM)),
                  full((nb, SSM_H)),
                  full((CONV_W, CONV_DIM)),
                  full((1, CONV_DIM)),
                  full((1, SSM_H))],
        out_specs=[full((nb, CONV_DIM)), full((nb, SSM_H))],
        out_shape=[jax.ShapeDtypeStruct((nb, CONV_DIM), F32), jax.ShapeDtypeStruct((nb, SSM_H), F32)],
        compiler_params=_cp(("arbitrary",)),
        name="ssd_decode_pre",
    )(proj, conv_state.transpose(1, 0, 2), dt_raw, cw, cb.reshape(1, -1), dtb.reshape(1, -1))
    xt = xa[:, :SSM_INNER].reshape(nb, SSM_H, SSM_P).transpose(0, 2, 1)
    bmat = xa[:, SSM_INNER:SSM_INNER + SSM_G * SSM_N].reshape(nb, 1, SSM_G * SSM_N)
    cmat = xa[:, SSM_INNER + SSM_G * SSM_N:].reshape(nb, 1, SSM_G * SSM_N)
    yt, h_new = pl.pallas_call(
        _ssd_dec_state_kernel,
        grid=(nb,),
        in_specs=[pl.BlockSpec((1, SSM_P, SSM_H), lambda b: (b, 0, 0)),
                  pl.BlockSpec((1, 1, SSM_H), lambda b: (b, 0, 0)),
                  pl.BlockSpec((1, SSM_H), lambda b: (0, 0)),
                  pl.BlockSpec((1, 1, SSM_G * SSM_N), lambda b: (b, 0, 0)),
                  pl.BlockSpec((1, 1, SSM_G * SSM_N), lambda b: (b, 0, 0)),
                  pl.BlockSpec((1, SSM_H), lambda b: (0, 0)),
                  pl.BlockSpec((1, SSM_H, SSM_P, SSM_N), lambda b: (b, 0, 0, 0))],
        out_specs=[pl.BlockSpec((1, SSM_P, SSM_H), lambda b: (b, 0, 0)),
                   pl.BlockSpec((1, SSM_H, SSM_P, SSM_N), lambda b: (b, 0, 0, 0))],
        out_shape=[jax.ShapeDtypeStruct((nb, SSM_P, SSM_H), F32),
                   jax.ShapeDtypeStruct(ssm_state.shape, F32)],
        compiler_params=_cp(("parallel",)),
        name="ssd_decode_state",
    )(xt, dts.reshape(nb, 1, SSM_H), alog.reshape(1, -1), bmat, cmat, dskip.reshape(1, -1), ssm_state)
    y = yt.transpose(0, 2, 1).reshape(nb, SSM_INNER)
    ob = pl.pallas_call(
        _ssd_dec_post_kernel,
        grid=(1,),
        in_specs=[full((nb, SSM_INNER)),
                  pl.BlockSpec((nb, SSM_INNER), lambda i: (0, zcol)),
                  full((1, SSM_INNER))],
        out_specs=full((nb, SSM_INNER)),
        out_shape=jax.ShapeDtypeStruct((nb, SSM_INNER), F32),
        compiler_params=_cp(("arbitrary",)),
        name="ssd_decode_post",
    )(y, proj, gn.reshape(1, -1))
    return ob, h_new


def _swa_prompt_kernel(sink_ref, q_ref, kp_ref, kc_ref, vp_ref, vc_ref, o_ref):
    i = pl.program_id(1)
    w = WINDOW
    pair_w = 2 * SW_D
    sinks = sink_ref[...]
    q = q_ref[...] * (SW_D ** -0.5)
    kk = jnp.concatenate([kp_ref[...], kc_ref[...]], axis=0)
    vv = jnp.concatenate([vp_ref[...], vc_ref[...]], axis=0).astype(BF16)
    low_k = lax.broadcasted_iota(jnp.int32, (2 * w, pair_w), 1) < SW_D
    low_q = lax.broadcasted_iota(jnp.int32, (w, pair_w), 1) < SW_D
    key = lax.broadcasted_iota(jnp.int32, (2 * w, w), 0)
    t = key - lax.broadcasted_iota(jnp.int32, (2 * w, w), 1)
    keymin = jnp.where(i > 0, 0, w)
    bias1 = jnp.where(t >= 1, jnp.where(t <= w, jnp.where(key >= keymin, 0.0, NEG), NEG), NEG)
    bias = jnp.concatenate([bias1] * SW_REP, axis=1)
    outs = []
    for g in range(SW_KV):
        pair = g // 2
        k2 = kk[:, pair * pair_w:(pair + 1) * pair_w]
        k2r = pltpu.roll(k2, SW_D, axis=1)
        kdup = (jnp.where(low_k, k2, k2r) if g % 2 == 0 else jnp.where(low_k, k2r, k2)).astype(BF16)
        qs = []
        for r in range(SW_REP):
            h = g * SW_REP + r
            q2 = q[:, (h // 2) * pair_w:(h // 2 + 1) * pair_w]
            qs.append(jnp.where(low_q, q2, 0.0) if h % 2 == 0 else jnp.where(low_q, 0.0, q2))
        qst = jnp.concatenate(qs, axis=0).astype(BF16)
        s = lax.dot_general(kdup, qst, (((1,), (1,)), ((), ())), preferred_element_type=F32) + bias
        sink = jnp.concatenate([jnp.broadcast_to(sinks[:, g * SW_REP + r:g * SW_REP + r + 1], (1, w))
                                for r in range(SW_REP)], axis=1)
        mx = jnp.maximum(jnp.max(s, axis=0, keepdims=True), sink)
        p = jnp.exp(s - mx)
        den = jnp.sum(p, axis=0, keepdims=True) + jnp.exp(sink - mx)
        ot = lax.dot_general(vv[:, pair * pair_w:(pair + 1) * pair_w], p.astype(BF16),
                             (((0,), (0,)), ((), ())), preferred_element_type=F32)
        ot = ot[(g % 2) * SW_D:(g % 2 + 1) * SW_D, :] / den
        for pr in range(SW_REP // 2):
            blk = jnp.concatenate([ot[:, (2 * pr) * w:(2 * pr + 1) * w],
                                   ot[:, (2 * pr + 1) * w:(2 * pr + 2) * w]], axis=0)
            outs.append(blk.T)
    o_ref[...] = jnp.concatenate(outs, axis=1)


def _swa_prompt(proj, sinks, nseq, seq):
    m = proj.shape[0]
    nb = seq // WINDOW
    kcol = SW_Q // SW_KVD
    vcol = kcol + 1
    prev = lambda b, i: (b * nb + jnp.maximum(i - 1, 0), kcol)
    prev_v = lambda b, i: (b * nb + jnp.maximum(i - 1, 0), vcol)
    return pl.pallas_call(
        _swa_prompt_kernel,
        grid=(nseq, nb),
        in_specs=[pl.BlockSpec((1, SW_KV * SW_REP), lambda b, i: (0, 0)),
                  pl.BlockSpec((WINDOW, SW_Q), lambda b, i: (b * nb + i, 0)),
                  pl.BlockSpec((WINDOW, SW_KVD), prev),
                  pl.BlockSpec((WINDOW, SW_KVD), lambda b, i: (b * nb + i, kcol)),
                  pl.BlockSpec((WINDOW, SW_KVD), prev_v),
                  pl.BlockSpec((WINDOW, SW_KVD), lambda b, i: (b * nb + i, vcol))],
        out_specs=pl.BlockSpec((WINDOW, SW_Q), lambda b, i: (b * nb + i, 0)),
        out_shape=jax.ShapeDtypeStruct((m, SW_Q), F32),
        compiler_params=_cp(("parallel", "arbitrary")),
        name="swa_prompt",
    )(sinks.reshape(1, -1), proj, proj, proj, proj, proj)


def _swa_decode_kernel(q_ref, kn_ref, vn_ref, sink_ref, kb_ref, vb_ref, o_ref):
    q = q_ref[0] * (SW_D ** -0.5)
    nh = SW_KV * SW_REP
    q4 = jnp.concatenate([q] * SW_KV, axis=-1)
    rowg = lax.broadcasted_iota(jnp.int32, (nh, SW_KVD), 0) // SW_REP
    colg = lax.broadcasted_iota(jnp.int32, (nh, SW_KVD), 1) // SW_D
    own = rowg == colg
    qbd = jnp.where(own, q4, 0.0)
    s = _dot(qbd, kb_ref[0])
    col = lax.broadcasted_iota(jnp.int32, (nh, WINDOW), 1)
    s = jnp.where(col >= 1, s, NEG)
    s_new = jnp.sum(qbd * kn_ref[0], axis=-1, keepdims=True)
    sink = sink_ref[...]
    mx = jnp.maximum(jnp.maximum(jnp.max(s, axis=-1, keepdims=True), s_new), sink)
    p = jnp.exp(s - mx)
    p_new = jnp.exp(s_new - mx)
    den = jnp.sum(p, axis=-1, keepdims=True) + p_new + jnp.exp(sink - mx)
    o = (_dot_nt(p, vb_ref[0]) + p_new * vn_ref[0]) / den
    o = jnp.where(own, o, 0.0)
    out = o[:, 0:SW_D]
    for g in range(1, SW_KV):
        out = out + o[:, g * SW_D:(g + 1) * SW_D]
    o_ref[0] = out


def _swa_decode(q, k_new, v_new, sinks, win_k, win_v):
    nb = q.shape[0]
    nh = SW_KV * SW_REP
    out = pl.pallas_call(
        _swa_decode_kernel,
        grid=(nb,),
        in_specs=[pl.BlockSpec((1, nh, SW_D), lambda b: (b, 0, 0)),
                  pl.BlockSpec((1, 1, SW_KVD), lambda b: (b, 0, 0)),
                  pl.BlockSpec((1, 1, SW_KVD), lambda b: (b, 0, 0)),
                  pl.BlockSpec((nh, 1), lambda b: (0, 0)),
                  pl.BlockSpec((1, SW_KVD, WINDOW), lambda b: (b, 0, 0)),
                  pl.BlockSpec((1, SW_KVD, WINDOW), lambda b: (b, 0, 0))],
        out_specs=pl.BlockSpec((1, nh, SW_D), lambda b: (b, 0, 0)),
        out_shape=jax.ShapeDtypeStruct((nb, nh, SW_D), F32),
        compiler_params=_cp(("parallel",)),
        name="swa_decode",
    )(q.reshape(nb, nh, SW_D), k_new.reshape(nb, 1, SW_KVD), v_new.reshape(nb, 1, SW_KVD),
      sinks.reshape(nh, 1), win_k.transpose(0, 2, 3, 1).reshape(nb, SW_KVD, WINDOW),
      win_v.transpose(0, 2, 3, 1).reshape(nb, SW_KVD, WINDOW))
    return out.reshape(nb, SW_Q)


def _trunk(x, mod, w, grp, *, nseq, seq, caches, bf16_w):
    prompt = caches is None
    tm_ffn, tm_in, tm_out = (min(1024, seq), min(512, seq), min(512, seq)) if prompt else (nseq,) * 3
    tf, tn, tk = 512, 512, 512
    g_norm = w['g_norm']
    ffn_w = (w['w_ffn_gate'], w['w_ffn_up'], w['w_ffn_down'])

    def ffn(x, l, s, sub):
        if prompt:
            return _ffn(x, mod, g_norm, None, l, s, sub, grp, tm=tm_ffn, tf=tf, tiles=bf16_w['ffn', l, s])
        x, bf16_w['ffn', l, s] = _ffn(x, mod, g_norm, ffn_w, l, s, sub, grp, tm=tm_ffn, tf=tf)
        return x

    ks, vs, convs, ssms, wks, wvs = [], [], [], [], [], []
    for l in range(DEPTH):
        j = l // 2
        x = ffn(x, l, 0, 0)
        if l % 2 == 0:
            lam_init = 0.8 - 0.6 * math.exp(-0.3 * l)
            w_in_t = w['w_in_even'].transpose(0, 2, 1)
            if prompt:
                proj, dt_raw = _inproj_res(x, mod, g_norm, bf16_w['in', l], None, w_in_t, l, j, grp,
                                           tm=tm_in, n_dt=SSM_H)
            else:
                proj, dt_raw, bf16_w['in', l] = _inproj(x, mod, g_norm, w_in_t, None, l, j, EVEN_MAIN, grp,
                                                        tm=tm_in, tn=tn, transposed=True, n_dt=SSM_H)
            lp = w['lambda_qk'][j]
            gs = w['g_subln'][j]
            ssm_w = (w['conv_w'][j], w['conv_b'][j], w['dt_bias'][j], w['a_log'][j], w['d_skip'][j],
                     w['g_ssm_norm'][j])
            p3 = proj.reshape(nseq, seq, EVEN_MAIN)
            if prompt:
                oa = _da_prompt(proj, lp, gs, nseq, seq, lam_init)
                ob, ssm_new = _ssd_prompt(proj, dt_raw, *ssm_w, nseq, seq)
                conv_new = p3[:, seq - (CONV_W - 1):, EVEN_MAIN - CONV_DIM:]
            else:
                cache_k, cache_v, page_table, conv_state, ssm_state = caches['even']
                oa = _da_decode(proj[:, :DA_Q], proj[:, DA_Q:DA_Q + DA_KVD],
                                proj[:, DA_Q + DA_KVD:DA_Q + 2 * DA_KVD], lp, gs, cache_k, cache_v,
                                page_table, j, lam_init)
                ob, ssm_new = _ssd_decode(proj, dt_raw, conv_state[j], ssm_state[j], *ssm_w)
                conv_new = jnp.concatenate([conv_state[j][:, 1:], p3[:, :, EVEN_MAIN - CONV_DIM:]], axis=1)
            ks.append(p3[:, :, DA_Q:DA_Q + DA_KVD].reshape(nseq, seq, DA_KV, 2 * DA_D))
            vs.append(p3[:, :, DA_Q + DA_KVD:DA_Q + 2 * DA_KVD].reshape(nseq, seq, DA_KV, 2 * DA_D))
            convs.append(conv_new)
            ssms.append(ssm_new)
            ys, w_out, b_out = [oa, ob], w['w_out_even'], None
        else:
            n_odd = SW_Q + 2 * SW_KVD
            b_in = w['b_in_odd'][j].reshape(1, -1)
            if prompt:
                proj = _inproj_res(x, mod, g_norm, bf16_w['in', l], b_in, None, l, j, grp, tm=tm_in)
            else:
                proj, bf16_w['in', l] = _inproj(x, mod, g_norm, w['w_in_odd'], b_in, l, j, n_odd, grp,
                                                tm=tm_in, tn=tn, transposed=False)
            p3 = proj.reshape(nseq, seq, n_odd)
            if prompt:
                y = _swa_prompt(proj, w['attn_sinks'][j], nseq, seq)
                tail = p3[:, seq - WINDOW:]
                wk_new = tail[:, :, SW_Q:SW_Q + SW_KVD].reshape(nseq, WINDOW, SW_KV, SW_D)
                wv_new = tail[:, :, SW_Q + SW_KVD:].reshape(nseq, WINDOW, SW_KV, SW_D)
            else:
                win_k, win_v = caches['odd']
                k_rows = proj[:, SW_Q:SW_Q + SW_KVD]
                v_rows = proj[:, SW_Q + SW_KVD:]
                y = _swa_decode(proj[:, :SW_Q], k_rows, v_rows, w['attn_sinks'][j], win_k[j], win_v[j])
                wk_new = jnp.concatenate([win_k[j][:, 1:], k_rows.reshape(nseq, 1, SW_KV, SW_D)], axis=1)
                wv_new = jnp.concatenate([win_v[j][:, 1:], v_rows.reshape(nseq, 1, SW_KV, SW_D)], axis=1)
            wks.append(wk_new)
            wvs.append(wv_new)
            ys, w_out, b_out = [y], w['w_out_odd'], w['b_out_odd'][j].reshape(1, -1)
        if prompt:
            x = _outproj_res(x, ys, mod, g_norm, bf16_w['out', l], b_out, l, grp, tm=tm_out)
        else:
            x, bf16_w['out', l] = _outproj(x, ys, mod, g_norm, w_out, b_out, l, j, grp, tm=tm_out, tk=tk)
        x = ffn(x, l, 1, 2)
    return (x, jnp.stack(ks), jnp.stack(vs), jnp.stack(convs), jnp.stack(ssms), jnp.stack(wks), jnp.stack(wvs))


def kernel(x_prompt, x_sample, cache_k, cache_v, state_conv, state_ssm, cache_win_k, cache_win_v, page_table,
           c_prompt, c_sample, w_mod, b_mod, g_norm, w_ffn_gate, w_ffn_up, w_ffn_down, w_in_even, lambda_qk,
           g_subln, conv_w, conv_b, dt_bias, a_log, d_skip, g_ssm_norm, w_out_even, w_in_odd, b_in_odd,
           attn_sinks, w_out_odd, b_out_odd):
    w = dict(g_norm=g_norm, w_ffn_gate=w_ffn_gate, w_ffn_up=w_ffn_up, w_ffn_down=w_ffn_down,
             w_in_even=w_in_even, lambda_qk=lambda_qk, g_subln=g_subln, conv_w=conv_w, conv_b=conv_b,
             dt_bias=dt_bias, a_log=a_log, d_skip=d_skip, g_ssm_norm=g_ssm_norm, w_out_even=w_out_even,
             w_in_odd=w_in_odd, b_in_odd=b_in_odd, attn_sinks=attn_sinks, w_out_odd=w_out_odd,
             b_out_odd=b_out_odd)
    bp, seq, d = x_prompt.shape
    bs, dec_seq, _ = x_sample.shape
    assert dec_seq == 1
    pad = (-(bs + bp)) % 16
    c_all = jnp.concatenate([c_sample, c_prompt, jnp.zeros((pad, d), F32)], axis=0)
    mod = _modulation(c_all, w_mod, b_mod)
    grp_s = _Group(0, bs, None)
    grp_p = _Group(bs, bp, seq)

    bf16_w = {}
    caches = dict(even=(cache_k, cache_v, page_table, state_conv, state_ssm), odd=(cache_win_k, cache_win_v))
    ys, ksm, vsm, cvs, sss, wks, wvs = _trunk(x_sample.reshape(bs, d), mod, w, grp_s, nseq=bs, seq=1,
                                              caches=caches, bf16_w=bf16_w)
    yp, kp, vp, cvp, ssp, wkp, wvp = _trunk(x_prompt.reshape(bp * seq, d), mod, w, grp_p, nseq=bp, seq=seq,
                                            caches=None, bf16_w=bf16_w)
    return (yp.reshape(bp, seq, d), ys.reshape(bs, 1, d), kp, vp, ksm, vsm, cvp, cvs, ssp, sss,
            wkp, wvp, wks, wvs)
```

```python
import functools
import math

import jax
import jax.numpy as jnp
from jax import lax
from jax.experimental import pallas as pl
from jax.experimental.pallas import tpu as pltpu

F32 = jnp.float32
BF16 = jnp.bfloat16

DEPTH = 4
N_SUB = 3
FFN_RES = 0.5
EPS = 1e-6
NEG = -1e30
LOG2E = math.log2(math.e)
PAGE = 128
DA_KV = 4
DA_REP = 2
DA_D = 64
DA_Q = DA_KV * DA_REP * 2 * DA_D
DA_KVD = DA_KV * 2 * DA_D
SSM_INNER = 1024
SSM_P = 64
SSM_H = 16
SSM_G = 2
SSM_N = 128
CONV_W = 4
CONV_DIM = SSM_INNER + 2 * SSM_G * SSM_N
CHUNK = 128
EVEN_MAIN = DA_Q + 2 * DA_KVD + SSM_INNER + CONV_DIM
SW_D = 64
SW_KV = 4
SW_REP = 8
SW_Q = SW_KV * SW_REP * SW_D
SW_KVD = SW_KV * SW_D
WINDOW = 128

SUBLANES = 8
VMEM_LIMIT = 56 * 1024 * 1024


def _cp(sem, limit=VMEM_LIMIT):
    return pltpu.CompilerParams(dimension_semantics=sem, vmem_limit_bytes=limit)


def _silu(x):
    return x / (1.0 + jnp.exp(-x))


def _softplus(x):
    return jnp.maximum(x, 0.0) + jnp.log(1.0 + jnp.exp(-jnp.abs(x)))


def _rms(x, g):
    return x * lax.rsqrt(jnp.mean(x * x, axis=-1, keepdims=True) + EPS) * g


def _dot(a, b):
    return jnp.dot(a.astype(BF16), b.astype(BF16), preferred_element_type=F32)


def _dot_nt(a, b):
    return lax.dot_general(a.astype(BF16), b.astype(BF16), (((1,), (1,)), ((), ())),
                           preferred_element_type=F32)


def _dot_tn(a, b):
    return lax.dot_general(a.astype(BF16), b.astype(BF16), (((0,), (0,)), ((), ())),
                           preferred_element_type=F32)


def _mod_kernel(c_ref, w_ref, b_ref, o_ref):
    h = _silu(c_ref[...])
    o_ref[0] = _dot(h, w_ref[0]) + b_ref[0]


def _modulation(c_all, w_mod, b_mod, tn=1024):
    rows, d = c_all.shape
    depth, _, n = w_mod.shape
    return pl.pallas_call(
        _mod_kernel,
        grid=(depth, n // tn),
        in_specs=[pl.BlockSpec((rows, d), lambda l, j: (0, 0)),
                  pl.BlockSpec((1, d, tn), lambda l, j: (l, 0, j)),
                  pl.BlockSpec((1, 1, tn), lambda l, j: (l, 0, j))],
        out_specs=pl.BlockSpec((1, rows, tn), lambda l, j: (l, 0, j)),
        out_shape=jax.ShapeDtypeStruct((depth, rows, n), F32),
        compiler_params=_cp(("parallel", "arbitrary")),
        name="modulation",
    )(c_all, w_mod, b_mod.reshape(depth, 1, n))


class _Group:
    def __init__(self, row0, nrows, seq):
        assert row0 % SUBLANES == 0
        self.row0, self.nrows, self.seq = row0, nrows, seq
        self.block_rows = nrows if seq is None else SUBLANES
        assert seq is not None or nrows % SUBLANES == 0
        assert seq is None or nrows <= SUBLANES

    def spec(self, layer, col, d):
        rb = self.row0 // self.block_rows
        return pl.BlockSpec((1, self.block_rows, d), lambda i, j: (layer, rb, col))

    def rows(self, ref, tm):
        if self.seq is None:
            return ref[0]
        return ref[0, pl.ds((pl.program_id(0) * tm) // self.seq, 1), :]


def _gn_spec(layer, d):
    return pl.BlockSpec((1, 2 * N_SUB, d), lambda i, j: (layer, 0, 0))


def _ffn_kernel(*refs, nf, res_w, kpre, grp, tm, emit):
    x_ref, sh_ref, sc_ref, gt_ref, gn_ref, wg_ref, wu_ref, wd_ref, o_ref = refs[:9]
    h_ref = refs[-1]
    f = pl.program_id(1)
    def split(n):
        return tuple((k * tm // n, (k + 1) * tm // n) for k in range(n)) if tm % (256 * n) == 0 else ((0, tm),)

    def mod_rows(ref, lo, hi):
        rows = grp.rows(ref, tm)
        return rows if rows.shape[0] == 1 else rows[lo:hi]

    def step(first, last):
        if emit:
            wg, wu, wd = (r[0, 0].astype(BF16) for r in (wg_ref, wu_ref, wd_ref))
            for out_ref, tile in zip(refs[9:12], (wg, wu, wd)):
                out_ref[0] = tile
        else:
            wg, wu, wd = wg_ref[0], wu_ref[0], wd_ref[0]
        for lo, hi in split(2 if first or last else 1):
            if first:
                xn = _rms(x_ref[lo:hi, :], gn_ref[0, kpre:kpre + 1, :])
                h = (xn * (1 + mod_rows(sc_ref, lo, hi)) + mod_rows(sh_ref, lo, hi)).astype(BF16)
                h_ref[lo:hi, :] = h
            else:
                h = h_ref[lo:hi, :]
            g = jnp.dot(h, wg, preferred_element_type=F32)
            u = jnp.dot(h, wu, preferred_element_type=F32)
            y = jnp.dot((_silu(g) * u).astype(BF16), wd, preferred_element_type=F32)
            if not first:
                y = o_ref[lo:hi, :] + y
            if last:
                yn = _rms(y, gn_ref[0, kpre + 1:kpre + 2, :])
                y = x_ref[lo:hi, :] + (res_w * mod_rows(gt_ref, lo, hi)) * yn
            o_ref[lo:hi, :] = y

    if nf == 1:
        step(True, True)
    else:
        pl.when(f == 0)(lambda: step(True, False))
        pl.when(f == nf - 1)(lambda: step(False, True))
        if nf > 2:
            pl.when(jnp.logical_and(f > 0, f < nf - 1))(lambda: step(False, False))


def _ffn(x, mod, g_norm, weights, l, s, sub, grp, *, tm, tf, tiles=None):
    m, d = x.shape
    emit = tiles is None
    if emit:
        nf = weights[0].shape[-1] // tf
        w_specs = [pl.BlockSpec((1, 1, d, tf), lambda i, j: (l, s, 0, j)),
                   pl.BlockSpec((1, 1, d, tf), lambda i, j: (l, s, 0, j)),
                   pl.BlockSpec((1, 1, tf, d), lambda i, j: (l, s, j, 0))]
        w_args = weights
        x_kw = {}
    else:
        nf = tiles[0].shape[0]
        w_specs = [pl.BlockSpec((1, d, tf), lambda i, j: (j, 0, 0)),
                   pl.BlockSpec((1, d, tf), lambda i, j: (j, 0, 0)),
                   pl.BlockSpec((1, tf, d), lambda i, j: (j, 0, 0))]
        w_args = tiles
        x_kw = dict(pipeline_mode=pl.Buffered(1))
    out_specs = [pl.BlockSpec((tm, d), lambda i, j: (i, 0))]
    out_shape = [jax.ShapeDtypeStruct((m, d), F32)]
    if emit:
        assert m == tm
        out_specs += [pl.BlockSpec((1, d, tf), lambda i, j: (j, 0, 0)),
                      pl.BlockSpec((1, d, tf), lambda i, j: (j, 0, 0)),
                      pl.BlockSpec((1, tf, d), lambda i, j: (j, 0, 0))]
        out_shape += [jax.ShapeDtypeStruct((nf, d, tf), BF16), jax.ShapeDtypeStruct((nf, d, tf), BF16),
                      jax.ShapeDtypeStruct((nf, tf, d), BF16)]
    kern = functools.partial(_ffn_kernel, nf=nf, res_w=FFN_RES, kpre=2 * sub, grp=grp, tm=tm, emit=emit)
    out = pl.pallas_call(
        kern,
        grid=(m // tm, nf),
        in_specs=[pl.BlockSpec((tm, d), lambda i, j: (i, 0), **x_kw),
                  grp.spec(l, 3 * sub + 0, d), grp.spec(l, 3 * sub + 1, d), grp.spec(l, 3 * sub + 2, d),
                  _gn_spec(l, d)] + w_specs,
        out_specs=out_specs,
        out_shape=out_shape,
        scratch_shapes=[pltpu.VMEM((tm, d), BF16)],
        compiler_params=_cp(("parallel", "arbitrary")),
        name="ffn_emit" if emit else "ffn",
    )(x, mod, mod, mod, g_norm, *w_args)
    return (out[0], tuple(out[1:])) if emit else out[0]


def _inproj_kernel(*refs, transposed, has_bias, has_dt, grp, tm):
    refs = list(refs)
    x_ref, sh_ref, sc_ref, gn_ref, w_ref = refs[:5]
    rest = refs[5:]
    b_ref = rest.pop(0) if has_bias else None
    wdt_ref = rest.pop(0) if has_dt else None
    o_ref = rest.pop(0)
    dt_ref = rest.pop(0) if has_dt else None
    wbo_ref, h_ref, wb_ref = rest
    i = pl.program_id(0)
    j = pl.program_id(1)

    @pl.when(j == 0)
    def _():
        xn = _rms(x_ref[...], gn_ref[0, 2:3, :])
        h = (xn * (1 + grp.rows(sc_ref, tm)) + grp.rows(sh_ref, tm)).astype(BF16)
        h_ref[...] = h
        if has_dt:
            dt_ref[...] = _dot_nt(h, wdt_ref[0])

    @pl.when(i == 0)
    def _():
        wf = w_ref[0]
        wb_ref[j] = wf.astype(BF16)
        wbo_ref[...] = (wf.T if transposed else wf).astype(BF16)

    w = wb_ref[j]
    if transposed:
        o = lax.dot_general(h_ref[...], w, (((1,), (1,)), ((), ())), preferred_element_type=F32)
    else:
        o = jnp.dot(h_ref[...], w, preferred_element_type=F32)
    o_ref[...] = o + b_ref[...] if has_bias else o


def _inproj(x, mod, g_norm, w, b, l, layer, n, grp, *, tm, tn, transposed, n_dt=0):
    m, d = x.shape
    assert m == tm
    nj = n // tn
    has_bias = b is not None
    has_dt = n_dt > 0
    wcol = lambda i, j: jnp.where(i == 0, j, nj - 1)
    if transposed:
        w_spec = pl.BlockSpec((1, tn, d), lambda i, j: (layer, wcol(i, j), 0))
        wb_shape = (nj, tn, d)
    else:
        w_spec = pl.BlockSpec((1, d, tn), lambda i, j: (layer, 0, wcol(i, j)))
        wb_shape = (nj, d, tn)
    in_specs = [pl.BlockSpec((tm, d), lambda i, j: (i, 0)),
                grp.spec(l, 3, d), grp.spec(l, 4, d), _gn_spec(l, d), w_spec]
    args = [x, mod, mod, g_norm, w]
    if has_bias:
        in_specs.append(pl.BlockSpec((1, tn), lambda i, j: (0, j)))
        args.append(b)
    out_specs = [pl.BlockSpec((tm, tn), lambda i, j: (i, j))]
    out_shape = [jax.ShapeDtypeStruct((m, n), F32)]
    if has_dt:
        assert transposed and n % n_dt == 0
        in_specs.append(pl.BlockSpec((1, n_dt, d), lambda i, j: (layer, n // n_dt, 0)))
        args.append(w)
        out_specs.append(pl.BlockSpec((tm, n_dt), lambda i, j: (i, 0)))
        out_shape.append(jax.ShapeDtypeStruct((m, n_dt), F32))
    out_specs.append(pl.BlockSpec((d, tn), lambda i, j: (0, j)))
    out_shape.append(jax.ShapeDtypeStruct((d, n), BF16))
    return pl.pallas_call(
        functools.partial(_inproj_kernel, transposed=transposed, has_bias=has_bias, has_dt=has_dt,
                          grp=grp, tm=tm),
        grid=(m // tm, nj),
        in_specs=in_specs,
        out_specs=out_specs,
        out_shape=out_shape,
        scratch_shapes=[pltpu.VMEM((tm, d), BF16), pltpu.VMEM(wb_shape, BF16)],
        compiler_params=_cp(("arbitrary", "arbitrary")),
        name="inproj_emit",
    )(*args)


def _inproj_res_kernel(*refs, n, chunk, has_bias, has_dt, grp, tm):
    refs = list(refs)
    x_ref, sh_ref, sc_ref, gn_ref, wb_ref = refs[:5]
    rest = refs[5:]
    b_ref = rest.pop(0) if has_bias else None
    wdt_ref = rest.pop(0) if has_dt else None
    o_ref = rest.pop(0)
    scale, shift = grp.rows(sc_ref, tm), grp.rows(sh_ref, tm)
    assert scale.shape[0] == 1
    half = tm // 2
    for lo in (0, half):
        xn = _rms(x_ref[lo:lo + half, :], gn_ref[0, 2:3, :])
        h = (xn * (1 + scale) + shift).astype(BF16)
        if has_dt:
            rest[0][lo:lo + half, :] = _dot_nt(h, wdt_ref[0])
        for c in range(n // chunk):
            cols = slice(c * chunk, (c + 1) * chunk)
            o = jnp.dot(h, wb_ref[:, cols], preferred_element_type=F32)
            o_ref[lo:lo + half, cols] = o + b_ref[:, cols] if has_bias else o
            if has_dt and c * chunk in (DA_Q, DA_Q + DA_KVD):
                kv_ref = rest[1] if c * chunk == DA_Q else rest[2]
                for g in range(DA_KV):
                    kv_ref[pl.ds(lo * DA_KV + g, half, stride=DA_KV), :] = o[:, g * 2 * DA_D:(g + 1) * 2 * DA_D]


def _inproj_res(x, mod, g_norm, wb, b, w_dt, l, layer, grp, *, tm, n_dt=0, chunk=512):
    assert chunk == DA_KVD
    m, d = x.shape
    n = wb.shape[1]
    has_bias = b is not None
    has_dt = n_dt > 0
    once = dict(pipeline_mode=pl.Buffered(1))
    in_specs = [pl.BlockSpec((tm, d), lambda i, j: (i, 0)),
                grp.spec(l, 3, d), grp.spec(l, 4, d), _gn_spec(l, d),
                pl.BlockSpec((d, n), lambda i, j: (0, 0), **once)]
    args = [x, mod, mod, g_norm, wb]
    if has_bias:
        in_specs.append(pl.BlockSpec((1, n), lambda i, j: (0, 0)))
        args.append(b)
    out_specs = [pl.BlockSpec((tm, n), lambda i, j: (i, 0))]
    out_shape = [jax.ShapeDtypeStruct((m, n), F32)]
    if has_dt:
        in_specs.append(pl.BlockSpec((1, n_dt, d), lambda i, j: (layer, n // n_dt, 0)))
        args.append(w_dt)
        out_specs.append(pl.BlockSpec((tm, n_dt), lambda i, j: (i, 0)))
        out_shape.append(jax.ShapeDtypeStruct((m, n_dt), F32))
        for _ in range(2):
            out_specs.append(pl.BlockSpec((tm * DA_KV, 2 * DA_D), lambda i, j: (i, 0)))
            out_shape.append(jax.ShapeDtypeStruct((m * DA_KV, 2 * DA_D), F32))
    out = pl.pallas_call(
        functools.partial(_inproj_res_kernel, n=n, chunk=chunk, has_bias=has_bias, has_dt=has_dt,
                          grp=grp, tm=tm),
        grid=(m // tm, 1),
        in_specs=in_specs,
        out_specs=out_specs,
        out_shape=out_shape,
        compiler_params=_cp(("parallel", "arbitrary")),
        name="inproj",
    )(*args)
    return out if has_dt else out[0]


def _outproj_kernel(*refs, bounds, has_bias, grp, tm):
    refs = list(refs)
    x_ref = refs.pop(0)
    y_refs = [refs.pop(0) for _ in bounds]
    gt_ref, gn_ref, w_ref = refs[:3]
    rest = refs[3:]
    b_ref = rest.pop(0) if has_bias else None
    o_ref, wbo_ref, wb_ref = rest
    nk = bounds[-1][1]
    i = pl.program_id(0)
    k = pl.program_id(1)

    @pl.when(i == 0)
    def _():
        wb = w_ref[0].astype(BF16)
        wb_ref[k] = wb
        wbo_ref[...] = wb

    @pl.when(k == 0)
    def _():
        o_ref[...] = jnp.zeros(o_ref.shape, F32)

    for y_ref, (lo, hi) in zip(y_refs, bounds):
        @pl.when(jnp.logical_and(k >= lo, k < hi))
        def _(y_ref=y_ref):
            o_ref[...] += jnp.dot(y_ref[...].astype(BF16), wb_ref[k], preferred_element_type=F32)

    @pl.when(k == nk - 1)
    def _():
        y = o_ref[...] + b_ref[...] if has_bias else o_ref[...]
        o_ref[...] = x_ref[...] + grp.rows(gt_ref, tm) * _rms(y, gn_ref[0, 3:4, :])


def _outproj(x, ys, mod, g_norm, w, b, l, layer, grp, *, tm, tk):
    m, d = x.shape
    has_bias = b is not None
    bounds, lo = [], 0
    for y in ys:
        bounds.append((lo, lo + y.shape[1] // tk))
        lo = bounds[-1][1]
    nk = lo
    assert nk * tk == w.shape[1]

    def y_spec(lo, hi):
        return pl.BlockSpec((tm, tk), lambda i, k: (i, jnp.clip(k - lo, 0, hi - lo - 1)))

    in_specs = ([pl.BlockSpec((tm, d), lambda i, k: (i, 0))]
                + [y_spec(lo, hi) for lo, hi in bounds]
                + [grp.spec(l, 5, d), _gn_spec(l, d),
                   pl.BlockSpec((1, tk, d), lambda i, k: (layer, jnp.where(i == 0, k, nk - 1), 0))])
    args = [x, *ys, mod, g_norm, w]
    if has_bias:
        in_specs.append(pl.BlockSpec((1, d), lambda i, k: (0, 0)))
        args.append(b)
    assert m == tm
    return pl.pallas_call(
        functools.partial(_outproj_kernel, bounds=tuple(bounds), has_bias=has_bias, grp=grp, tm=tm),
        grid=(m // tm, nk),
        in_specs=in_specs,
        out_specs=[pl.BlockSpec((tm, d), lambda i, k: (i, 0)), pl.BlockSpec((tk, d), lambda i, k: (k, 0))],
        out_shape=[jax.ShapeDtypeStruct((m, d), F32), jax.ShapeDtypeStruct((nk * tk, d), BF16)],
        scratch_shapes=[pltpu.VMEM((nk, tk, d), BF16)],
        compiler_params=_cp(("arbitrary", "arbitrary")),
        name="outproj_emit",
    )(*args)


def _outproj_res_kernel(*refs, n_y, has_bias, grp, tm):
    x_ref = refs[0]
    y_refs = refs[1:1 + n_y]
    gt_ref, gn_ref, wb_ref = refs[1 + n_y:4 + n_y]
    b_ref = refs[4 + n_y] if has_bias else None
    o_ref = refs[-1]
    gate = grp.rows(gt_ref, tm)
    assert gate.shape[0] == 1
    half = tm // 2
    for lo in (0, half):
        rows = slice(lo, lo + half)
        ycat = jnp.concatenate([r[rows, :].astype(BF16) for r in y_refs], axis=-1)
        y = jnp.dot(ycat, wb_ref[...], preferred_element_type=F32)
        if has_bias:
            y = y + b_ref[...]
        o_ref[rows, :] = x_ref[rows, :] + gate * _rms(y, gn_ref[0, 3:4, :])


def _outproj_res(x, ys, mod, g_norm, wb, b, l, grp, *, tm):
    m, d = x.shape
    has_bias = b is not None
    in_specs = ([pl.BlockSpec((tm, d), lambda i, k: (i, 0))]
                + [pl.BlockSpec((tm, y.shape[1]), lambda i, k: (i, 0)) for y in ys]
                + [grp.spec(l, 5, d), _gn_spec(l, d),
                   pl.BlockSpec(wb.shape, lambda i, k: (0, 0), pipeline_mode=pl.Buffered(1))])
    args = [x, *ys, mod, g_norm, wb]
    if has_bias:
        in_specs.append(pl.BlockSpec((1, d), lambda i, k: (0, 0)))
        args.append(b)
    return pl.pallas_call(
        functools.partial(_outproj_res_kernel, n_y=len(ys), has_bias=has_bias, grp=grp, tm=tm),
        grid=(m // tm, 1),
        in_specs=in_specs,
        out_specs=pl.BlockSpec((tm, d), lambda i, k: (i, 0)),
        out_shape=jax.ShapeDtypeStruct((m, d), F32),
        compiler_params=_cp(("parallel", "arbitrary")),
        name="outproj",
    )(*args)


def _lambda(lp, lam_init):
    a = jnp.sum(lp[0:1] * lp[1:2], axis=-1, keepdims=True)
    b = jnp.sum(lp[2:3] * lp[3:4], axis=-1, keepdims=True)
    return jnp.exp(a) - jnp.exp(b) + lam_init


def _da_prompt_kernel(q_ref, k_ref, v_ref, lp_ref, gs_ref, o_ref, *, tq, lam_init):
    i = pl.program_id(2)
    lam = _lambda(lp_ref[...], lam_init)
    q = q_ref[...] * (DA_D ** -0.5 * LOG2E)
    low = lax.broadcasted_iota(jnp.int32, (tq, 2 * DA_D), 1) < DA_D
    heads = (q[:, :2 * DA_D], q[:, 2 * DA_D:])
    qp = (jnp.concatenate([jnp.where(low, h, 0.0) for h in heads], axis=0).astype(BF16),
          jnp.concatenate([jnp.where(low, 0.0, h) for h in heads], axis=0).astype(BF16))
    key = lax.broadcasted_iota(jnp.int32, (tq, 2 * tq), 0)
    col = lax.broadcasted_iota(jnp.int32, (tq, 2 * tq), 1)
    causal_bias = jnp.where(key <= jnp.where(col >= tq, col - tq, col), 0.0, NEG)

    def step(j, carry, diagonal):
        off = pl.multiple_of(j * tq, tq)
        kb = k_ref[pl.ds(off, tq), :].astype(BF16)
        vb = v_ref[pl.ds(off, tq), :].astype(BF16)
        out = []
        for mi in range(2):
            m_old, l_old, acc = carry[3 * mi:3 * mi + 3]
            s = lax.dot_general(kb, qp[mi], (((1,), (1,)), ((), ())), preferred_element_type=F32)
            if diagonal:
                s = s + causal_bias
            m_new = jnp.maximum(m_old, jnp.max(s, axis=0, keepdims=True))
            alpha = jnp.exp2(m_old - m_new)
            p = jnp.exp2(s - m_new)
            l_new = alpha * l_old + jnp.sum(p, axis=0, keepdims=True)
            acc = alpha * acc + lax.dot_general(vb, p.astype(BF16), (((0,), (0,)), ((), ())),
                                                preferred_element_type=F32)
            out += [m_new, l_new, acc]
        return tuple(out)

    m0 = jnp.full((1, 2 * tq), NEG, F32)
    l0 = jnp.zeros((1, 2 * tq), F32)
    a0 = jnp.zeros((2 * DA_D, 2 * tq), F32)
    res = lax.fori_loop(0, i, lambda j, c: step(j, c, False), (m0, l0, a0, m0, l0, a0))
    res = step(i, res, True)
    a = res[2] / res[1] - lam * (res[5] / res[4])
    a = a * lax.rsqrt(jnp.mean(a * a, axis=0, keepdims=True) + EPS) * gs_ref[...] * (1 - lam_init)
    a = a.T
    o_ref[:, 0:128] = a[:tq]
    o_ref[:, 128:256] = a[tq:]


def _da_prompt(proj, lp, gs, nseq, seq, lam_init, tq=512):
    m = proj.shape[0]
    tq = min(tq, seq)
    nq = seq // tq
    kcol = DA_Q // 128
    vcol = (DA_Q + DA_KVD) // 128
    return pl.pallas_call(
        functools.partial(_da_prompt_kernel, tq=tq, lam_init=lam_init),
        grid=(nseq, DA_KV, nq),
        in_specs=[pl.BlockSpec((tq, 256), lambda b, g, i: (b * nq + i, g)),
                  pl.BlockSpec((seq, 128), lambda b, g, i: (b, kcol + g)),
                  pl.BlockSpec((seq, 128), lambda b, g, i: (b, vcol + g)),
                  pl.BlockSpec((4, DA_D), lambda b, g, i: (0, 0)),
                  pl.BlockSpec((2 * DA_D, 1), lambda b, g, i: (0, 0))],
        out_specs=pl.BlockSpec((tq, 256), lambda b, g, i: (b * nq + i, g)),
        out_shape=jax.ShapeDtypeStruct((m, DA_Q), F32),
        compiler_params=_cp(("parallel", "parallel", "arbitrary")),
        name="diff_attn_prompt",
    )(proj, proj, proj, lp, gs.reshape(-1, 1))


def _da_decode_kernel(pt_ref, qm_ref, kn_ref, vn_ref, lp_ref, gs_ref, *refs, pc, nc, lam_init):
    del pt_ref
    k_refs = refs[:pc]
    v_refs = refs[pc:2 * pc]
    o_ref, m_ref, l_ref, acc_ref, kb_ref, vb_ref = refs[2 * pc:]
    c = pl.program_id(1)

    @pl.when(c == 0)
    def _():
        m_ref[...] = jnp.full(m_ref.shape, NEG, F32)
        l_ref[...] = jnp.zeros(l_ref.shape, F32)
        acc_ref[...] = jnp.zeros(acc_ref.shape, F32)

    rows = PAGE * DA_KV
    for t in range(pc):
        kb_ref[t * rows:(t + 1) * rows, :] = k_refs[t][0, 0].astype(BF16)
        vb_ref[t * rows:(t + 1) * rows, :] = v_refs[t][0, 0].astype(BF16)
    qm = qm_ref[0] * (DA_D ** -0.5)
    s = _dot_nt(qm, kb_ref[...])
    row_g = (lax.broadcasted_iota(jnp.int32, s.shape, 0) // DA_REP) & (DA_KV - 1)
    col_g = lax.broadcasted_iota(jnp.int32, s.shape, 1) & (DA_KV - 1)
    s = jnp.where(row_g == col_g, s, NEG)
    m_old = m_ref[...]
    m_new = jnp.maximum(m_old, jnp.max(s, axis=-1, keepdims=True))
    alpha = jnp.exp(m_old - m_new)
    p = jnp.exp(s - m_new)
    l_ref[...] = alpha * l_ref[...] + jnp.sum(p, axis=-1, keepdims=True)
    acc_ref[...] = alpha * acc_ref[...] + jnp.dot(p.astype(BF16), vb_ref[...], preferred_element_type=F32)
    m_ref[...] = m_new

    @pl.when(c == nc - 1)
    def _():
        lam = _lambda(lp_ref[...], lam_init)
        s_new = jnp.sum(qm * kn_ref[0], axis=-1, keepdims=True)
        m_old = m_ref[...]
        m_fin = jnp.maximum(m_old, s_new)
        alpha = jnp.exp(m_old - m_fin)
        p_new = jnp.exp(s_new - m_fin)
        l_fin = alpha * l_ref[...] + p_new
        acc = alpha * acc_ref[...] + p_new * vn_ref[0]
        o = acc / l_fin
        a = o[0:8] - lam * o[8:16]
        o_ref[0] = _rms(a, gs_ref[...]) * (1 - lam_init)


def _da_decode(q, k_new, v_new, lp, gs, cache_k, cache_v, page_table, j, lam_init, pc=32):
    nb = q.shape[0]
    n_pages = page_table.shape[1]
    pc = min(pc, n_pages)
    nc = n_pages // pc
    rows = PAGE * DA_KV
    ck = cache_k.reshape(cache_k.shape[0], cache_k.shape[1], rows, 2 * DA_D)
    cv = cache_v.reshape(cache_v.shape[0], cache_v.shape[1], rows, 2 * DA_D)
    qt = q.reshape(nb, DA_KV, DA_REP, 2, DA_D).transpose(0, 3, 1, 2, 4)
    own_m = jnp.eye(2, dtype=bool)[None, :, None, None, :, None]
    qm = jnp.where(own_m, qt[:, :, :, :, None, :], 0.0).reshape(nb, 16, 2 * DA_D)
    rep = lambda a: jnp.broadcast_to(a.reshape(nb, 1, DA_KV, 1, 2 * DA_D),
                                     (nb, 2, DA_KV, DA_REP, 2 * DA_D)).reshape(nb, 16, 2 * DA_D)

    def page_spec(t):
        return pl.BlockSpec((1, 1, rows, 2 * DA_D), lambda b, c, pt: (j, pt[b, c * pc + t], 0, 0))

    row_spec = pl.BlockSpec((1, 16, 2 * DA_D), lambda b, c, pt: (b, 0, 0))
    grid_spec = pltpu.PrefetchScalarGridSpec(
        num_scalar_prefetch=1,
        grid=(nb, nc),
        in_specs=[row_spec, row_spec, row_spec,
                  pl.BlockSpec((4, DA_D), lambda b, c, pt: (0, 0)),
                  pl.BlockSpec((1, 2 * DA_D), lambda b, c, pt: (0, 0))]
                 + [page_spec(t) for t in range(pc)] + [page_spec(t) for t in range(pc)],
        out_specs=pl.BlockSpec((1, 8, 128), lambda b, c, pt: (b, 0, 0)),
        scratch_shapes=[pltpu.VMEM((16, 1), F32), pltpu.VMEM((16, 1), F32),
                        pltpu.VMEM((16, 2 * DA_D), F32),
                        pltpu.VMEM((pc * rows, 2 * DA_D), BF16), pltpu.VMEM((pc * rows, 2 * DA_D), BF16)])
    out = pl.pallas_call(
        functools.partial(_da_decode_kernel, pc=pc, nc=nc, lam_init=lam_init),
        grid_spec=grid_spec,
        out_shape=jax.ShapeDtypeStruct((nb, 8, 128), F32),
        compiler_params=_cp(("parallel", "arbitrary")),
        name="diff_attn_decode",
    )(page_table, qm, rep(k_new), rep(v_new), lp, gs.reshape(1, -1), *([ck] * pc), *([cv] * pc))
    return out.reshape(nb, DA_Q)


def _split3(x):
    x1 = x.astype(BF16)
    r1 = x - x1.astype(F32)
    x2 = r1.astype(BF16)
    x3 = (r1 - x2.astype(F32)).astype(BF16)
    return x1, x2, x3


def _group_norm_gate(y, z, gn):
    yz = y * _silu(z)
    w = SSM_INNER // SSM_G
    parts = []
    for g in range(SSM_G):
        seg = yz[:, g * w:(g + 1) * w]
        parts.append(seg * lax.rsqrt(jnp.mean(seg * seg, axis=-1, keepdims=True) + EPS))
    return jnp.concatenate(parts, axis=-1) * gn


def _ssd_prompt_kernel(xbc_ref, z_ref, dt_ref, dtt_ref, cw_ref, cb_ref, dtb_ref, dtbt_ref,
                       alog_ref, alogt_ref, dskip_ref, gn_ref, y_ref, st_ref, ext_ref, h_ref, *, nchunk):
    c = pl.program_id(1)
    q = CHUNK

    @pl.when(c == 0)
    def _():
        ext_ref[0:8, :] = jnp.zeros((8, CONV_DIM), F32)
        h_ref[...] = jnp.zeros(h_ref.shape, F32)

    @pl.when(c > 0)
    def _():
        ext_ref[0:8, :] = ext_ref[q:q + 8, :]

    ext_ref[8:q + 8, :] = xbc_ref[...]
    cw = cw_ref[...]
    acc = ext_ref[5:q + 5, :] * cw[0:1]
    for kk in range(1, CONV_W):
        acc = acc + ext_ref[5 + kk:q + 5 + kk, :] * cw[kk:kk + 1]
    xa = _silu(acc + cb_ref[...])
    xs = xa[:, :SSM_INNER]
    bm = xa[:, SSM_INNER:SSM_INNER + SSM_G * SSM_N].astype(BF16)
    cm = xa[:, SSM_INNER + SSM_G * SSM_N:].astype(BF16)

    dt = _softplus(dt_ref[...] + dtb_ref[...])
    dtt = _softplus(dtt_ref[...] + dtbt_ref[...])
    a = dt * (-jnp.exp(alog_ref[...]))
    at = dtt * (-jnp.exp(alogt_ref[...]))
    ri = lax.broadcasted_iota(jnp.int32, (q, q), 0)
    ci = lax.broadcasted_iota(jnp.int32, (q, q), 1)
    lower = ri >= ci
    tril = jnp.where(lower, 1.0, 0.0).astype(BF16)
    triu = jnp.where(ri <= ci, 1.0, 0.0).astype(BF16)
    cs = sum(jnp.dot(tril, part, preferred_element_type=F32) for part in _split3(a))
    cst = sum(jnp.dot(part, triu, preferred_element_type=F32) for part in _split3(at))
    tot = cs[q - 1:q, :]
    e_cs = jnp.exp(cs)
    e_dec = jnp.exp(tot - cs)
    e_tot = jnp.exp(tot)
    dskip = dskip_ref[...]

    ys = []
    for g in range(SSM_G):
        bg = bm[:, g * SSM_N:(g + 1) * SSM_N]
        cg = cm[:, g * SSM_N:(g + 1) * SSM_N]
        cb = lax.dot_general(cg, bg, (((1,), (1,)), ((), ())), preferred_element_type=F32)
        for r in range(SSM_H // SSM_G):
            h = g * (SSM_H // SSM_G) + r
            lmat = jnp.exp(jnp.where(lower, cs[:, h:h + 1] - cst[h:h + 1, :], NEG))
            x_h = xs[:, h * SSM_P:(h + 1) * SSM_P]
            xdt = x_h * dt[:, h:h + 1]
            y_diag = _dot(cb * lmat, xdt)
            hp = h_ref[h]
            y_off = lax.dot_general(cg, hp.astype(BF16), (((1,), (1,)), ((), ())),
                                    preferred_element_type=F32) * e_cs[:, h:h + 1]
            upd = lax.dot_general((xdt * e_dec[:, h:h + 1]).astype(BF16), bg, (((0,), (0,)), ((), ())),
                                  preferred_element_type=F32)
            h_ref[h] = e_tot[:, h:h + 1] * hp + upd
            ys.append(y_diag + y_off + dskip[:, h:h + 1] * x_h)
    y = jnp.concatenate(ys, axis=-1)
    y_ref[...] = _group_norm_gate(y, z_ref[...], gn_ref[...])

    @pl.when(c == nchunk - 1)
    def _():
        st_ref[0] = h_ref[...]


def _ssd_prompt(proj, dt_raw, cw, cb, dtb, alog, dskip, gn, nseq, seq):
    m = proj.shape[0]
    nchunk = seq // CHUNK
    zcol = (DA_Q + 2 * DA_KVD) // SSM_INNER
    xcol = (DA_Q + 2 * DA_KVD + SSM_INNER) // CONV_DIM
    const = lambda b, c: (0, 0)
    y, st = pl.pallas_call(
        functools.partial(_ssd_prompt_kernel, nchunk=nchunk),
        grid=(nseq, nchunk),
        in_specs=[pl.BlockSpec((CHUNK, CONV_DIM), lambda b, c: (b * nchunk + c, xcol)),
                  pl.BlockSpec((CHUNK, SSM_INNER), lambda b, c: (b * nchunk + c, zcol)),
                  pl.BlockSpec((CHUNK, SSM_H), lambda b, c: (b * nchunk + c, 0)),
                  pl.BlockSpec((SSM_H, CHUNK), lambda b, c: (0, b * nchunk + c)),
                  pl.BlockSpec((CONV_W, CONV_DIM), const),
                  pl.BlockSpec((1, CONV_DIM), const),
                  pl.BlockSpec((1, SSM_H), const),
                  pl.BlockSpec((SSM_H, 1), const),
                  pl.BlockSpec((1, SSM_H), const),
                  pl.BlockSpec((SSM_H, 1), const),
                  pl.BlockSpec((1, SSM_H), const),
                  pl.BlockSpec((1, SSM_INNER), const)],
        out_specs=[pl.BlockSpec((CHUNK, SSM_INNER), lambda b, c: (b * nchunk + c, 0)),
                   pl.BlockSpec((1, SSM_H, SSM_P, SSM_N), lambda b, c: (b, 0, 0, 0))],
        out_shape=[jax.ShapeDtypeStruct((m, SSM_INNER), F32),
                   jax.ShapeDtypeStruct((nseq, SSM_H, SSM_P, SSM_N), F32)],
        scratch_shapes=[pltpu.VMEM((CHUNK + 8, CONV_DIM), F32),
                        pltpu.VMEM((SSM_H, SSM_P, SSM_N), F32)],
        compiler_params=_cp(("parallel", "arbitrary")),
        name="ssd_prompt",
    )(proj, proj, dt_raw, dt_raw.T, cw, cb.reshape(1, -1), dtb.reshape(1, -1), dtb.reshape(-1, 1),
      alog.reshape(1, -1), alog.reshape(-1, 1), dskip.reshape(1, -1), gn.reshape(1, -1))
    return y, st


def _ssd_dec_pre_kernel(xbc_ref, cst_ref, dt_ref, cw_ref, cb_ref, dtb_ref, xa_ref, dts_ref):
    cw = cw_ref[...]
    acc = cst_ref[0] * cw[0:1]
    for kk in range(1, CONV_W - 1):
        acc = acc + cst_ref[kk] * cw[kk:kk + 1]
    acc = acc + xbc_ref[...] * cw[CONV_W - 1:CONV_W]
    xa_ref[...] = _silu(acc + cb_ref[...])
    dts_ref[...] = _softplus(dt_ref[...] + dtb_ref[...])


def _ssd_dec_state_kernel(xt_ref, dts_ref, alog_ref, b_ref, c_ref, dskip_ref, h_ref, yt_ref, hn_ref):
    xt = xt_ref[0]
    dts = dts_ref[0]
    a = dts * (-jnp.exp(alog_ref[...]))
    e_a = jnp.exp(a)
    dskip = dskip_ref[...]
    cols = []
    for h in range(SSM_H):
        g = h // (SSM_H // SSM_G)
        xcol = xt[:, h:h + 1]
        brow = b_ref[0][:, g * SSM_N:(g + 1) * SSM_N]
        crow = c_ref[0][:, g * SSM_N:(g + 1) * SSM_N]
        hn = e_a[:, h:h + 1] * h_ref[0, 0, h] + (xcol * dts[:, h:h + 1]) * brow
        hn_ref[0, h] = hn
        cols.append(jnp.sum(hn * crow, axis=-1, keepdims=True) + dskip[:, h:h + 1] * xcol)
    yt_ref[0] = jnp.concatenate(cols, axis=-1)


def _ssd_dec_post_kernel(y_ref, z_ref, gn_ref, o_ref):
    o_ref[...] = _group_norm_gate(y_ref[...], z_ref[...], gn_ref[...])


def _ssd_decode(proj, dt_raw, conv_state, ssm_state, layer, cw, cb, dtb, alog, dskip, gn):
    nb = proj.shape[0]
    zcol = (DA_Q + 2 * DA_KVD) // SSM_INNER
    xcol = (DA_Q + 2 * DA_KVD + SSM_INNER) // CONV_DIM
    full = lambda shape: pl.BlockSpec(shape, lambda i: (0,) * len(shape))
    xa, dts = pl.pallas_call(
        _ssd_dec_pre_kernel,
        grid=(1,),
        in_specs=[pl.BlockSpec((nb, CONV_DIM), lambda i: (0, xcol)),
                  full((CONV_W - 1, nb, CONV_DIM)),
                  full((nb, SSM_H)),
                  full((CONV_W, CONV_DIM)),
                  full((1, CONV_DIM)),
                  full((1, SSM_H))],
        out_specs=[full((nb, CONV_DIM)), full((nb, SSM_H))],
        out_shape=[jax.ShapeDtypeStruct((nb, CONV_DIM), F32), jax.ShapeDtypeStruct((nb, SSM_H), F32)],
        compiler_params=_cp(("arbitrary",)),
        name="ssd_decode_pre",
    )(proj, conv_state.transpose(1, 0, 2), dt_raw, cw, cb.reshape(1, -1), dtb.reshape(1, -1))
    xt = xa[:, :SSM_INNER].reshape(nb, SSM_H, SSM_P).transpose(0, 2, 1)
    bmat = xa[:, SSM_INNER:SSM_INNER + SSM_G * SSM_N].reshape(nb, 1, SSM_G * SSM_N)
    cmat = xa[:, SSM_INNER + SSM_G * SSM_N:].reshape(nb, 1, SSM_G * SSM_N)
    yt, h_new = pl.pallas_call(
        _ssd_dec_state_kernel,
        grid=(nb,),
        in_specs=[pl.BlockSpec((1, SSM_P, SSM_H), lambda b: (b, 0, 0)),
                  pl.BlockSpec((1, 1, SSM_H), lambda b: (b, 0, 0)),
                  pl.BlockSpec((1, SSM_H), lambda b: (0, 0)),
                  pl.BlockSpec((1, 1, SSM_G * SSM_N), lambda b: (b, 0, 0)),
                  pl.BlockSpec((1, 1, SSM_G * SSM_N), lambda b: (b, 0, 0)),
                  pl.BlockSpec((1, SSM_H), lambda b: (0, 0)),
                  pl.BlockSpec((1, 1, SSM_H, SSM_P, SSM_N), lambda b: (layer, b, 0, 0, 0))],
        out_specs=[pl.BlockSpec((1, SSM_P, SSM_H), lambda b: (b, 0, 0)),
                   pl.BlockSpec((1, SSM_H, SSM_P, SSM_N), lambda b: (b, 0, 0, 0))],
        out_shape=[jax.ShapeDtypeStruct((nb, SSM_P, SSM_H), F32),
                   jax.ShapeDtypeStruct(ssm_state.shape[1:], F32)],
        compiler_params=_cp(("parallel",)),
        name="ssd_decode_state",
    )(xt, dts.reshape(nb, 1, SSM_H), alog.reshape(1, -1), bmat, cmat, dskip.reshape(1, -1), ssm_state)
    y = yt.transpose(0, 2, 1).reshape(nb, SSM_INNER)
    ob = pl.pallas_call(
        _ssd_dec_post_kernel,
        grid=(1,),
        in_specs=[full((nb, SSM_INNER)),
                  pl.BlockSpec((nb, SSM_INNER), lambda i: (0, zcol)),
                  full((1, SSM_INNER))],
        out_specs=full((nb, SSM_INNER)),
        out_shape=jax.ShapeDtypeStruct((nb, SSM_INNER), F32),
        compiler_params=_cp(("arbitrary",)),
        name="ssd_decode_post",
    )(y, proj, gn.reshape(1, -1))
    return ob, h_new


def _swa_prompt_kernel(sink_ref, q_ref, kp_ref, kc_ref, vp_ref, vc_ref, o_ref):
    i = pl.program_id(1)
    w = WINDOW
    pair_w = 2 * SW_D
    sinks = sink_ref[...] * LOG2E
    q = q_ref[...] * (SW_D ** -0.5 * LOG2E)
    kk = jnp.concatenate([kp_ref[...], kc_ref[...]], axis=0)
    vv = jnp.concatenate([vp_ref[...], vc_ref[...]], axis=0).astype(BF16)
    low_k = lax.broadcasted_iota(jnp.int32, (2 * w, pair_w), 1) < SW_D
    low_q = lax.broadcasted_iota(jnp.int32, (w, pair_w), 1) < SW_D
    key = lax.broadcasted_iota(jnp.int32, (2 * w, w), 0)
    t = key - lax.broadcasted_iota(jnp.int32, (2 * w, w), 1)
    keymin = jnp.where(i > 0, 0, w)
    bias1 = jnp.where(t >= 1, jnp.where(t <= w, jnp.where(key >= keymin, 0.0, NEG), NEG), NEG)
    bias = jnp.concatenate([bias1] * SW_REP, axis=1)
    outs = []
    for g in range(SW_KV):
        pair = g // 2
        k2 = kk[:, pair * pair_w:(pair + 1) * pair_w]
        k2r = pltpu.roll(k2, SW_D, axis=1)
        kdup = (jnp.where(low_k, k2, k2r) if g % 2 == 0 else jnp.where(low_k, k2r, k2)).astype(BF16)
        qs = []
        for r in range(SW_REP):
            h = g * SW_REP + r
            q2 = q[:, (h // 2) * pair_w:(h // 2 + 1) * pair_w]
            qs.append(jnp.where(low_q, q2, 0.0) if h % 2 == 0 else jnp.where(low_q, 0.0, q2))
        qst = jnp.concatenate(qs, axis=0).astype(BF16)
        s = lax.dot_general(kdup, qst, (((1,), (1,)), ((), ())), preferred_element_type=F32) + bias
        sink = jnp.concatenate([jnp.broadcast_to(sinks[:, g * SW_REP + r:g * SW_REP + r + 1], (1, w))
                                for r in range(SW_REP)], axis=1)
        mx = jnp.maximum(jnp.max(s, axis=0, keepdims=True), sink)
        p = jnp.exp2(s - mx)
        den = jnp.sum(p, axis=0, keepdims=True) + jnp.exp2(sink - mx)
        ot = lax.dot_general(vv[:, pair * pair_w:(pair + 1) * pair_w], p.astype(BF16),
                             (((0,), (0,)), ((), ())), preferred_element_type=F32)
        ot = ot[(g % 2) * SW_D:(g % 2 + 1) * SW_D, :] / den
        for pr in range(SW_REP // 2):
            blk = jnp.concatenate([ot[:, (2 * pr) * w:(2 * pr + 1) * w],
                                   ot[:, (2 * pr + 1) * w:(2 * pr + 2) * w]], axis=0)
            outs.append(blk.T)
    o_ref[...] = jnp.concatenate(outs, axis=1)


def _swa_prompt(proj, sinks, nseq, seq):
    m = proj.shape[0]
    nb = seq // WINDOW
    kcol = SW_Q // SW_KVD
    vcol = kcol + 1
    prev = lambda b, i: (b * nb + jnp.maximum(i - 1, 0), kcol)
    prev_v = lambda b, i: (b * nb + jnp.maximum(i - 1, 0), vcol)
    return pl.pallas_call(
        _swa_prompt_kernel,
        grid=(nseq, nb),
        in_specs=[pl.BlockSpec((1, SW_KV * SW_REP), lambda b, i: (0, 0)),
                  pl.BlockSpec((WINDOW, SW_Q), lambda b, i: (b * nb + i, 0)),
                  pl.BlockSpec((WINDOW, SW_KVD), prev),
                  pl.BlockSpec((WINDOW, SW_KVD), lambda b, i: (b * nb + i, kcol)),
                  pl.BlockSpec((WINDOW, SW_KVD), prev_v),
                  pl.BlockSpec((WINDOW, SW_KVD), lambda b, i: (b * nb + i, vcol))],
        out_specs=pl.BlockSpec((WINDOW, SW_Q), lambda b, i: (b * nb + i, 0)),
        out_shape=jax.ShapeDtypeStruct((m, SW_Q), F32),
        compiler_params=_cp(("parallel", "arbitrary")),
        name="swa_prompt",
    )(sinks.reshape(1, -1), proj, proj, proj, proj, proj)


def _swa_decode_kernel(q_ref, kn_ref, vn_ref, sink_ref, kb_ref, vb_ref, o_ref):
    q = q_ref[0] * (SW_D ** -0.5)
    nh = SW_KV * SW_REP
    q4 = jnp.concatenate([q] * SW_KV, axis=-1)
    rowg = lax.broadcasted_iota(jnp.int32, (nh, SW_KVD), 0) // SW_REP
    colg = lax.broadcasted_iota(jnp.int32, (nh, SW_KVD), 1) // SW_D
    own = rowg == colg
    qbd = jnp.where(own, q4, 0.0)
    s = _dot(qbd, kb_ref[0])
    col = lax.broadcasted_iota(jnp.int32, (nh, WINDOW), 1)
    s = jnp.where(col >= 1, s, NEG)
    s_new = jnp.sum(qbd * kn_ref[0], axis=-1, keepdims=True)
    sink = sink_ref[...]
    mx = jnp.maximum(jnp.maximum(jnp.max(s, axis=-1, keepdims=True), s_new), sink)
    p = jnp.exp(s - mx)
    p_new = jnp.exp(s_new - mx)
    den = jnp.sum(p, axis=-1, keepdims=True) + p_new + jnp.exp(sink - mx)
    o = (_dot_nt(p, vb_ref[0]) + p_new * vn_ref[0]) / den
    o = jnp.where(own, o, 0.0)
    out = o[:, 0:SW_D]
    for g in range(1, SW_KV):
        out = out + o[:, g * SW_D:(g + 1) * SW_D]
    o_ref[0] = out


def _swa_decode(q, k_new, v_new, sinks, win_k, win_v):
    nb = q.shape[0]
    nh = SW_KV * SW_REP
    out = pl.pallas_call(
        _swa_decode_kernel,
        grid=(nb,),
        in_specs=[pl.BlockSpec((1, nh, SW_D), lambda b: (b, 0, 0)),
                  pl.BlockSpec((1, 1, SW_KVD), lambda b: (b, 0, 0)),
                  pl.BlockSpec((1, 1, SW_KVD), lambda b: (b, 0, 0)),
                  pl.BlockSpec((nh, 1), lambda b: (0, 0)),
                  pl.BlockSpec((1, SW_KVD, WINDOW), lambda b: (b, 0, 0)),
                  pl.BlockSpec((1, SW_KVD, WINDOW), lambda b: (b, 0, 0))],
        out_specs=pl.BlockSpec((1, nh, SW_D), lambda b: (b, 0, 0)),
        out_shape=jax.ShapeDtypeStruct((nb, nh, SW_D), F32),
        compiler_params=_cp(("parallel",)),
        name="swa_decode",
    )(q.reshape(nb, nh, SW_D), k_new.reshape(nb, 1, SW_KVD), v_new.reshape(nb, 1, SW_KVD),
      sinks.reshape(nh, 1), win_k.transpose(0, 2, 3, 1).reshape(nb, SW_KVD, WINDOW),
      win_v.transpose(0, 2, 3, 1).reshape(nb, SW_KVD, WINDOW))
    return out.reshape(nb, SW_Q)


def _trunk(x, mod, w, grp, *, nseq, seq, caches, bf16_w):
    prompt = caches is None
    tm_ffn, tm_in, tm_out = (min(1024, seq), min(512, seq), min(512, seq)) if prompt else (nseq,) * 3
    tf, tn, tk = 512, 512, 512
    g_norm = w['g_norm']
    ffn_w = (w['w_ffn_gate'], w['w_ffn_up'], w['w_ffn_down'])

    def ffn(x, l, s, sub):
        if prompt:
            return _ffn(x, mod, g_norm, None, l, s, sub, grp, tm=tm_ffn, tf=tf, tiles=bf16_w['ffn', l, s])
        x, bf16_w['ffn', l, s] = _ffn(x, mod, g_norm, ffn_w, l, s, sub, grp, tm=tm_ffn, tf=tf)
        return x

    ks, vs, convs, ssms, wks, wvs = [], [], [], [], [], []
    for l in range(DEPTH):
        j = l // 2
        x = ffn(x, l, 0, 0)
        if l % 2 == 0:
            lam_init = 0.8 - 0.6 * math.exp(-0.3 * l)
            w_in_t = w['w_in_even'].transpose(0, 2, 1)
            if prompt:
                proj, dt_raw, k_rows, v_rows = _inproj_res(x, mod, g_norm, bf16_w['in', l], None, w_in_t, l, j,
                                                           grp, tm=tm_in, n_dt=SSM_H)
            else:
                proj, dt_raw, bf16_w['in', l] = _inproj(x, mod, g_norm, w_in_t, None, l, j, EVEN_MAIN, grp,
                                                        tm=tm_in, tn=tn, transposed=True, n_dt=SSM_H)
                k_rows = proj[:, DA_Q:DA_Q + DA_KVD]
                v_rows = proj[:, DA_Q + DA_KVD:DA_Q + 2 * DA_KVD]
            lp = w['lambda_qk'][j]
            gs = w['g_subln'][j]
            ssm_w = (w['conv_w'][j], w['conv_b'][j], w['dt_bias'][j], w['a_log'][j], w['d_skip'][j],
                     w['g_ssm_norm'][j])
            p3 = proj.reshape(nseq, seq, EVEN_MAIN)
            if prompt:
                oa = _da_prompt(proj, lp, gs, nseq, seq, lam_init)
                ob, ssm_new = _ssd_prompt(proj, dt_raw, *ssm_w, nseq, seq)
                conv_new = p3[:, seq - (CONV_W - 1):, EVEN_MAIN - CONV_DIM:]
            else:
                cache_k, cache_v, page_table, conv_state, ssm_state = caches['even']
                oa = _da_decode(proj[:, :DA_Q], proj[:, DA_Q:DA_Q + DA_KVD],
                                proj[:, DA_Q + DA_KVD:DA_Q + 2 * DA_KVD], lp, gs, cache_k, cache_v,
                                page_table, j, lam_init)
                ob, ssm_new = _ssd_decode(proj, dt_raw, conv_state[j], ssm_state, j, *ssm_w)
                conv_new = jnp.concatenate([conv_state[j][:, 1:], p3[:, :, EVEN_MAIN - CONV_DIM:]], axis=1)
            ks.append(k_rows.reshape(nseq, seq, DA_KV, 2 * DA_D))
            vs.append(v_rows.reshape(nseq, seq, DA_KV, 2 * DA_D))
            convs.append(conv_new)
            ssms.append(ssm_new)
            ys, w_out, b_out = [oa, ob], w['w_out_even'], None
        else:
            n_odd = SW_Q + 2 * SW_KVD
            b_in = w['b_in_odd'][j].reshape(1, -1)
            if prompt:
                proj = _inproj_res(x, mod, g_norm, bf16_w['in', l], b_in, None, l, j, grp, tm=tm_in)
            else:
                proj, bf16_w['in', l] = _inproj(x, mod, g_norm, w['w_in_odd'], b_in, l, j, n_odd, grp,
                                                tm=tm_in, tn=tn, transposed=False)
            p3 = proj.reshape(nseq, seq, n_odd)
            if prompt:
                y = _swa_prompt(proj, w['attn_sinks'][j], nseq, seq)
                tail = p3[:, seq - WINDOW:]
                wk_new = tail[:, :, SW_Q:SW_Q + SW_KVD].reshape(nseq, WINDOW, SW_KV, SW_D)
                wv_new = tail[:, :, SW_Q + SW_KVD:].reshape(nseq, WINDOW, SW_KV, SW_D)
            else:
                win_k, win_v = caches['odd']
                k_rows = proj[:, SW_Q:SW_Q + SW_KVD]
                v_rows = proj[:, SW_Q + SW_KVD:]
                y = _swa_decode(proj[:, :SW_Q], k_rows, v_rows, w['attn_sinks'][j], win_k[j], win_v[j])
                wk_new = jnp.concatenate([win_k[j][:, 1:], k_rows.reshape(nseq, 1, SW_KV, SW_D)], axis=1)
                wv_new = jnp.concatenate([win_v[j][:, 1:], v_rows.reshape(nseq, 1, SW_KV, SW_D)], axis=1)
            wks.append(wk_new)
            wvs.append(wv_new)
            ys, w_out, b_out = [y], w['w_out_odd'], w['b_out_odd'][j].reshape(1, -1)
        if prompt:
            x = _outproj_res(x, ys, mod, g_norm, bf16_w['out', l], b_out, l, grp, tm=tm_out)
        else:
            x, bf16_w['out', l] = _outproj(x, ys, mod, g_norm, w_out, b_out, l, j, grp, tm=tm_out, tk=tk)
        x = ffn(x, l, 1, 2)
    return (x, jnp.stack(ks), jnp.stack(vs), jnp.stack(convs), jnp.stack(ssms), jnp.stack(wks), jnp.stack(wvs))


def kernel(x_prompt, x_sample, cache_k, cache_v, state_conv, state_ssm, cache_win_k, cache_win_v, page_table,
           c_prompt, c_sample, w_mod, b_mod, g_norm, w_ffn_gate, w_ffn_up, w_ffn_down, w_in_even, lambda_qk,
           g_subln, conv_w, conv_b, dt_bias, a_log, d_skip, g_ssm_norm, w_out_even, w_in_odd, b_in_odd,
           attn_sinks, w_out_odd, b_out_odd):
    w = dict(g_norm=g_norm, w_ffn_gate=w_ffn_gate, w_ffn_up=w_ffn_up, w_ffn_down=w_ffn_down,
             w_in_even=w_in_even, lambda_qk=lambda_qk, g_subln=g_subln, conv_w=conv_w, conv_b=conv_b,
             dt_bias=dt_bias, a_log=a_log, d_skip=d_skip, g_ssm_norm=g_ssm_norm, w_out_even=w_out_even,
             w_in_odd=w_in_odd, b_in_odd=b_in_odd, attn_sinks=attn_sinks, w_out_odd=w_out_odd,
             b_out_odd=b_out_odd)
    bp, seq, d = x_prompt.shape
    bs, dec_seq, _ = x_sample.shape
    assert dec_seq == 1
    pad = (-(bs + bp)) % 16
    c_all = jnp.concatenate([c_sample, c_prompt, jnp.zeros((pad, d), F32)], axis=0)
    mod = _modulation(c_all, w_mod, b_mod)
    grp_s = _Group(0, bs, None)
    grp_p = _Group(bs, bp, seq)

    bf16_w = {}
    caches = dict(even=(cache_k, cache_v, page_table, state_conv, state_ssm), odd=(cache_win_k, cache_win_v))
    ys, ksm, vsm, cvs, sss, wks, wvs = _trunk(x_sample.reshape(bs, d), mod, w, grp_s, nseq=bs, seq=1,
                                              caches=caches, bf16_w=bf16_w)
    yp, kp, vp, cvp, ssp, wkp, wvp = _trunk(x_prompt.reshape(bp * seq, d), mod, w, grp_p, nseq=bp, seq=seq,
                                            caches=None, bf16_w=bf16_w)
    return (yp.reshape(bp, seq, d), ys.reshape(bs, 1, d), kp, vp, ksm, vsm, cvp, cvs, ssp, sss,
            wkp, wvp, wks, wvs)
```

```python
import functools
import math

import jax
import jax.numpy as jnp
from jax import lax
from jax.experimental import pallas as pl
from jax.experimental.pallas import tpu as pltpu

F32 = jnp.float32
BF16 = jnp.bfloat16

DEPTH = 4
N_SUB = 3
FFN_RES = 0.5
EPS = 1e-6
NEG = -1e30
LOG2E = math.log2(math.e)
PAGE = 128
DA_KV = 4
DA_REP = 2
DA_D = 64
DA_Q = DA_KV * DA_REP * 2 * DA_D
DA_KVD = DA_KV * 2 * DA_D
SSM_INNER = 1024
SSM_P = 64
SSM_H = 16
SSM_G = 2
SSM_N = 128
CONV_W = 4
CONV_DIM = SSM_INNER + 2 * SSM_G * SSM_N
CHUNK = 128
EVEN_MAIN = DA_Q + 2 * DA_KVD + SSM_INNER + CONV_DIM
SW_D = 64
SW_KV = 4
SW_REP = 8
SW_Q = SW_KV * SW_REP * SW_D
SW_KVD = SW_KV * SW_D
WINDOW = 128

SUBLANES = 8
VMEM_LIMIT = 56 * 1024 * 1024


def _cp(sem, limit=VMEM_LIMIT):
    return pltpu.CompilerParams(dimension_semantics=sem, vmem_limit_bytes=limit)


def _silu(x):
    return x / (1.0 + jnp.exp(-x))


def _softplus(x):
    return jnp.maximum(x, 0.0) + jnp.log(1.0 + jnp.exp(-jnp.abs(x)))


def _rms(x, g):
    return x * lax.rsqrt(jnp.mean(x * x, axis=-1, keepdims=True) + EPS) * g


def _dot(a, b):
    return jnp.dot(a.astype(BF16), b.astype(BF16), preferred_element_type=F32)


def _dot_nt(a, b):
    return lax.dot_general(a.astype(BF16), b.astype(BF16), (((1,), (1,)), ((), ())),
                           preferred_element_type=F32)


def _dot_tn(a, b):
    return lax.dot_general(a.astype(BF16), b.astype(BF16), (((0,), (0,)), ((), ())),
                           preferred_element_type=F32)


def _mod_kernel(c_ref, w_ref, b_ref, o_ref):
    h = _silu(c_ref[...])
    o_ref[0] = _dot(h, w_ref[0]) + b_ref[0]


def _modulation(c_all, w_mod, b_mod, tn=1024):
    rows, d = c_all.shape
    depth, _, n = w_mod.shape
    return pl.pallas_call(
        _mod_kernel,
        grid=(depth, n // tn),
        in_specs=[pl.BlockSpec((rows, d), lambda l, j: (0, 0)),
                  pl.BlockSpec((1, d, tn), lambda l, j: (l, 0, j)),
                  pl.BlockSpec((1, 1, tn), lambda l, j: (l, 0, j))],
        out_specs=pl.BlockSpec((1, rows, tn), lambda l, j: (l, 0, j)),
        out_shape=jax.ShapeDtypeStruct((depth, rows, n), F32),
        compiler_params=_cp(("parallel", "arbitrary")),
        name="modulation",
    )(c_all, w_mod, b_mod.reshape(depth, 1, n))


class _Group:
    def __init__(self, row0, nrows, seq):
        assert row0 % SUBLANES == 0
        self.row0, self.nrows, self.seq = row0, nrows, seq
        self.block_rows = nrows if seq is None else SUBLANES
        assert seq is not None or nrows % SUBLANES == 0
        assert seq is None or nrows <= SUBLANES

    def spec(self, layer, col, d):
        rb = self.row0 // self.block_rows
        return pl.BlockSpec((1, self.block_rows, d), lambda i, j: (layer, rb, col))

    def rows(self, ref, tm):
        if self.seq is None:
            return ref[0]
        return ref[0, pl.ds((pl.program_id(0) * tm) // self.seq, 1), :]


def _gn_spec(layer, d):
    return pl.BlockSpec((1, 2 * N_SUB, d), lambda i, j: (layer, 0, 0))


def _ffn_kernel(*refs, nf, res_w, kpre, grp, tm, emit):
    x_ref, sh_ref, sc_ref, gt_ref, gn_ref, wg_ref, wu_ref, wd_ref, o_ref = refs[:9]
    h_ref = refs[-1]
    f = pl.program_id(1)
    def split(n):
        return tuple((k * tm // n, (k + 1) * tm // n) for k in range(n)) if tm % (256 * n) == 0 else ((0, tm),)

    def mod_rows(ref, lo, hi):
        rows = grp.rows(ref, tm)
        return rows if rows.shape[0] == 1 else rows[lo:hi]

    def step(first, last):
        if emit:
            wg, wu, wd = (r[0, 0].astype(BF16) for r in (wg_ref, wu_ref, wd_ref))
            for out_ref, tile in zip(refs[9:12], (wg, wu, wd)):
                out_ref[0] = tile
        else:
            wg, wu, wd = wg_ref[0], wu_ref[0], wd_ref[0]
        for lo, hi in split(2 if first or last else 1):
            if first:
                xn = _rms(x_ref[lo:hi, :], gn_ref[0, kpre:kpre + 1, :])
                h = (xn * (1 + mod_rows(sc_ref, lo, hi)) + mod_rows(sh_ref, lo, hi)).astype(BF16)
                h_ref[lo:hi, :] = h
            else:
                h = h_ref[lo:hi, :]
            g = jnp.dot(h, wg, preferred_element_type=F32)
            u = jnp.dot(h, wu, preferred_element_type=F32)
            y = jnp.dot((_silu(g) * u).astype(BF16), wd, preferred_element_type=F32)
            if not first:
                y = o_ref[lo:hi, :] + y
            if last:
                yn = _rms(y, gn_ref[0, kpre + 1:kpre + 2, :])
                y = x_ref[lo:hi, :] + (res_w * mod_rows(gt_ref, lo, hi)) * yn
            o_ref[lo:hi, :] = y

    if nf == 1:
        step(True, True)
    else:
        pl.when(f == 0)(lambda: step(True, False))
        pl.when(f == nf - 1)(lambda: step(False, True))
        if nf > 2:
            pl.when(jnp.logical_and(f > 0, f < nf - 1))(lambda: step(False, False))


def _ffn(x, mod, g_norm, weights, l, s, sub, grp, *, tm, tf, tiles=None):
    m, d = x.shape
    emit = tiles is None
    if emit:
        nf = weights[0].shape[-1] // tf
        w_specs = [pl.BlockSpec((1, 1, d, tf), lambda i, j: (l, s, 0, j)),
                   pl.BlockSpec((1, 1, d, tf), lambda i, j: (l, s, 0, j)),
                   pl.BlockSpec((1, 1, tf, d), lambda i, j: (l, s, j, 0))]
        w_args = weights
        x_kw = {}
    else:
        nf = tiles[0].shape[0]
        w_specs = [pl.BlockSpec((1, d, tf), lambda i, j: (j, 0, 0)),
                   pl.BlockSpec((1, d, tf), lambda i, j: (j, 0, 0)),
                   pl.BlockSpec((1, tf, d), lambda i, j: (j, 0, 0))]
        w_args = tiles
        x_kw = dict(pipeline_mode=pl.Buffered(1))
    out_specs = [pl.BlockSpec((tm, d), lambda i, j: (i, 0))]
    out_shape = [jax.ShapeDtypeStruct((m, d), F32)]
    if emit:
        assert m == tm
        out_specs += [pl.BlockSpec((1, d, tf), lambda i, j: (j, 0, 0)),
                      pl.BlockSpec((1, d, tf), lambda i, j: (j, 0, 0)),
                      pl.BlockSpec((1, tf, d), lambda i, j: (j, 0, 0))]
        out_shape += [jax.ShapeDtypeStruct((nf, d, tf), BF16), jax.ShapeDtypeStruct((nf, d, tf), BF16),
                      jax.ShapeDtypeStruct((nf, tf, d), BF16)]
    kern = functools.partial(_ffn_kernel, nf=nf, res_w=FFN_RES, kpre=2 * sub, grp=grp, tm=tm, emit=emit)
    out = pl.pallas_call(
        kern,
        grid=(m // tm, nf),
        in_specs=[pl.BlockSpec((tm, d), lambda i, j: (i, 0), **x_kw),
                  grp.spec(l, 3 * sub + 0, d), grp.spec(l, 3 * sub + 1, d), grp.spec(l, 3 * sub + 2, d),
                  _gn_spec(l, d)] + w_specs,
        out_specs=out_specs,
        out_shape=out_shape,
        scratch_shapes=[pltpu.VMEM((tm, d), BF16)],
        compiler_params=_cp(("parallel", "arbitrary")),
        name="ffn_emit" if emit else "ffn",
    )(x, mod, mod, mod, g_norm, *w_args)
    return (out[0], tuple(out[1:])) if emit else out[0]


def _inproj_kernel(*refs, transposed, has_bias, has_dt, grp, tm):
    refs = list(refs)
    x_ref, sh_ref, sc_ref, gn_ref, w_ref = refs[:5]
    rest = refs[5:]
    b_ref = rest.pop(0) if has_bias else None
    wdt_ref = rest.pop(0) if has_dt else None
    o_ref = rest.pop(0)
    dt_ref = rest.pop(0) if has_dt else None
    wbo_ref, h_ref, wb_ref = rest
    i = pl.program_id(0)
    j = pl.program_id(1)

    @pl.when(j == 0)
    def _():
        xn = _rms(x_ref[...], gn_ref[0, 2:3, :])
        h = (xn * (1 + grp.rows(sc_ref, tm)) + grp.rows(sh_ref, tm)).astype(BF16)
        h_ref[...] = h
        if has_dt:
            dt_ref[...] = _dot_nt(h, wdt_ref[0])

    @pl.when(i == 0)
    def _():
        wf = w_ref[0]
        wb_ref[j] = wf.astype(BF16)
        wbo_ref[...] = (wf.T if transposed else wf).astype(BF16)

    w = wb_ref[j]
    if transposed:
        o = lax.dot_general(h_ref[...], w, (((1,), (1,)), ((), ())), preferred_element_type=F32)
    else:
        o = jnp.dot(h_ref[...], w, preferred_element_type=F32)
    o_ref[...] = o + b_ref[...] if has_bias else o


def _inproj(x, mod, g_norm, w, b, l, layer, n, grp, *, tm, tn, transposed, n_dt=0):
    m, d = x.shape
    assert m == tm
    nj = n // tn
    has_bias = b is not None
    has_dt = n_dt > 0
    wcol = lambda i, j: jnp.where(i == 0, j, nj - 1)
    if transposed:
        w_spec = pl.BlockSpec((1, tn, d), lambda i, j: (layer, wcol(i, j), 0))
        wb_shape = (nj, tn, d)
    else:
        w_spec = pl.BlockSpec((1, d, tn), lambda i, j: (layer, 0, wcol(i, j)))
        wb_shape = (nj, d, tn)
    in_specs = [pl.BlockSpec((tm, d), lambda i, j: (i, 0)),
                grp.spec(l, 3, d), grp.spec(l, 4, d), _gn_spec(l, d), w_spec]
    args = [x, mod, mod, g_norm, w]
    if has_bias:
        in_specs.append(pl.BlockSpec((1, tn), lambda i, j: (0, j)))
        args.append(b)
    out_specs = [pl.BlockSpec((tm, tn), lambda i, j: (i, j))]
    out_shape = [jax.ShapeDtypeStruct((m, n), F32)]
    if has_dt:
        assert transposed and n % n_dt == 0
        in_specs.append(pl.BlockSpec((1, n_dt, d), lambda i, j: (layer, n // n_dt, 0)))
        args.append(w)
        out_specs.append(pl.BlockSpec((tm, n_dt), lambda i, j: (i, 0)))
        out_shape.append(jax.ShapeDtypeStruct((m, n_dt), F32))
    out_specs.append(pl.BlockSpec((d, tn), lambda i, j: (0, j)))
    out_shape.append(jax.ShapeDtypeStruct((d, n), BF16))
    return pl.pallas_call(
        functools.partial(_inproj_kernel, transposed=transposed, has_bias=has_bias, has_dt=has_dt,
                          grp=grp, tm=tm),
        grid=(m // tm, nj),
        in_specs=in_specs,
        out_specs=out_specs,
        out_shape=out_shape,
        scratch_shapes=[pltpu.VMEM((tm, d), BF16), pltpu.VMEM(wb_shape, BF16)],
        compiler_params=_cp(("arbitrary", "arbitrary")),
        name="inproj_emit",
    )(*args)


def _inproj_res_kernel(*refs, n, chunk, has_bias, has_dt, grp, tm):
    refs = list(refs)
    x_ref, sh_ref, sc_ref, gn_ref, wb_ref = refs[:5]
    rest = refs[5:]
    b_ref = rest.pop(0) if has_bias else None
    wdt_ref = rest.pop(0) if has_dt else None
    o_ref = rest.pop(0)
    scale, shift = grp.rows(sc_ref, tm), grp.rows(sh_ref, tm)
    assert scale.shape[0] == 1
    half = tm // 2
    for lo in (0, half):
        xn = _rms(x_ref[lo:lo + half, :], gn_ref[0, 2:3, :])
        h = (xn * (1 + scale) + shift).astype(BF16)
        if has_dt:
            rest[0][lo:lo + half, :] = _dot_nt(h, wdt_ref[0])
        for c in range(n // chunk):
            cols = slice(c * chunk, (c + 1) * chunk)
            o = jnp.dot(h, wb_ref[:, cols], preferred_element_type=F32)
            o_ref[lo:lo + half, cols] = o + b_ref[:, cols] if has_bias else o
            if has_dt and c * chunk in (DA_Q, DA_Q + DA_KVD):
                kv_ref = rest[1] if c * chunk == DA_Q else rest[2]
                for g in range(DA_KV):
                    kv_ref[pl.ds(lo * DA_KV + g, half, stride=DA_KV), :] = o[:, g * 2 * DA_D:(g + 1) * 2 * DA_D]


def _inproj_res(x, mod, g_norm, wb, b, w_dt, l, layer, grp, *, tm, n_dt=0, chunk=512):
    assert chunk == DA_KVD
    m, d = x.shape
    n = wb.shape[1]
    has_bias = b is not None
    has_dt = n_dt > 0
    once = dict(pipeline_mode=pl.Buffered(1))
    in_specs = [pl.BlockSpec((tm, d), lambda i, j: (i, 0)),
                grp.spec(l, 3, d), grp.spec(l, 4, d), _gn_spec(l, d),
                pl.BlockSpec((d, n), lambda i, j: (0, 0), **once)]
    args = [x, mod, mod, g_norm, wb]
    if has_bias:
        in_specs.append(pl.BlockSpec((1, n), lambda i, j: (0, 0)))
        args.append(b)
    out_specs = [pl.BlockSpec((tm, n), lambda i, j: (i, 0))]
    out_shape = [jax.ShapeDtypeStruct((m, n), F32)]
    if has_dt:
        in_specs.append(pl.BlockSpec((1, n_dt, d), lambda i, j: (layer, n // n_dt, 0)))
        args.append(w_dt)
        out_specs.append(pl.BlockSpec((tm, n_dt), lambda i, j: (i, 0)))
        out_shape.append(jax.ShapeDtypeStruct((m, n_dt), F32))
        for _ in range(2):
            out_specs.append(pl.BlockSpec((tm * DA_KV, 2 * DA_D), lambda i, j: (i, 0)))
            out_shape.append(jax.ShapeDtypeStruct((m * DA_KV, 2 * DA_D), F32))
    out = pl.pallas_call(
        functools.partial(_inproj_res_kernel, n=n, chunk=chunk, has_bias=has_bias, has_dt=has_dt,
                          grp=grp, tm=tm),
        grid=(m // tm, 1),
        in_specs=in_specs,
        out_specs=out_specs,
        out_shape=out_shape,
        compiler_params=_cp(("parallel", "arbitrary")),
        name="inproj",
    )(*args)
    return out if has_dt else out[0]


def _outproj_kernel(*refs, bounds, has_bias, grp, tm):
    refs = list(refs)
    x_ref = refs.pop(0)
    y_refs = [refs.pop(0) for _ in bounds]
    gt_ref, gn_ref, w_ref = refs[:3]
    rest = refs[3:]
    b_ref = rest.pop(0) if has_bias else None
    o_ref, wbo_ref, wb_ref = rest
    nk = bounds[-1][1]
    i = pl.program_id(0)
    k = pl.program_id(1)

    @pl.when(i == 0)
    def _():
        wb = w_ref[0].astype(BF16)
        wb_ref[k] = wb
        wbo_ref[...] = wb

    @pl.when(k == 0)
    def _():
        o_ref[...] = jnp.zeros(o_ref.shape, F32)

    for y_ref, (lo, hi) in zip(y_refs, bounds):
        @pl.when(jnp.logical_and(k >= lo, k < hi))
        def _(y_ref=y_ref):
            o_ref[...] += jnp.dot(y_ref[...].astype(BF16), wb_ref[k], preferred_element_type=F32)

    @pl.when(k == nk - 1)
    def _():
        y = o_ref[...] + b_ref[...] if has_bias else o_ref[...]
        o_ref[...] = x_ref[...] + grp.rows(gt_ref, tm) * _rms(y, gn_ref[0, 3:4, :])


def _outproj(x, ys, mod, g_norm, w, b, l, layer, grp, *, tm, tk):
    m, d = x.shape
    has_bias = b is not None
    bounds, lo = [], 0
    for y in ys:
        bounds.append((lo, lo + y.shape[1] // tk))
        lo = bounds[-1][1]
    nk = lo
    assert nk * tk == w.shape[1]

    def y_spec(lo, hi):
        return pl.BlockSpec((tm, tk), lambda i, k: (i, jnp.clip(k - lo, 0, hi - lo - 1)))

    in_specs = ([pl.BlockSpec((tm, d), lambda i, k: (i, 0))]
                + [y_spec(lo, hi) for lo, hi in bounds]
                + [grp.spec(l, 5, d), _gn_spec(l, d),
                   pl.BlockSpec((1, tk, d), lambda i, k: (layer, jnp.where(i == 0, k, nk - 1), 0))])
    args = [x, *ys, mod, g_norm, w]
    if has_bias:
        in_specs.append(pl.BlockSpec((1, d), lambda i, k: (0, 0)))
        args.append(b)
    assert m == tm
    return pl.pallas_call(
        functools.partial(_outproj_kernel, bounds=tuple(bounds), has_bias=has_bias, grp=grp, tm=tm),
        grid=(m // tm, nk),
        in_specs=in_specs,
        out_specs=[pl.BlockSpec((tm, d), lambda i, k: (i, 0)), pl.BlockSpec((tk, d), lambda i, k: (k, 0))],
        out_shape=[jax.ShapeDtypeStruct((m, d), F32), jax.ShapeDtypeStruct((nk * tk, d), BF16)],
        scratch_shapes=[pltpu.VMEM((nk, tk, d), BF16)],
        compiler_params=_cp(("arbitrary", "arbitrary")),
        name="outproj_emit",
    )(*args)


def _outproj_res_kernel(*refs, n_y, has_bias, grp, tm):
    x_ref = refs[0]
    y_refs = refs[1:1 + n_y]
    gt_ref, gn_ref, wb_ref = refs[1 + n_y:4 + n_y]
    b_ref = refs[4 + n_y] if has_bias else None
    o_ref = refs[-1]
    gate = grp.rows(gt_ref, tm)
    assert gate.shape[0] == 1
    half = tm // 2
    for lo in (0, half):
        rows = slice(lo, lo + half)
        ycat = jnp.concatenate([r[rows, :].astype(BF16) for r in y_refs], axis=-1)
        y = jnp.dot(ycat, wb_ref[...], preferred_element_type=F32)
        if has_bias:
            y = y + b_ref[...]
        o_ref[rows, :] = x_ref[rows, :] + gate * _rms(y, gn_ref[0, 3:4, :])


def _outproj_res(x, ys, mod, g_norm, wb, b, l, grp, *, tm):
    m, d = x.shape
    has_bias = b is not None
    in_specs = ([pl.BlockSpec((tm, d), lambda i, k: (i, 0))]
                + [pl.BlockSpec((tm, y.shape[1]), lambda i, k: (i, 0)) for y in ys]
                + [grp.spec(l, 5, d), _gn_spec(l, d),
                   pl.BlockSpec(wb.shape, lambda i, k: (0, 0), pipeline_mode=pl.Buffered(1))])
    args = [x, *ys, mod, g_norm, wb]
    if has_bias:
        in_specs.append(pl.BlockSpec((1, d), lambda i, k: (0, 0)))
        args.append(b)
    return pl.pallas_call(
        functools.partial(_outproj_res_kernel, n_y=len(ys), has_bias=has_bias, grp=grp, tm=tm),
        grid=(m // tm, 1),
        in_specs=in_specs,
        out_specs=pl.BlockSpec((tm, d), lambda i, k: (i, 0)),
        out_shape=jax.ShapeDtypeStruct((m, d), F32),
        compiler_params=_cp(("parallel", "arbitrary")),
        name="outproj",
    )(*args)


def _lambda(lp, lam_init):
    a = jnp.sum(lp[0:1] * lp[1:2], axis=-1, keepdims=True)
    b = jnp.sum(lp[2:3] * lp[3:4], axis=-1, keepdims=True)
    return jnp.exp(a) - jnp.exp(b) + lam_init


def _da_prompt_kernel(q_ref, k_ref, v_ref, lp_ref, gs_ref, o_ref, *, tq, lam_init):
    i = pl.program_id(2)
    lam = _lambda(lp_ref[...], lam_init)
    q = q_ref[...] * (DA_D ** -0.5 * LOG2E)
    low = lax.broadcasted_iota(jnp.int32, (tq, 2 * DA_D), 1) < DA_D
    heads = (q[:, :2 * DA_D], q[:, 2 * DA_D:])
    qp = (jnp.concatenate([jnp.where(low, h, 0.0) for h in heads], axis=0).astype(BF16),
          jnp.concatenate([jnp.where(low, 0.0, h) for h in heads], axis=0).astype(BF16))
    key = lax.broadcasted_iota(jnp.int32, (tq, 2 * tq), 0)
    col = lax.broadcasted_iota(jnp.int32, (tq, 2 * tq), 1)
    causal_bias = jnp.where(key <= jnp.where(col >= tq, col - tq, col), 0.0, NEG)

    def step(j, carry, diagonal):
        off = pl.multiple_of(j * tq, tq)
        kb = k_ref[pl.ds(off, tq), :].astype(BF16)
        vb = v_ref[pl.ds(off, tq), :].astype(BF16)
        out = []
        for mi in range(2):
            m_old, l_old, acc = carry[3 * mi:3 * mi + 3]
            s = lax.dot_general(kb, qp[mi], (((1,), (1,)), ((), ())), preferred_element_type=F32)
            if diagonal:
                s = s + causal_bias
            m_new = jnp.maximum(m_old, jnp.max(s, axis=0, keepdims=True))
            alpha = jnp.exp2(m_old - m_new)
            p = jnp.exp2(s - m_new)
            l_new = alpha * l_old + jnp.sum(p, axis=0, keepdims=True)
            acc = alpha * acc + lax.dot_general(vb, p.astype(BF16), (((0,), (0,)), ((), ())),
                                                preferred_element_type=F32)
            out += [m_new, l_new, acc]
        return tuple(out)

    m0 = jnp.full((1, 2 * tq), NEG, F32)
    l0 = jnp.zeros((1, 2 * tq), F32)
    a0 = jnp.zeros((2 * DA_D, 2 * tq), F32)
    res = lax.fori_loop(0, i, lambda j, c: step(j, c, False), (m0, l0, a0, m0, l0, a0))
    res = step(i, res, True)
    a = res[2] / res[1] - lam * (res[5] / res[4])
    a = a * lax.rsqrt(jnp.mean(a * a, axis=0, keepdims=True) + EPS) * gs_ref[...] * (1 - lam_init)
    a = a.T
    o_ref[:, 0:128] = a[:tq]
    o_ref[:, 128:256] = a[tq:]


def _da_prompt(proj, lp, gs, nseq, seq, lam_init, tq=512):
    m = proj.shape[0]
    tq = min(tq, seq)
    nq = seq // tq
    kcol = DA_Q // 128
    vcol = (DA_Q + DA_KVD) // 128
    return pl.pallas_call(
        functools.partial(_da_prompt_kernel, tq=tq, lam_init=lam_init),
        grid=(nseq, DA_KV, nq),
        in_specs=[pl.BlockSpec((tq, 256), lambda b, g, i: (b * nq + i, g)),
                  pl.BlockSpec((seq, 128), lambda b, g, i: (b, kcol + g)),
                  pl.BlockSpec((seq, 128), lambda b, g, i: (b, vcol + g)),
                  pl.BlockSpec((4, DA_D), lambda b, g, i: (0, 0)),
                  pl.BlockSpec((2 * DA_D, 1), lambda b, g, i: (0, 0))],
        out_specs=pl.BlockSpec((tq, 256), lambda b, g, i: (b * nq + i, g)),
        out_shape=jax.ShapeDtypeStruct((m, DA_Q), F32),
        compiler_params=_cp(("parallel", "parallel", "arbitrary")),
        name="diff_attn_prompt",
    )(proj, proj, proj, lp, gs.reshape(-1, 1))


def _da_decode_kernel(pt_ref, qm_ref, kn_ref, vn_ref, lp_ref, gs_ref, *refs, pc, nc, lam_init):
    del pt_ref
    k_refs = refs[:pc]
    v_refs = refs[pc:2 * pc]
    o_ref, m_ref, l_ref, acc_ref, kb_ref, vb_ref = refs[2 * pc:]
    c = pl.program_id(1)

    @pl.when(c == 0)
    def _():
        m_ref[...] = jnp.full(m_ref.shape, NEG, F32)
        l_ref[...] = jnp.zeros(l_ref.shape, F32)
        acc_ref[...] = jnp.zeros(acc_ref.shape, F32)

    rows = PAGE * DA_KV
    for t in range(pc):
        kb_ref[t * rows:(t + 1) * rows, :] = k_refs[t][0, 0].astype(BF16)
        vb_ref[t * rows:(t + 1) * rows, :] = v_refs[t][0, 0].astype(BF16)
    qm = qm_ref[0] * (DA_D ** -0.5)
    s = _dot_nt(qm, kb_ref[...])
    row_g = (lax.broadcasted_iota(jnp.int32, s.shape, 0) // DA_REP) & (DA_KV - 1)
    col_g = lax.broadcasted_iota(jnp.int32, s.shape, 1) & (DA_KV - 1)
    s = jnp.where(row_g == col_g, s, NEG)
    m_old = m_ref[...]
    m_new = jnp.maximum(m_old, jnp.max(s, axis=-1, keepdims=True))
    alpha = jnp.exp(m_old - m_new)
    p = jnp.exp(s - m_new)
    l_ref[...] = alpha * l_ref[...] + jnp.sum(p, axis=-1, keepdims=True)
    acc_ref[...] = alpha * acc_ref[...] + jnp.dot(p.astype(BF16), vb_ref[...], preferred_element_type=F32)
    m_ref[...] = m_new

    @pl.when(c == nc - 1)
    def _():
        lam = _lambda(lp_ref[...], lam_init)
        s_new = jnp.sum(qm * kn_ref[0], axis=-1, keepdims=True)
        m_old = m_ref[...]
        m_fin = jnp.maximum(m_old, s_new)
        alpha = jnp.exp(m_old - m_fin)
        p_new = jnp.exp(s_new - m_fin)
        l_fin = alpha * l_ref[...] + p_new
        acc = alpha * acc_ref[...] + p_new * vn_ref[0]
        o = acc / l_fin
        a = o[0:8] - lam * o[8:16]
        o_ref[0] = _rms(a, gs_ref[...]) * (1 - lam_init)


def _da_decode(q, k_new, v_new, lp, gs, cache_k, cache_v, page_table, j, lam_init, pc=32):
    nb = q.shape[0]
    n_pages = page_table.shape[1]
    pc = min(pc, n_pages)
    nc = n_pages // pc
    rows = PAGE * DA_KV
    ck = cache_k.reshape(cache_k.shape[0], cache_k.shape[1], rows, 2 * DA_D)
    cv = cache_v.reshape(cache_v.shape[0], cache_v.shape[1], rows, 2 * DA_D)
    qt = q.reshape(nb, DA_KV, DA_REP, 2, DA_D).transpose(0, 3, 1, 2, 4)
    own_m = jnp.eye(2, dtype=bool)[None, :, None, None, :, None]
    qm = jnp.where(own_m, qt[:, :, :, :, None, :], 0.0).reshape(nb, 16, 2 * DA_D)
    rep = lambda a: jnp.broadcast_to(a.reshape(nb, 1, DA_KV, 1, 2 * DA_D),
                                     (nb, 2, DA_KV, DA_REP, 2 * DA_D)).reshape(nb, 16, 2 * DA_D)

    def page_spec(t):
        return pl.BlockSpec((1, 1, rows, 2 * DA_D), lambda b, c, pt: (j, pt[b, c * pc + t], 0, 0))

    row_spec = pl.BlockSpec((1, 16, 2 * DA_D), lambda b, c, pt: (b, 0, 0))
    grid_spec = pltpu.PrefetchScalarGridSpec(
        num_scalar_prefetch=1,
        grid=(nb, nc),
        in_specs=[row_spec, row_spec, row_spec,
                  pl.BlockSpec((4, DA_D), lambda b, c, pt: (0, 0)),
                  pl.BlockSpec((1, 2 * DA_D), lambda b, c, pt: (0, 0))]
                 + [page_spec(t) for t in range(pc)] + [page_spec(t) for t in range(pc)],
        out_specs=pl.BlockSpec((1, 8, 128), lambda b, c, pt: (b, 0, 0)),
        scratch_shapes=[pltpu.VMEM((16, 1), F32), pltpu.VMEM((16, 1), F32),
                        pltpu.VMEM((16, 2 * DA_D), F32),
                        pltpu.VMEM((pc * rows, 2 * DA_D), BF16), pltpu.VMEM((pc * rows, 2 * DA_D), BF16)])
    out = pl.pallas_call(
        functools.partial(_da_decode_kernel, pc=pc, nc=nc, lam_init=lam_init),
        grid_spec=grid_spec,
        out_shape=jax.ShapeDtypeStruct((nb, 8, 128), F32),
        compiler_params=_cp(("parallel", "arbitrary")),
        name="diff_attn_decode",
    )(page_table, qm, rep(k_new), rep(v_new), lp, gs.reshape(1, -1), *([ck] * pc), *([cv] * pc))
    return out.reshape(nb, DA_Q)


def _split3(x):
    x1 = x.astype(BF16)
    r1 = x - x1.astype(F32)
    x2 = r1.astype(BF16)
    x3 = (r1 - x2.astype(F32)).astype(BF16)
    return x1, x2, x3


def _group_norm_gate(y, z, gn):
    yz = y * _silu(z)
    w = SSM_INNER // SSM_G
    parts = []
    for g in range(SSM_G):
        seg = yz[:, g * w:(g + 1) * w]
        parts.append(seg * lax.rsqrt(jnp.mean(seg * seg, axis=-1, keepdims=True) + EPS))
    return jnp.concatenate(parts, axis=-1) * gn


def _ssd_prompt_kernel(xbc_ref, z_ref, dt_ref, dtt_ref, cw_ref, cb_ref, dtb_ref, dtbt_ref,
                       alog_ref, alogt_ref, dskipt_ref, gn_ref, y_ref, st_ref, ext_ref, h_ref, *, nchunk):
    c = pl.program_id(1)
    q = CHUNK

    @pl.when(c == 0)
    def _():
        ext_ref[0:8, :] = jnp.zeros((8, CONV_DIM), F32)
        h_ref[...] = jnp.zeros(h_ref.shape, F32)

    @pl.when(c > 0)
    def _():
        ext_ref[0:8, :] = ext_ref[q:q + 8, :]

    ext_ref[8:q + 8, :] = xbc_ref[...]
    cw = cw_ref[...]
    acc = ext_ref[5:q + 5, :] * cw[0:1]
    for kk in range(1, CONV_W):
        acc = acc + ext_ref[5 + kk:q + 5 + kk, :] * cw[kk:kk + 1]
    xa = _silu(acc + cb_ref[...])
    xs = xa[:, :SSM_INNER]
    bm = xa[:, SSM_INNER:SSM_INNER + SSM_G * SSM_N].astype(BF16)
    cm = xa[:, SSM_INNER + SSM_G * SSM_N:].astype(BF16)

    dt = _softplus(dt_ref[...] + dtb_ref[...])
    dtt = _softplus(dtt_ref[...] + dtbt_ref[...])
    a = dt * (-jnp.exp(alog_ref[...]))
    at = dtt * (-jnp.exp(alogt_ref[...]))
    ri = lax.broadcasted_iota(jnp.int32, (q, q), 0)
    ci = lax.broadcasted_iota(jnp.int32, (q, q), 1)
    lower = ri >= ci
    tril = jnp.where(lower, 1.0, 0.0).astype(BF16)
    triu = jnp.where(ri <= ci, 1.0, 0.0).astype(BF16)
    cs = sum(jnp.dot(tril, part, preferred_element_type=F32) for part in _split3(a))
    cst = sum(jnp.dot(part, triu, preferred_element_type=F32) for part in _split3(at))
    tot = cst[:, q - 1:q]
    e_cst = jnp.exp(cst)
    e_dect = jnp.exp(tot - cst)
    e_tot = jnp.exp(tot)
    dskip = dskipt_ref[...]
    xst = xs.T
    upper = ri <= ci

    yts = []
    for g in range(SSM_G):
        bg = bm[:, g * SSM_N:(g + 1) * SSM_N]
        cg = cm[:, g * SSM_N:(g + 1) * SSM_N]
        cbt = lax.dot_general(bg, cg, (((1,), (1,)), ((), ())), preferred_element_type=F32)
        for r in range(SSM_H // SSM_G):
            h = g * (SSM_H // SSM_G) + r
            lmat_t = jnp.exp(jnp.where(upper, cst[h:h + 1, :] - cs[:, h:h + 1], NEG))
            xt_h = xst[h * SSM_P:(h + 1) * SSM_P, :]
            xdt_t = xt_h * dtt[h:h + 1, :]
            y_diag = _dot(xdt_t, cbt * lmat_t)
            hp = h_ref[h]
            y_off = lax.dot_general(hp.astype(BF16), cg, (((1,), (1,)), ((), ())),
                                    preferred_element_type=F32) * e_cst[h:h + 1, :]
            upd = _dot(xdt_t * e_dect[h:h + 1, :], bg)
            h_ref[h] = e_tot[h:h + 1, :] * hp + upd
            yts.append(y_diag + y_off + dskip[h:h + 1, :] * xt_h)
    y = jnp.concatenate(yts, axis=0).T
    y_ref[...] = _group_norm_gate(y, z_ref[...], gn_ref[...])

    @pl.when(c == nchunk - 1)
    def _():
        st_ref[0] = h_ref[...]


def _ssd_prompt(proj, dt_raw, cw, cb, dtb, alog, dskip, gn, nseq, seq):
    m = proj.shape[0]
    nchunk = seq // CHUNK
    zcol = (DA_Q + 2 * DA_KVD) // SSM_INNER
    xcol = (DA_Q + 2 * DA_KVD + SSM_INNER) // CONV_DIM
    const = lambda b, c: (0, 0)
    y, st = pl.pallas_call(
        functools.partial(_ssd_prompt_kernel, nchunk=nchunk),
        grid=(nseq, nchunk),
        in_specs=[pl.BlockSpec((CHUNK, CONV_DIM), lambda b, c: (b * nchunk + c, xcol)),
                  pl.BlockSpec((CHUNK, SSM_INNER), lambda b, c: (b * nchunk + c, zcol)),
                  pl.BlockSpec((CHUNK, SSM_H), lambda b, c: (b * nchunk + c, 0)),
                  pl.BlockSpec((SSM_H, CHUNK), lambda b, c: (0, b * nchunk + c)),
                  pl.BlockSpec((CONV_W, CONV_DIM), const),
                  pl.BlockSpec((1, CONV_DIM), const),
                  pl.BlockSpec((1, SSM_H), const),
                  pl.BlockSpec((SSM_H, 1), const),
                  pl.BlockSpec((1, SSM_H), const),
                  pl.BlockSpec((SSM_H, 1), const),
                  pl.BlockSpec((SSM_H, 1), const),
                  pl.BlockSpec((1, SSM_INNER), const)],
        out_specs=[pl.BlockSpec((CHUNK, SSM_INNER), lambda b, c: (b * nchunk + c, 0)),
                   pl.BlockSpec((1, SSM_H, SSM_P, SSM_N), lambda b, c: (b, 0, 0, 0))],
        out_shape=[jax.ShapeDtypeStruct((m, SSM_INNER), F32),
                   jax.ShapeDtypeStruct((nseq, SSM_H, SSM_P, SSM_N), F32)],
        scratch_shapes=[pltpu.VMEM((CHUNK + 8, CONV_DIM), F32),
                        pltpu.VMEM((SSM_H, SSM_P, SSM_N), F32)],
        compiler_params=_cp(("parallel", "arbitrary")),
        name="ssd_prompt",
    )(proj, proj, dt_raw, dt_raw.T, cw, cb.reshape(1, -1), dtb.reshape(1, -1), dtb.reshape(-1, 1),
      alog.reshape(1, -1), alog.reshape(-1, 1), dskip.reshape(-1, 1), gn.reshape(1, -1))
    return y, st


def _ssd_dec_pre_kernel(xbc_ref, cst_ref, dt_ref, cw_ref, cb_ref, dtb_ref, xa_ref, dts_ref):
    cw = cw_ref[...]
    acc = cst_ref[0] * cw[0:1]
    for kk in range(1, CONV_W - 1):
        acc = acc + cst_ref[kk] * cw[kk:kk + 1]
    acc = acc + xbc_ref[...] * cw[CONV_W - 1:CONV_W]
    xa_ref[...] = _silu(acc + cb_ref[...])
    dts_ref[...] = _softplus(dt_ref[...] + dtb_ref[...])


def _ssd_dec_state_kernel(xt_ref, dts_ref, alog_ref, b_ref, c_ref, dskip_ref, h_ref, yt_ref, hn_ref):
    dskip = dskip_ref[...]
    for s in range(xt_ref.shape[0]):
        xt = xt_ref[s]
        dts = dts_ref[s]
        a = dts * (-jnp.exp(alog_ref[...]))
        e_a = jnp.exp(a)
        cols = []
        for h in range(SSM_H):
            g = h // (SSM_H // SSM_G)
            xcol = xt[:, h:h + 1]
            brow = b_ref[s][:, g * SSM_N:(g + 1) * SSM_N]
            crow = c_ref[s][:, g * SSM_N:(g + 1) * SSM_N]
            hn = e_a[:, h:h + 1] * h_ref[0, s, h] + (xcol * dts[:, h:h + 1]) * brow
            hn_ref[s, h] = hn
            cols.append(jnp.sum(hn * crow, axis=-1, keepdims=True) + dskip[:, h:h + 1] * xcol)
        yt_ref[s] = jnp.concatenate(cols, axis=-1)


def _ssd_dec_post_kernel(y_ref, z_ref, gn_ref, o_ref):
    o_ref[...] = _group_norm_gate(y_ref[...], z_ref[...], gn_ref[...])


def _ssd_decode(proj, dt_raw, conv_state, ssm_state, layer, cw, cb, dtb, alog, dskip, gn):
    nb = proj.shape[0]
    zcol = (DA_Q + 2 * DA_KVD) // SSM_INNER
    xcol = (DA_Q + 2 * DA_KVD + SSM_INNER) // CONV_DIM
    full = lambda shape: pl.BlockSpec(shape, lambda i: (0,) * len(shape))
    xa, dts = pl.pallas_call(
        _ssd_dec_pre_kernel,
        grid=(1,),
        in_specs=[pl.BlockSpec((nb, CONV_DIM), lambda i: (0, xcol)),
                  full((CONV_W - 1, nb, CONV_DIM)),
                  full((nb, SSM_H)),
                  full((CONV_W, CONV_DIM)),
                  full((1, CONV_DIM)),
                  full((1, SSM_H))],
        out_specs=[full((nb, CONV_DIM)), full((nb, SSM_H))],
        out_shape=[jax.ShapeDtypeStruct((nb, CONV_DIM), F32), jax.ShapeDtypeStruct((nb, SSM_H), F32)],
        compiler_params=_cp(("arbitrary",)),
        name="ssd_decode_pre",
    )(proj, conv_state.transpose(1, 0, 2), dt_raw, cw, cb.reshape(1, -1), dtb.reshape(1, -1))
    sb = 1
    xt = xa[:, :SSM_INNER].reshape(nb, SSM_H, SSM_P).transpose(0, 2, 1)
    bmat = xa[:, SSM_INNER:SSM_INNER + SSM_G * SSM_N].reshape(nb, 1, SSM_G * SSM_N)
    cmat = xa[:, SSM_INNER + SSM_G * SSM_N:].reshape(nb, 1, SSM_G * SSM_N)
    yt, h_new = pl.pallas_call(
        _ssd_dec_state_kernel,
        grid=(nb // sb,),
        in_specs=[pl.BlockSpec((sb, SSM_P, SSM_H), lambda b: (b, 0, 0)),
                  pl.BlockSpec((sb, 1, SSM_H), lambda b: (b, 0, 0)),
                  pl.BlockSpec((1, SSM_H), lambda b: (0, 0)),
                  pl.BlockSpec((sb, 1, SSM_G * SSM_N), lambda b: (b, 0, 0)),
                  pl.BlockSpec((sb, 1, SSM_G * SSM_N), lambda b: (b, 0, 0)),
                  pl.BlockSpec((1, SSM_H), lambda b: (0, 0)),
                  pl.BlockSpec((1, sb, SSM_H, SSM_P, SSM_N), lambda b: (layer, b, 0, 0, 0))],
        out_specs=[pl.BlockSpec((sb, SSM_P, SSM_H), lambda b: (b, 0, 0)),
                   pl.BlockSpec((sb, SSM_H, SSM_P, SSM_N), lambda b: (b, 0, 0, 0))],
        out_shape=[jax.ShapeDtypeStruct((nb, SSM_P, SSM_H), F32),
                   jax.ShapeDtypeStruct(ssm_state.shape[1:], F32)],
        compiler_params=_cp(("parallel",)),
        name="ssd_decode_state",
    )(xt, dts.reshape(nb, 1, SSM_H), alog.reshape(1, -1), bmat, cmat, dskip.reshape(1, -1), ssm_state)
    y = yt.transpose(0, 2, 1).reshape(nb, SSM_INNER)
    ob = pl.pallas_call(
        _ssd_dec_post_kernel,
        grid=(1,),
        in_specs=[full((nb, SSM_INNER)),
                  pl.BlockSpec((nb, SSM_INNER), lambda i: (0, zcol)),
                  full((1, SSM_INNER))],
        out_specs=full((nb, SSM_INNER)),
        out_shape=jax.ShapeDtypeStruct((nb, SSM_INNER), F32),
        compiler_params=_cp(("arbitrary",)),
        name="ssd_decode_post",
    )(y, proj, gn.reshape(1, -1))
    return ob, h_new


def _swa_prompt_kernel(sink_ref, q_ref, kp_ref, kc_ref, vp_ref, vc_ref, o_ref):
    i = pl.program_id(1)
    w = WINDOW
    pair_w = 2 * SW_D
    sinks = sink_ref[...] * LOG2E
    q = q_ref[...] * (SW_D ** -0.5 * LOG2E)
    kk = jnp.concatenate([kp_ref[...], kc_ref[...]], axis=0)
    vv = jnp.concatenate([vp_ref[...], vc_ref[...]], axis=0).astype(BF16)
    low_k = lax.broadcasted_iota(jnp.int32, (2 * w, pair_w), 1) < SW_D
    low_q = lax.broadcasted_iota(jnp.int32, (w, pair_w), 1) < SW_D
    key = lax.broadcasted_iota(jnp.int32, (2 * w, w), 0)
    t = key - lax.broadcasted_iota(jnp.int32, (2 * w, w), 1)
    keymin = jnp.where(i > 0, 0, w)
    bias1 = jnp.where(t >= 1, jnp.where(t <= w, jnp.where(key >= keymin, 0.0, NEG), NEG), NEG)
    bias = jnp.concatenate([bias1] * SW_REP, axis=1)
    outs = []
    for g in range(SW_KV):
        pair = g // 2
        k2 = kk[:, pair * pair_w:(pair + 1) * pair_w]
        k2r = pltpu.roll(k2, SW_D, axis=1)
        kdup = (jnp.where(low_k, k2, k2r) if g % 2 == 0 else jnp.where(low_k, k2r, k2)).astype(BF16)
        qs = []
        for r in range(SW_REP):
            h = g * SW_REP + r
            q2 = q[:, (h // 2) * pair_w:(h // 2 + 1) * pair_w]
            qs.append(jnp.where(low_q, q2, 0.0) if h % 2 == 0 else jnp.where(low_q, 0.0, q2))
        qst = jnp.concatenate(qs, axis=0).astype(BF16)
        s = lax.dot_general(kdup, qst, (((1,), (1,)), ((), ())), preferred_element_type=F32) + bias
        sink = jnp.concatenate([jnp.broadcast_to(sinks[:, g * SW_REP + r:g * SW_REP + r + 1], (1, w))
                                for r in range(SW_REP)], axis=1)
        mx = jnp.maximum(jnp.max(s, axis=0, keepdims=True), sink)
        p = jnp.exp2(s - mx)
        den = jnp.sum(p, axis=0, keepdims=True) + jnp.exp2(sink - mx)
        ot = lax.dot_general(vv[:, pair * pair_w:(pair + 1) * pair_w], p.astype(BF16),
                             (((0,), (0,)), ((), ())), preferred_element_type=F32)
        ot = ot[(g % 2) * SW_D:(g % 2 + 1) * SW_D, :] / den
        for pr in range(SW_REP // 2):
            blk = jnp.concatenate([ot[:, (2 * pr) * w:(2 * pr + 1) * w],
                                   ot[:, (2 * pr + 1) * w:(2 * pr + 2) * w]], axis=0)
            outs.append(blk.T)
    o_ref[...] = jnp.concatenate(outs, axis=1)


def _swa_prompt(proj, sinks, nseq, seq):
    m = proj.shape[0]
    nb = seq // WINDOW
    kcol = SW_Q // SW_KVD
    vcol = kcol + 1
    prev = lambda b, i: (b * nb + jnp.maximum(i - 1, 0), kcol)
    prev_v = lambda b, i: (b * nb + jnp.maximum(i - 1, 0), vcol)
    return pl.pallas_call(
        _swa_prompt_kernel,
        grid=(nseq, nb),
        in_specs=[pl.BlockSpec((1, SW_KV * SW_REP), lambda b, i: (0, 0)),
                  pl.BlockSpec((WINDOW, SW_Q), lambda b, i: (b * nb + i, 0)),
                  pl.BlockSpec((WINDOW, SW_KVD), prev),
                  pl.BlockSpec((WINDOW, SW_KVD), lambda b, i: (b * nb + i, kcol)),
                  pl.BlockSpec((WINDOW, SW_KVD), prev_v),
                  pl.BlockSpec((WINDOW, SW_KVD), lambda b, i: (b * nb + i, vcol))],
        out_specs=pl.BlockSpec((WINDOW, SW_Q), lambda b, i: (b * nb + i, 0)),
        out_shape=jax.ShapeDtypeStruct((m, SW_Q), F32),
        compiler_params=_cp(("parallel", "arbitrary")),
        name="swa_prompt",
    )(sinks.reshape(1, -1), proj, proj, proj, proj, proj)


def _swa_decode_kernel(q_ref, kn_ref, vn_ref, sink_ref, kb_ref, vb_ref, o_ref):
    nh = SW_KV * SW_REP
    rowg = lax.broadcasted_iota(jnp.int32, (nh, SW_KVD), 0) // SW_REP
    colg = lax.broadcasted_iota(jnp.int32, (nh, SW_KVD), 1) // SW_D
    own = rowg == colg
    col = lax.broadcasted_iota(jnp.int32, (nh, WINDOW), 1)
    sink = sink_ref[...]
    for b in range(q_ref.shape[0]):
        q = q_ref[b] * (SW_D ** -0.5)
        q4 = jnp.concatenate([q] * SW_KV, axis=-1)
        qbd = jnp.where(own, q4, 0.0)
        s = _dot(qbd, kb_ref[b])
        s = jnp.where(col >= 1, s, NEG)
        s_new = jnp.sum(qbd * kn_ref[b], axis=-1, keepdims=True)
        mx = jnp.maximum(jnp.maximum(jnp.max(s, axis=-1, keepdims=True), s_new), sink)
        p = jnp.exp(s - mx)
        p_new = jnp.exp(s_new - mx)
        den = jnp.sum(p, axis=-1, keepdims=True) + p_new + jnp.exp(sink - mx)
        o = (_dot_nt(p, vb_ref[b]) + p_new * vn_ref[b]) / den
        o = jnp.where(own, o, 0.0)
        out = o[:, 0:SW_D]
        for g in range(1, SW_KV):
            out = out + o[:, g * SW_D:(g + 1) * SW_D]
        o_ref[b] = out


def _swa_decode(q, k_new, v_new, sinks, win_k, win_v):
    nb = q.shape[0]
    nh = SW_KV * SW_REP
    sb = 4 if nb % 4 == 0 else 1
    out = pl.pallas_call(
        _swa_decode_kernel,
        grid=(nb // sb,),
        in_specs=[pl.BlockSpec((sb, nh, SW_D), lambda b: (b, 0, 0)),
                  pl.BlockSpec((sb, 1, SW_KVD), lambda b: (b, 0, 0)),
                  pl.BlockSpec((sb, 1, SW_KVD), lambda b: (b, 0, 0)),
                  pl.BlockSpec((nh, 1), lambda b: (0, 0)),
                  pl.BlockSpec((sb, SW_KVD, WINDOW), lambda b: (b, 0, 0)),
                  pl.BlockSpec((sb, SW_KVD, WINDOW), lambda b: (b, 0, 0))],
        out_specs=pl.BlockSpec((sb, nh, SW_D), lambda b: (b, 0, 0)),
        out_shape=jax.ShapeDtypeStruct((nb, nh, SW_D), F32),
        compiler_params=_cp(("parallel",)),
        name="swa_decode",
    )(q.reshape(nb, nh, SW_D), k_new.reshape(nb, 1, SW_KVD), v_new.reshape(nb, 1, SW_KVD),
      sinks.reshape(nh, 1), win_k.transpose(0, 2, 3, 1).reshape(nb, SW_KVD, WINDOW),
      win_v.transpose(0, 2, 3, 1).reshape(nb, SW_KVD, WINDOW))
    return out.reshape(nb, SW_Q)


def _trunk(x, mod, w, grp, *, nseq, seq, caches, bf16_w):
    prompt = caches is None
    tm_ffn, tm_in, tm_out = (min(1024, seq), min(512, seq), min(512, seq)) if prompt else (nseq,) * 3
    tf, tn, tk = 512, 512, 512
    g_norm = w['g_norm']
    ffn_w = (w['w_ffn_gate'], w['w_ffn_up'], w['w_ffn_down'])

    def ffn(x, l, s, sub):
        if prompt:
            return _ffn(x, mod, g_norm, None, l, s, sub, grp, tm=tm_ffn, tf=tf, tiles=bf16_w['ffn', l, s])
        x, bf16_w['ffn', l, s] = _ffn(x, mod, g_norm, ffn_w, l, s, sub, grp, tm=tm_ffn, tf=tf)
        return x

    ks, vs, convs, ssms, wks, wvs = [], [], [], [], [], []
    for l in range(DEPTH):
        j = l // 2
        x = ffn(x, l, 0, 0)
        if l % 2 == 0:
            lam_init = 0.8 - 0.6 * math.exp(-0.3 * l)
            w_in_t = w['w_in_even'].transpose(0, 2, 1)
            if prompt:
                proj, dt_raw, k_rows, v_rows = _inproj_res(x, mod, g_norm, bf16_w['in', l], None, w_in_t, l, j,
                                                           grp, tm=tm_in, n_dt=SSM_H)
            else:
                proj, dt_raw, bf16_w['in', l] = _inproj(x, mod, g_norm, w_in_t, None, l, j, EVEN_MAIN, grp,
                                                        tm=tm_in, tn=tn, transposed=True, n_dt=SSM_H)
                k_rows = proj[:, DA_Q:DA_Q + DA_KVD]
                v_rows = proj[:, DA_Q + DA_KVD:DA_Q + 2 * DA_KVD]
            lp = w['lambda_qk'][j]
            gs = w['g_subln'][j]
            ssm_w = (w['conv_w'][j], w['conv_b'][j], w['dt_bias'][j], w['a_log'][j], w['d_skip'][j],
                     w['g_ssm_norm'][j])
            p3 = proj.reshape(nseq, seq, EVEN_MAIN)
            if prompt:
                oa = _da_prompt(proj, lp, gs, nseq, seq, lam_init)
                ob, ssm_new = _ssd_prompt(proj, dt_raw, *ssm_w, nseq, seq)
                conv_new = p3[:, seq - (CONV_W - 1):, EVEN_MAIN - CONV_DIM:]
            else:
                cache_k, cache_v, page_table, conv_state, ssm_state = caches['even']
                oa = _da_decode(proj[:, :DA_Q], proj[:, DA_Q:DA_Q + DA_KVD],
                                proj[:, DA_Q + DA_KVD:DA_Q + 2 * DA_KVD], lp, gs, cache_k, cache_v,
                                page_table, j, lam_init)
                ob, ssm_new = _ssd_decode(proj, dt_raw, conv_state[j], ssm_state, j, *ssm_w)
                conv_new = jnp.concatenate([conv_state[j][:, 1:], p3[:, :, EVEN_MAIN - CONV_DIM:]], axis=1)
            ks.append(k_rows.reshape(nseq, seq, DA_KV, 2 * DA_D))
            vs.append(v_rows.reshape(nseq, seq, DA_KV, 2 * DA_D))
            convs.append(conv_new)
            ssms.append(ssm_new)
            ys, w_out, b_out = [oa, ob], w['w_out_even'], None
        else:
            n_odd = SW_Q + 2 * SW_KVD
            b_in = w['b_in_odd'][j].reshape(1, -1)
            if prompt:
                proj = _inproj_res(x, mod, g_norm, bf16_w['in', l], b_in, None, l, j, grp, tm=tm_in)
            else:
                proj, bf16_w['in', l] = _inproj(x, mod, g_norm, w['w_in_odd'], b_in, l, j, n_odd, grp,
                                                tm=tm_in, tn=tn, transposed=False)
            p3 = proj.reshape(nseq, seq, n_odd)
            if prompt:
                y = _swa_prompt(proj, w['attn_sinks'][j], nseq, seq)
                tail = p3[:, seq - WINDOW:]
                wk_new = tail[:, :, SW_Q:SW_Q + SW_KVD].reshape(nseq, WINDOW, SW_KV, SW_D)
                wv_new = tail[:, :, SW_Q + SW_KVD:].reshape(nseq, WINDOW, SW_KV, SW_D)
            else:
                win_k, win_v = caches['odd']
                k_rows = proj[:, SW_Q:SW_Q + SW_KVD]
                v_rows = proj[:, SW_Q + SW_KVD:]
                y = _swa_decode(proj[:, :SW_Q], k_rows, v_rows, w['attn_sinks'][j], win_k[j], win_v[j])
                wk_new = jnp.concatenate([win_k[j][:, 1:], k_rows.reshape(nseq, 1, SW_KV, SW_D)], axis=1)
                wv_new = jnp.concatenate([win_v[j][:, 1:], v_rows.reshape(nseq, 1, SW_KV, SW_D)], axis=1)
            wks.append(wk_new)
            wvs.append(wv_new)
            ys, w_out, b_out = [y], w['w_out_odd'], w['b_out_odd'][j].reshape(1, -1)
        if prompt:
            x = _outproj_res(x, ys, mod, g_norm, bf16_w['out', l], b_out, l, grp, tm=tm_out)
        else:
            x, bf16_w['out', l] = _outproj(x, ys, mod, g_norm, w_out, b_out, l, j, grp, tm=tm_out, tk=tk)
        x = ffn(x, l, 1, 2)
    return (x, jnp.stack(ks), jnp.stack(vs), jnp.stack(convs), jnp.stack(ssms), jnp.stack(wks), jnp.stack(wvs))


def kernel(x_prompt, x_sample, cache_k, cache_v, state_conv, state_ssm, cache_win_k, cache_win_v, page_table,
           c_prompt, c_sample, w_mod, b_mod, g_norm, w_ffn_gate, w_ffn_up, w_ffn_down, w_in_even, lambda_qk,
           g_subln, conv_w, conv_b, dt_bias, a_log, d_skip, g_ssm_norm, w_out_even, w_in_odd, b_in_odd,
           attn_sinks, w_out_odd, b_out_odd):
    w = dict(g_norm=g_norm, w_ffn_gate=w_ffn_gate, w_ffn_up=w_ffn_up, w_ffn_down=w_ffn_down,
             w_in_even=w_in_even, lambda_qk=lambda_qk, g_subln=g_subln, conv_w=conv_w, conv_b=conv_b,
             dt_bias=dt_bias, a_log=a_log, d_skip=d_skip, g_ssm_norm=g_ssm_norm, w_out_even=w_out_even,
             w_in_odd=w_in_odd, b_in_odd=b_in_odd, attn_sinks=attn_sinks, w_out_odd=w_out_odd,
             b_out_odd=b_out_odd)
    bp, seq, d = x_prompt.shape
    bs, dec_seq, _ = x_sample.shape
    assert dec_seq == 1
    pad = (-(bs + bp)) % 16
    c_all = jnp.concatenate([c_sample, c_prompt, jnp.zeros((pad, d), F32)], axis=0)
    mod = _modulation(c_all, w_mod, b_mod)
    grp_s = _Group(0, bs, None)
    grp_p = _Group(bs, bp, seq)

    bf16_w = {}
    caches = dict(even=(cache_k, cache_v, page_table, state_conv, state_ssm), odd=(cache_win_k, cache_win_v))
    ys, ksm, vsm, cvs, sss, wks, wvs = _trunk(x_sample.reshape(bs, d), mod, w, grp_s, nseq=bs, seq=1,
                                              caches=caches, bf16_w=bf16_w)
    yp, kp, vp, cvp, ssp, wkp, wvp = _trunk(x_prompt.reshape(bp * seq, d), mod, w, grp_p, nseq=bp, seq=seq,
                                            caches=None, bf16_w=bf16_w)
    return (yp.reshape(bp, seq, d), ys.reshape(bs, 1, d), kp, vp, ksm, vsm, cvp, cvs, ssp, sss,
            wkp, wvp, wks, wvs)
```

```python
import functools
import math

import jax
import jax.numpy as jnp
from jax import lax
from jax.experimental import pallas as pl
from jax.experimental.pallas import tpu as pltpu

F32 = jnp.float32
BF16 = jnp.bfloat16

DEPTH = 4
N_SUB = 3
FFN_RES = 0.5
EPS = 1e-6
NEG = -1e30
LOG2E = math.log2(math.e)
PAGE = 128
DA_KV = 4
DA_REP = 2
DA_D = 64
DA_Q = DA_KV * DA_REP * 2 * DA_D
DA_KVD = DA_KV * 2 * DA_D
SSM_INNER = 1024
SSM_P = 64
SSM_H = 16
SSM_G = 2
SSM_N = 128
CONV_W = 4
CONV_DIM = SSM_INNER + 2 * SSM_G * SSM_N
CHUNK = 128
EVEN_MAIN = DA_Q + 2 * DA_KVD + SSM_INNER + CONV_DIM
SW_D = 64
SW_KV = 4
SW_REP = 8
SW_Q = SW_KV * SW_REP * SW_D
SW_KVD = SW_KV * SW_D
WINDOW = 128

SUBLANES = 8
VMEM_LIMIT = 56 * 1024 * 1024


def _cp(sem, limit=VMEM_LIMIT):
    return pltpu.CompilerParams(dimension_semantics=sem, vmem_limit_bytes=limit)


def _silu(x):
    return x / (1.0 + jnp.exp(-x))


def _softplus(x):
    return jnp.maximum(x, 0.0) + jnp.log(1.0 + jnp.exp(-jnp.abs(x)))


def _rms(x, g):
    return x * lax.rsqrt(jnp.mean(x * x, axis=-1, keepdims=True) + EPS) * g


def _dot(a, b):
    return jnp.dot(a.astype(BF16), b.astype(BF16), preferred_element_type=F32)


def _dot_nt(a, b):
    return lax.dot_general(a.astype(BF16), b.astype(BF16), (((1,), (1,)), ((), ())),
                           preferred_element_type=F32)


def _dot_tn(a, b):
    return lax.dot_general(a.astype(BF16), b.astype(BF16), (((0,), (0,)), ((), ())),
                           preferred_element_type=F32)


def _mod_kernel(c_ref, w_ref, b_ref, o_ref):
    h = _silu(c_ref[...])
    o_ref[0] = _dot(h, w_ref[0]) + b_ref[0]


def _modulation(c_all, w_mod, b_mod, tn=1024):
    rows, d = c_all.shape
    depth, _, n = w_mod.shape
    return pl.pallas_call(
        _mod_kernel,
        grid=(depth, n // tn),
        in_specs=[pl.BlockSpec((rows, d), lambda l, j: (0, 0)),
                  pl.BlockSpec((1, d, tn), lambda l, j: (l, 0, j)),
                  pl.BlockSpec((1, 1, tn), lambda l, j: (l, 0, j))],
        out_specs=pl.BlockSpec((1, rows, tn), lambda l, j: (l, 0, j)),
        out_shape=jax.ShapeDtypeStruct((depth, rows, n), F32),
        compiler_params=_cp(("parallel", "arbitrary")),
        name="modulation",
    )(c_all, w_mod, b_mod.reshape(depth, 1, n))


class _Group:
    def __init__(self, row0, nrows, seq):
        assert row0 % SUBLANES == 0
        self.row0, self.nrows, self.seq = row0, nrows, seq
        self.block_rows = nrows if seq is None else SUBLANES
        assert seq is not None or nrows % SUBLANES == 0
        assert seq is None or nrows <= SUBLANES

    def spec(self, layer, col, d):
        rb = self.row0 // self.block_rows
        return pl.BlockSpec((1, self.block_rows, d), lambda i, j: (layer, rb, col))

    def rows(self, ref, tm):
        if self.seq is None:
            return ref[0]
        return ref[0, pl.ds((pl.program_id(0) * tm) // self.seq, 1), :]


def _gn_spec(layer, d):
    return pl.BlockSpec((1, 2 * N_SUB, d), lambda i, j: (layer, 0, 0))


def _ffn_kernel(*refs, nf, res_w, kpre, grp, tm, emit):
    x_ref, sh_ref, sc_ref, gt_ref, gn_ref, wg_ref, wu_ref, wd_ref, o_ref = refs[:9]
    h_ref = refs[-1]
    f = pl.program_id(1)
    def split(n):
        return tuple((k * tm // n, (k + 1) * tm // n) for k in range(n)) if tm % (256 * n) == 0 else ((0, tm),)

    def mod_rows(ref, lo, hi):
        rows = grp.rows(ref, tm)
        return rows if rows.shape[0] == 1 else rows[lo:hi]

    def step(first, last):
        if emit:
            wg, wu, wd = (r[0, 0].astype(BF16) for r in (wg_ref, wu_ref, wd_ref))
            for out_ref, tile in zip(refs[9:12], (wg, wu, wd)):
                out_ref[0] = tile
        else:
            wg, wu, wd = wg_ref[0], wu_ref[0], wd_ref[0]
        for lo, hi in split(2 if first or last else 1):
            if first:
                xn = _rms(x_ref[lo:hi, :], gn_ref[0, kpre:kpre + 1, :])
                h = (xn * (1 + mod_rows(sc_ref, lo, hi)) + mod_rows(sh_ref, lo, hi)).astype(BF16)
                h_ref[lo:hi, :] = h
            else:
                h = h_ref[lo:hi, :]
            g = jnp.dot(h, wg, preferred_element_type=F32)
            u = jnp.dot(h, wu, preferred_element_type=F32)
            y = jnp.dot((_silu(g) * u).astype(BF16), wd, preferred_element_type=F32)
            if not first:
                y = o_ref[lo:hi, :] + y
            if last:
                yn = _rms(y, gn_ref[0, kpre + 1:kpre + 2, :])
                y = x_ref[lo:hi, :] + (res_w * mod_rows(gt_ref, lo, hi)) * yn
            o_ref[lo:hi, :] = y

    if nf == 1:
        step(True, True)
    else:
        pl.when(f == 0)(lambda: step(True, False))
        pl.when(f == nf - 1)(lambda: step(False, True))
        if nf > 2:
            pl.when(jnp.logical_and(f > 0, f < nf - 1))(lambda: step(False, False))


def _ffn(x, mod, g_norm, weights, l, s, sub, grp, *, tm, tf, tiles=None):
    m, d = x.shape
    emit = tiles is None
    if emit:
        nf = weights[0].shape[-1] // tf
        w_specs = [pl.BlockSpec((1, 1, d, tf), lambda i, j: (l, s, 0, j)),
                   pl.BlockSpec((1, 1, d, tf), lambda i, j: (l, s, 0, j)),
                   pl.BlockSpec((1, 1, tf, d), lambda i, j: (l, s, j, 0))]
        w_args = weights
        x_kw = {}
    else:
        nf = tiles[0].shape[0]
        w_specs = [pl.BlockSpec((1, d, tf), lambda i, j: (j, 0, 0)),
                   pl.BlockSpec((1, d, tf), lambda i, j: (j, 0, 0)),
                   pl.BlockSpec((1, tf, d), lambda i, j: (j, 0, 0))]
        w_args = tiles
        x_kw = {}
    out_specs = [pl.BlockSpec((tm, d), lambda i, j: (i, 0))]
    out_shape = [jax.ShapeDtypeStruct((m, d), F32)]
    if emit:
        assert m == tm
        out_specs += [pl.BlockSpec((1, d, tf), lambda i, j: (j, 0, 0)),
                      pl.BlockSpec((1, d, tf), lambda i, j: (j, 0, 0)),
                      pl.BlockSpec((1, tf, d), lambda i, j: (j, 0, 0))]
        out_shape += [jax.ShapeDtypeStruct((nf, d, tf), BF16), jax.ShapeDtypeStruct((nf, d, tf), BF16),
                      jax.ShapeDtypeStruct((nf, tf, d), BF16)]
    kern = functools.partial(_ffn_kernel, nf=nf, res_w=FFN_RES, kpre=2 * sub, grp=grp, tm=tm, emit=emit)
    out = pl.pallas_call(
        kern,
        grid=(m // tm, nf),
        in_specs=[pl.BlockSpec((tm, d), lambda i, j: (i, 0), **x_kw),
                  grp.spec(l, 3 * sub + 0, d), grp.spec(l, 3 * sub + 1, d), grp.spec(l, 3 * sub + 2, d),
                  _gn_spec(l, d)] + w_specs,
        out_specs=out_specs,
        out_shape=out_shape,
        scratch_shapes=[pltpu.VMEM((tm, d), BF16)],
        compiler_params=_cp(("parallel", "arbitrary")),
        name="ffn_emit" if emit else "ffn",
    )(x, mod, mod, mod, g_norm, *w_args)
    return (out[0], tuple(out[1:])) if emit else out[0]


def _inproj_kernel(*refs, transposed, has_bias, has_dt, grp, tm):
    refs = list(refs)
    x_ref, sh_ref, sc_ref, gn_ref, w_ref = refs[:5]
    rest = refs[5:]
    b_ref = rest.pop(0) if has_bias else None
    wdt_ref = rest.pop(0) if has_dt else None
    o_ref = rest.pop(0)
    dt_ref = rest.pop(0) if has_dt else None
    wbo_ref, h_ref, wb_ref = rest
    i = pl.program_id(0)
    j = pl.program_id(1)

    @pl.when(j == 0)
    def _():
        xn = _rms(x_ref[...], gn_ref[0, 2:3, :])
        h = (xn * (1 + grp.rows(sc_ref, tm)) + grp.rows(sh_ref, tm)).astype(BF16)
        h_ref[...] = h
        if has_dt:
            dt_ref[...] = _dot_nt(h, wdt_ref[0])

    @pl.when(i == 0)
    def _():
        wf = w_ref[0]
        wb_ref[j] = wf.astype(BF16)
        wbo_ref[...] = (wf.T if transposed else wf).astype(BF16)

    w = wb_ref[j]
    if transposed:
        o = lax.dot_general(h_ref[...], w, (((1,), (1,)), ((), ())), preferred_element_type=F32)
    else:
        o = jnp.dot(h_ref[...], w, preferred_element_type=F32)
    o_ref[...] = o + b_ref[...] if has_bias else o


def _inproj(x, mod, g_norm, w, b, l, layer, n, grp, *, tm, tn, transposed, n_dt=0):
    m, d = x.shape
    assert m == tm
    nj = n // tn
    has_bias = b is not None
    has_dt = n_dt > 0
    wcol = lambda i, j: jnp.where(i == 0, j, nj - 1)
    if transposed:
        w_spec = pl.BlockSpec((1, tn, d), lambda i, j: (layer, wcol(i, j), 0))
        wb_shape = (nj, tn, d)
    else:
        w_spec = pl.BlockSpec((1, d, tn), lambda i, j: (layer, 0, wcol(i, j)))
        wb_shape = (nj, d, tn)
    in_specs = [pl.BlockSpec((tm, d), lambda i, j: (i, 0)),
                grp.spec(l, 3, d), grp.spec(l, 4, d), _gn_spec(l, d), w_spec]
    args = [x, mod, mod, g_norm, w]
    if has_bias:
        in_specs.append(pl.BlockSpec((1, tn), lambda i, j: (0, j)))
        args.append(b)
    out_specs = [pl.BlockSpec((tm, tn), lambda i, j: (i, j))]
    out_shape = [jax.ShapeDtypeStruct((m, n), F32)]
    if has_dt:
        assert transposed and n % n_dt == 0
        in_specs.append(pl.BlockSpec((1, n_dt, d), lambda i, j: (layer, n // n_dt, 0)))
        args.append(w)
        out_specs.append(pl.BlockSpec((tm, n_dt), lambda i, j: (i, 0)))
        out_shape.append(jax.ShapeDtypeStruct((m, n_dt), F32))
    out_specs.append(pl.BlockSpec((d, tn), lambda i, j: (0, j)))
    out_shape.append(jax.ShapeDtypeStruct((d, n), BF16))
    return pl.pallas_call(
        functools.partial(_inproj_kernel, transposed=transposed, has_bias=has_bias, has_dt=has_dt,
                          grp=grp, tm=tm),
        grid=(m // tm, nj),
        in_specs=in_specs,
        out_specs=out_specs,
        out_shape=out_shape,
        scratch_shapes=[pltpu.VMEM((tm, d), BF16), pltpu.VMEM(wb_shape, BF16)],
        compiler_params=_cp(("arbitrary", "arbitrary")),
        name="inproj_emit",
    )(*args)


def _inproj_res_kernel(*refs, n, chunk, has_bias, has_dt, n_alias, grp, tm):
    refs = list(refs)
    x_ref, sh_ref, sc_ref, gn_ref, wb_ref = refs[:5]
    rest = refs[5:]
    b_ref = rest.pop(0) if has_bias else None
    wdt_ref = rest.pop(0) if has_dt else None
    del rest[:n_alias]
    o_ref = rest.pop(0)
    scale, shift = grp.rows(sc_ref, tm), grp.rows(sh_ref, tm)
    assert scale.shape[0] == 1
    half = tm // 2
    for lo in (0, half):
        xn = _rms(x_ref[lo:lo + half, :], gn_ref[0, 2:3, :])
        h = (xn * (1 + scale) + shift).astype(BF16)
        if has_dt:
            rest[0][lo:lo + half, :] = _dot_nt(h, wdt_ref[0])
        for c in range(n // chunk):
            cols = slice(c * chunk, (c + 1) * chunk)
            o = jnp.dot(h, wb_ref[:, cols], preferred_element_type=F32)
            o_ref[lo:lo + half, cols] = o + b_ref[:, cols] if has_bias else o
            if has_dt and c * chunk in (DA_Q, DA_Q + DA_KVD):
                kv_ref = rest[1] if c * chunk == DA_Q else rest[2]
                for g in range(DA_KV):
                    kv_ref[0, pl.ds(lo * DA_KV + g, half, stride=DA_KV), :] = o[:, g * 2 * DA_D:(g + 1) * 2 * DA_D]


def _inproj_res(x, mod, g_norm, wb, b, w_dt, l, layer, grp, *, tm, n_dt=0, chunk=512, kv_stack=None):
    assert chunk == DA_KVD
    m, d = x.shape
    n = wb.shape[1]
    has_bias = b is not None
    has_dt = n_dt > 0
    once = dict(pipeline_mode=pl.Buffered(1))
    in_specs = [pl.BlockSpec((tm, d), lambda i, j: (i, 0)),
                grp.spec(l, 3, d), grp.spec(l, 4, d), _gn_spec(l, d),
                pl.BlockSpec((d, n), lambda i, j: (0, 0), **once)]
    args = [x, mod, mod, g_norm, wb]
    if has_bias:
        in_specs.append(pl.BlockSpec((1, n), lambda i, j: (0, 0)))
        args.append(b)
    out_specs = [pl.BlockSpec((tm, n), lambda i, j: (i, 0))]
    out_shape = [jax.ShapeDtypeStruct((m, n), F32)]
    if has_dt:
        in_specs.append(pl.BlockSpec((1, n_dt, d), lambda i, j: (layer, n // n_dt, 0)))
        args.append(w_dt)
        out_specs.append(pl.BlockSpec((tm, n_dt), lambda i, j: (i, 0)))
        out_shape.append(jax.ShapeDtypeStruct((m, n_dt), F32))
        n_layers = w_dt.shape[0]
        for _ in range(2):
            out_specs.append(pl.BlockSpec((1, tm * DA_KV, 2 * DA_D), lambda i, j: (layer, i, 0)))
            out_shape.append(jax.ShapeDtypeStruct((n_layers, m * DA_KV, 2 * DA_D), F32))
    aliases = {}
    if kv_stack is not None:
        for t, stack in enumerate(kv_stack):
            aliases[len(args)] = 2 + t
            in_specs.append(pl.BlockSpec(memory_space=pl.ANY))
            args.append(stack)
    out = pl.pallas_call(
        functools.partial(_inproj_res_kernel, n=n, chunk=chunk, has_bias=has_bias, has_dt=has_dt,
                          n_alias=len(aliases), grp=grp, tm=tm),
        grid=(m // tm, 1),
        in_specs=in_specs,
        out_specs=out_specs,
        out_shape=out_shape,
        input_output_aliases=aliases,
        compiler_params=_cp(("parallel", "arbitrary")),
        name="inproj",
    )(*args)
    return out if has_dt else out[0]


def _outproj_kernel(*refs, bounds, has_bias, grp, tm):
    refs = list(refs)
    x_ref = refs.pop(0)
    y_refs = [refs.pop(0) for _ in bounds]
    gt_ref, gn_ref, w_ref = refs[:3]
    rest = refs[3:]
    b_ref = rest.pop(0) if has_bias else None
    o_ref, wbo_ref, wb_ref = rest
    nk = bounds[-1][1]
    i = pl.program_id(0)
    k = pl.program_id(1)

    @pl.when(i == 0)
    def _():
        wb = w_ref[0].astype(BF16)
        wb_ref[k] = wb
        wbo_ref[...] = wb

    @pl.when(k == 0)
    def _():
        o_ref[...] = jnp.zeros(o_ref.shape, F32)

    for y_ref, (lo, hi) in zip(y_refs, bounds):
        @pl.when(jnp.logical_and(k >= lo, k < hi))
        def _(y_ref=y_ref):
            o_ref[...] += jnp.dot(y_ref[...].astype(BF16), wb_ref[k], preferred_element_type=F32)

    @pl.when(k == nk - 1)
    def _():
        y = o_ref[...] + b_ref[...] if has_bias else o_ref[...]
        o_ref[...] = x_ref[...] + grp.rows(gt_ref, tm) * _rms(y, gn_ref[0, 3:4, :])


def _outproj(x, ys, mod, g_norm, w, b, l, layer, grp, *, tm, tk):
    m, d = x.shape
    has_bias = b is not None
    bounds, lo = [], 0
    for y in ys:
        bounds.append((lo, lo + y.shape[1] // tk))
        lo = bounds[-1][1]
    nk = lo
    assert nk * tk == w.shape[1]

    def y_spec(lo, hi):
        return pl.BlockSpec((tm, tk), lambda i, k: (i, jnp.clip(k - lo, 0, hi - lo - 1)))

    in_specs = ([pl.BlockSpec((tm, d), lambda i, k: (i, 0))]
                + [y_spec(lo, hi) for lo, hi in bounds]
                + [grp.spec(l, 5, d), _gn_spec(l, d),
                   pl.BlockSpec((1, tk, d), lambda i, k: (layer, jnp.where(i == 0, k, nk - 1), 0))])
    args = [x, *ys, mod, g_norm, w]
    if has_bias:
        in_specs.append(pl.BlockSpec((1, d), lambda i, k: (0, 0)))
        args.append(b)
    assert m == tm
    return pl.pallas_call(
        functools.partial(_outproj_kernel, bounds=tuple(bounds), has_bias=has_bias, grp=grp, tm=tm),
        grid=(m // tm, nk),
        in_specs=in_specs,
        out_specs=[pl.BlockSpec((tm, d), lambda i, k: (i, 0)), pl.BlockSpec((tk, d), lambda i, k: (k, 0))],
        out_shape=[jax.ShapeDtypeStruct((m, d), F32), jax.ShapeDtypeStruct((nk * tk, d), BF16)],
        scratch_shapes=[pltpu.VMEM((nk, tk, d), BF16)],
        compiler_params=_cp(("arbitrary", "arbitrary")),
        name="outproj_emit",
    )(*args)


def _outproj_res_kernel(*refs, n_y, has_bias, grp, tm):
    x_ref = refs[0]
    y_refs = refs[1:1 + n_y]
    gt_ref, gn_ref, wb_ref = refs[1 + n_y:4 + n_y]
    b_ref = refs[4 + n_y] if has_bias else None
    o_ref = refs[-1]
    gate = grp.rows(gt_ref, tm)
    assert gate.shape[0] == 1
    half = tm // 2
    for lo in (0, half):
        rows = slice(lo, lo + half)
        ycat = jnp.concatenate([r[rows, :].astype(BF16) for r in y_refs], axis=-1)
        y = jnp.dot(ycat, wb_ref[...], preferred_element_type=F32)
        if has_bias:
            y = y + b_ref[...]
        o_ref[rows, :] = x_ref[rows, :] + gate * _rms(y, gn_ref[0, 3:4, :])


def _outproj_res(x, ys, mod, g_norm, wb, b, l, grp, *, tm):
    m, d = x.shape
    has_bias = b is not None
    in_specs = ([pl.BlockSpec((tm, d), lambda i, k: (i, 0))]
                + [pl.BlockSpec((tm, y.shape[1]), lambda i, k: (i, 0)) for y in ys]
                + [grp.spec(l, 5, d), _gn_spec(l, d),
                   pl.BlockSpec(wb.shape, lambda i, k: (0, 0), pipeline_mode=pl.Buffered(1))])
    args = [x, *ys, mod, g_norm, wb]
    if has_bias:
        in_specs.append(pl.BlockSpec((1, d), lambda i, k: (0, 0)))
        args.append(b)
    return pl.pallas_call(
        functools.partial(_outproj_res_kernel, n_y=len(ys), has_bias=has_bias, grp=grp, tm=tm),
        grid=(m // tm, 1),
        in_specs=in_specs,
        out_specs=pl.BlockSpec((tm, d), lambda i, k: (i, 0)),
        out_shape=jax.ShapeDtypeStruct((m, d), F32),
        compiler_params=_cp(("parallel", "arbitrary")),
        name="outproj",
    )(*args)


def _lambda(lp, lam_init):
    a = jnp.sum(lp[0:1] * lp[1:2], axis=-1, keepdims=True)
    b = jnp.sum(lp[2:3] * lp[3:4], axis=-1, keepdims=True)
    return jnp.exp(a) - jnp.exp(b) + lam_init


def _da_prompt_kernel(q_ref, k_ref, v_ref, lp_ref, gs_ref, o_ref, *, tq, lam_init):
    i = pl.program_id(2)
    lam = _lambda(lp_ref[...], lam_init)
    q = q_ref[...] * (DA_D ** -0.5 * LOG2E)
    low = lax.broadcasted_iota(jnp.int32, (tq, 2 * DA_D), 1) < DA_D
    heads = (q[:, :2 * DA_D], q[:, 2 * DA_D:])
    qp = (jnp.concatenate([jnp.where(low, h, 0.0) for h in heads], axis=0).astype(BF16),
          jnp.concatenate([jnp.where(low, 0.0, h) for h in heads], axis=0).astype(BF16))
    key = lax.broadcasted_iota(jnp.int32, (tq, 2 * tq), 0)
    col = lax.broadcasted_iota(jnp.int32, (tq, 2 * tq), 1)
    causal_bias = jnp.where(key <= jnp.where(col >= tq, col - tq, col), 0.0, NEG)

    def step(j, carry, diagonal):
        off = pl.multiple_of(j * tq, tq)
        kb = k_ref[pl.ds(off, tq), :].astype(BF16)
        vb = v_ref[pl.ds(off, tq), :].astype(BF16)
        out = []
        for mi in range(2):
            m_old, l_old, acc = carry[3 * mi:3 * mi + 3]
            s = lax.dot_general(kb, qp[mi], (((1,), (1,)), ((), ())), preferred_element_type=F32)
            if diagonal:
                s = s + causal_bias
            m_new = jnp.maximum(m_old, jnp.max(s, axis=0, keepdims=True))
            alpha = jnp.exp2(m_old - m_new)
            p = jnp.exp2(s - m_new)
            l_new = alpha * l_old + jnp.sum(p, axis=0, keepdims=True)
            acc = alpha * acc + lax.dot_general(vb, p.astype(BF16), (((0,), (0,)), ((), ())),
                                                preferred_element_type=F32)
            out += [m_new, l_new, acc]
        return tuple(out)

    m0 = jnp.full((1, 2 * tq), NEG, F32)
    l0 = jnp.zeros((1, 2 * tq), F32)
    a0 = jnp.zeros((2 * DA_D, 2 * tq), F32)
    res = lax.fori_loop(0, i, lambda j, c: step(j, c, False), (m0, l0, a0, m0, l0, a0))
    res = step(i, res, True)
    a = res[2] / res[1] - lam * (res[5] / res[4])
    a = a * lax.rsqrt(jnp.mean(a * a, axis=0, keepdims=True) + EPS) * gs_ref[...] * (1 - lam_init)
    a = a.T
    o_ref[:, 0:128] = a[:tq]
    o_ref[:, 128:256] = a[tq:]


def _da_prompt(proj, lp, gs, nseq, seq, lam_init, tq=512):
    m = proj.shape[0]
    tq = min(tq, seq)
    nq = seq // tq
    kcol = DA_Q // 128
    vcol = (DA_Q + DA_KVD) // 128
    return pl.pallas_call(
        functools.partial(_da_prompt_kernel, tq=tq, lam_init=lam_init),
        grid=(nseq, DA_KV, nq),
        in_specs=[pl.BlockSpec((tq, 256), lambda b, g, i: (b * nq + i, g)),
                  pl.BlockSpec((seq, 128), lambda b, g, i: (b, kcol + g)),
                  pl.BlockSpec((seq, 128), lambda b, g, i: (b, vcol + g)),
                  pl.BlockSpec((4, DA_D), lambda b, g, i: (0, 0)),
                  pl.BlockSpec((2 * DA_D, 1), lambda b, g, i: (0, 0))],
        out_specs=pl.BlockSpec((tq, 256), lambda b, g, i: (b * nq + i, g)),
        out_shape=jax.ShapeDtypeStruct((m, DA_Q), F32),
        compiler_params=_cp(("parallel", "parallel", "arbitrary")),
        name="diff_attn_prompt",
    )(proj, proj, proj, lp, gs.reshape(-1, 1))


def _da_decode_kernel(pt_ref, qm_ref, kn_ref, vn_ref, lp_ref, gs_ref, *refs, pc, nc, lam_init):
    del pt_ref
    k_refs = refs[:pc]
    v_refs = refs[pc:2 * pc]
    o_ref, m_ref, l_ref, acc_ref, kb_ref, vb_ref = refs[2 * pc:]
    c = pl.program_id(1)

    @pl.when(c == 0)
    def _():
        m_ref[...] = jnp.full(m_ref.shape, NEG, F32)
        l_ref[...] = jnp.zeros(l_ref.shape, F32)
        acc_ref[...] = jnp.zeros(acc_ref.shape, F32)

    rows = PAGE * DA_KV
    for t in range(pc):
        kb_ref[t * rows:(t + 1) * rows, :] = k_refs[t][0, 0].astype(BF16)
        vb_ref[t * rows:(t + 1) * rows, :] = v_refs[t][0, 0].astype(BF16)
    qm = qm_ref[0] * (DA_D ** -0.5)
    s = _dot_nt(qm, kb_ref[...])
    row_g = (lax.broadcasted_iota(jnp.int32, s.shape, 0) // DA_REP) & (DA_KV - 1)
    col_g = lax.broadcasted_iota(jnp.int32, s.shape, 1) & (DA_KV - 1)
    s = jnp.where(row_g == col_g, s, NEG)
    m_old = m_ref[...]
    m_new = jnp.maximum(m_old, jnp.max(s, axis=-1, keepdims=True))
    alpha = jnp.exp(m_old - m_new)
    p = jnp.exp(s - m_new)
    l_ref[...] = alpha * l_ref[...] + jnp.sum(p, axis=-1, keepdims=True)
    acc_ref[...] = alpha * acc_ref[...] + jnp.dot(p.astype(BF16), vb_ref[...], preferred_element_type=F32)
    m_ref[...] = m_new

    @pl.when(c == nc - 1)
    def _():
        lam = _lambda(lp_ref[...], lam_init)
        s_new = jnp.sum(qm * kn_ref[0], axis=-1, keepdims=True)
        m_old = m_ref[...]
        m_fin = jnp.maximum(m_old, s_new)
        alpha = jnp.exp(m_old - m_fin)
        p_new = jnp.exp(s_new - m_fin)
        l_fin = alpha * l_ref[...] + p_new
        acc = alpha * acc_ref[...] + p_new * vn_ref[0]
        o = acc / l_fin
        a = o[0:8] - lam * o[8:16]
        o_ref[0] = _rms(a, gs_ref[...]) * (1 - lam_init)


def _da_decode(q, k_new, v_new, lp, gs, cache_k, cache_v, page_table, j, lam_init, pc=32):
    nb = q.shape[0]
    n_pages = page_table.shape[1]
    pc = min(pc, n_pages)
    nc = n_pages // pc
    rows = PAGE * DA_KV
    ck = cache_k.reshape(cache_k.shape[0], cache_k.shape[1], rows, 2 * DA_D)
    cv = cache_v.reshape(cache_v.shape[0], cache_v.shape[1], rows, 2 * DA_D)
    qt = q.reshape(nb, DA_KV, DA_REP, 2, DA_D).transpose(0, 3, 1, 2, 4)
    own_m = jnp.eye(2, dtype=bool)[None, :, None, None, :, None]
    qm = jnp.where(own_m, qt[:, :, :, :, None, :], 0.0).reshape(nb, 16, 2 * DA_D)
    rep = lambda a: jnp.broadcast_to(a.reshape(nb, 1, DA_KV, 1, 2 * DA_D),
                                     (nb, 2, DA_KV, DA_REP, 2 * DA_D)).reshape(nb, 16, 2 * DA_D)

    def page_spec(t):
        return pl.BlockSpec((1, 1, rows, 2 * DA_D), lambda b, c, pt: (j, pt[b, c * pc + t], 0, 0))

    row_spec = pl.BlockSpec((1, 16, 2 * DA_D), lambda b, c, pt: (b, 0, 0))
    grid_spec = pltpu.PrefetchScalarGridSpec(
        num_scalar_prefetch=1,
        grid=(nb, nc),
        in_specs=[row_spec, row_spec, row_spec,
                  pl.BlockSpec((4, DA_D), lambda b, c, pt: (0, 0)),
                  pl.BlockSpec((1, 2 * DA_D), lambda b, c, pt: (0, 0))]
                 + [page_spec(t) for t in range(pc)] + [page_spec(t) for t in range(pc)],
        out_specs=pl.BlockSpec((1, 8, 128), lambda b, c, pt: (b, 0, 0)),
        scratch_shapes=[pltpu.VMEM((16, 1), F32), pltpu.VMEM((16, 1), F32),
                        pltpu.VMEM((16, 2 * DA_D), F32),
                        pltpu.VMEM((pc * rows, 2 * DA_D), BF16), pltpu.VMEM((pc * rows, 2 * DA_D), BF16)])
    out = pl.pallas_call(
        functools.partial(_da_decode_kernel, pc=pc, nc=nc, lam_init=lam_init),
        grid_spec=grid_spec,
        out_shape=jax.ShapeDtypeStruct((nb, 8, 128), F32),
        compiler_params=_cp(("parallel", "arbitrary")),
        name="diff_attn_decode",
    )(page_table, qm, rep(k_new), rep(v_new), lp, gs.reshape(1, -1), *([ck] * pc), *([cv] * pc))
    return out.reshape(nb, DA_Q)


def _split3(x):
    x1 = x.astype(BF16)
    r1 = x - x1.astype(F32)
    x2 = r1.astype(BF16)
    x3 = (r1 - x2.astype(F32)).astype(BF16)
    return x1, x2, x3


def _group_norm_gate(y, z, gn):
    yz = y * _silu(z)
    w = SSM_INNER // SSM_G
    parts = []
    for g in range(SSM_G):
        seg = yz[:, g * w:(g + 1) * w]
        parts.append(seg * lax.rsqrt(jnp.mean(seg * seg, axis=-1, keepdims=True) + EPS))
    return jnp.concatenate(parts, axis=-1) * gn


def _ssd_prompt_kernel(xbc_ref, z_ref, dt_ref, dtt_ref, cw_ref, cb_ref, dtb_ref, dtbt_ref,
                       alog_ref, alogt_ref, dskipt_ref, gn_ref, y_ref, st_ref, ext_ref, h_ref, *, nchunk):
    c = pl.program_id(1)
    q = CHUNK

    @pl.when(c == 0)
    def _():
        ext_ref[0:8, :] = jnp.zeros((8, CONV_DIM), F32)
        h_ref[...] = jnp.zeros(h_ref.shape, F32)

    @pl.when(c > 0)
    def _():
        ext_ref[0:8, :] = ext_ref[q:q + 8, :]

    ext_ref[8:q + 8, :] = xbc_ref[...]
    cw = cw_ref[...]
    acc = ext_ref[5:q + 5, :] * cw[0:1]
    for kk in range(1, CONV_W):
        acc = acc + ext_ref[5 + kk:q + 5 + kk, :] * cw[kk:kk + 1]
    xa = _silu(acc + cb_ref[...])
    xs = xa[:, :SSM_INNER]
    bm = xa[:, SSM_INNER:SSM_INNER + SSM_G * SSM_N].astype(BF16)
    cm = xa[:, SSM_INNER + SSM_G * SSM_N:].astype(BF16)

    dt = _softplus(dt_ref[...] + dtb_ref[...])
    dtt = _softplus(dtt_ref[...] + dtbt_ref[...])
    a = dt * (-jnp.exp(alog_ref[...]))
    at = dtt * (-jnp.exp(alogt_ref[...]))
    ri = lax.broadcasted_iota(jnp.int32, (q, q), 0)
    ci = lax.broadcasted_iota(jnp.int32, (q, q), 1)
    lower = ri >= ci
    tril = jnp.where(lower, 1.0, 0.0).astype(BF16)
    triu = jnp.where(ri <= ci, 1.0, 0.0).astype(BF16)
    cs = sum(jnp.dot(tril, part, preferred_element_type=F32) for part in _split3(a))
    cst = sum(jnp.dot(part, triu, preferred_element_type=F32) for part in _split3(at))
    tot = cst[:, q - 1:q]
    e_cst = jnp.exp(cst)
    e_dect = jnp.exp(tot - cst)
    e_tot = jnp.exp(tot)
    dskip = dskipt_ref[...]
    xst = xs.T
    upper = ri <= ci

    yts = []
    for g in range(SSM_G):
        bg = bm[:, g * SSM_N:(g + 1) * SSM_N]
        cg = cm[:, g * SSM_N:(g + 1) * SSM_N]
        cbt = lax.dot_general(bg, cg, (((1,), (1,)), ((), ())), preferred_element_type=F32)
        for r in range(SSM_H // SSM_G):
            h = g * (SSM_H // SSM_G) + r
            lmat_t = jnp.exp(jnp.where(upper, cst[h:h + 1, :] - cs[:, h:h + 1], NEG))
            xt_h = xst[h * SSM_P:(h + 1) * SSM_P, :]
            xdt_t = xt_h * dtt[h:h + 1, :]
            y_diag = _dot(xdt_t, cbt * lmat_t)
            hp = h_ref[h]
            y_off = lax.dot_general(hp.astype(BF16), cg, (((1,), (1,)), ((), ())),
                                    preferred_element_type=F32) * e_cst[h:h + 1, :]
            upd = _dot(xdt_t * e_dect[h:h + 1, :], bg)
            h_ref[h] = e_tot[h:h + 1, :] * hp + upd
            yts.append(y_diag + y_off + dskip[h:h + 1, :] * xt_h)
    y = jnp.concatenate(yts, axis=0).T
    y_ref[...] = _group_norm_gate(y, z_ref[...], gn_ref[...])

    @pl.when(c == nchunk - 1)
    def _():
        st_ref[0] = h_ref[...]


def _ssd_prompt(proj, dt_raw, cw, cb, dtb, alog, dskip, gn, nseq, seq):
    m = proj.shape[0]
    nchunk = seq // CHUNK
    zcol = (DA_Q + 2 * DA_KVD) // SSM_INNER
    xcol = (DA_Q + 2 * DA_KVD + SSM_INNER) // CONV_DIM
    const = lambda b, c: (0, 0)
    y, st = pl.pallas_call(
        functools.partial(_ssd_prompt_kernel, nchunk=nchunk),
        grid=(nseq, nchunk),
        in_specs=[pl.BlockSpec((CHUNK, CONV_DIM), lambda b, c: (b * nchunk + c, xcol)),
                  pl.BlockSpec((CHUNK, SSM_INNER), lambda b, c: (b * nchunk + c, zcol)),
                  pl.BlockSpec((CHUNK, SSM_H), lambda b, c: (b * nchunk + c, 0)),
                  pl.BlockSpec((SSM_H, CHUNK), lambda b, c: (0, b * nchunk + c)),
                  pl.BlockSpec((CONV_W, CONV_DIM), const),
                  pl.BlockSpec((1, CONV_DIM), const),
                  pl.BlockSpec((1, SSM_H), const),
                  pl.BlockSpec((SSM_H, 1), const),
                  pl.BlockSpec((1, SSM_H), const),
                  pl.BlockSpec((SSM_H, 1), const),
                  pl.BlockSpec((SSM_H, 1), const),
                  pl.BlockSpec((1, SSM_INNER), const)],
        out_specs=[pl.BlockSpec((CHUNK, SSM_INNER), lambda b, c: (b * nchunk + c, 0)),
                   pl.BlockSpec((1, SSM_H, SSM_P, SSM_N), lambda b, c: (b, 0, 0, 0))],
        out_shape=[jax.ShapeDtypeStruct((m, SSM_INNER), F32),
                   jax.ShapeDtypeStruct((nseq, SSM_H, SSM_P, SSM_N), F32)],
        scratch_shapes=[pltpu.VMEM((CHUNK + 8, CONV_DIM), F32),
                        pltpu.VMEM((SSM_H, SSM_P, SSM_N), F32)],
        compiler_params=_cp(("parallel", "arbitrary")),
        name="ssd_prompt",
    )(proj, proj, dt_raw, dt_raw.T, cw, cb.reshape(1, -1), dtb.reshape(1, -1), dtb.reshape(-1, 1),
      alog.reshape(1, -1), alog.reshape(-1, 1), dskip.reshape(-1, 1), gn.reshape(1, -1))
    return y, st


def _ssd_dec_pre_kernel(xbc_ref, cst_ref, dt_ref, cw_ref, cb_ref, dtb_ref, xa_ref, dts_ref):
    cw = cw_ref[...]
    acc = cst_ref[0] * cw[0:1]
    for kk in range(1, CONV_W - 1):
        acc = acc + cst_ref[kk] * cw[kk:kk + 1]
    acc = acc + xbc_ref[...] * cw[CONV_W - 1:CONV_W]
    xa_ref[...] = _silu(acc + cb_ref[...])
    dts_ref[...] = _softplus(dt_ref[...] + dtb_ref[...])


def _ssd_dec_state_kernel(xt_ref, dts_ref, alog_ref, b_ref, c_ref, dskip_ref, h_ref, *rest):
    yt_ref, hn_ref = rest[-2:]
    dskip = dskip_ref[...]
    for s in range(xt_ref.shape[0]):
        xt = xt_ref[s]
        dts = dts_ref[s]
        a = dts * (-jnp.exp(alog_ref[...]))
        e_a = jnp.exp(a)
        cols = []
        for h in range(SSM_H):
            g = h // (SSM_H // SSM_G)
            xcol = xt[:, h:h + 1]
            brow = b_ref[s][:, g * SSM_N:(g + 1) * SSM_N]
            crow = c_ref[s][:, g * SSM_N:(g + 1) * SSM_N]
            hn = e_a[:, h:h + 1] * h_ref[0, s, h] + (xcol * dts[:, h:h + 1]) * brow
            hn_ref[0, s, h] = hn
            cols.append(jnp.sum(hn * crow, axis=-1, keepdims=True) + dskip[:, h:h + 1] * xcol)
        yt_ref[s] = jnp.concatenate(cols, axis=-1)


def _ssd_dec_post_kernel(y_ref, z_ref, gn_ref, o_ref):
    o_ref[...] = _group_norm_gate(y_ref[...], z_ref[...], gn_ref[...])


def _ssd_decode(proj, dt_raw, conv_state, ssm_state, state_stack, layer, cw, cb, dtb, alog, dskip, gn):
    nb = proj.shape[0]
    zcol = (DA_Q + 2 * DA_KVD) // SSM_INNER
    xcol = (DA_Q + 2 * DA_KVD + SSM_INNER) // CONV_DIM
    full = lambda shape: pl.BlockSpec(shape, lambda i: (0,) * len(shape))
    xa, dts = pl.pallas_call(
        _ssd_dec_pre_kernel,
        grid=(1,),
        in_specs=[pl.BlockSpec((nb, CONV_DIM), lambda i: (0, xcol)),
                  full((CONV_W - 1, nb, CONV_DIM)),
                  full((nb, SSM_H)),
                  full((CONV_W, CONV_DIM)),
                  full((1, CONV_DIM)),
                  full((1, SSM_H))],
        out_specs=[full((nb, CONV_DIM)), full((nb, SSM_H))],
        out_shape=[jax.ShapeDtypeStruct((nb, CONV_DIM), F32), jax.ShapeDtypeStruct((nb, SSM_H), F32)],
        compiler_params=_cp(("arbitrary",)),
        name="ssd_decode_pre",
    )(proj, conv_state.transpose(1, 0, 2), dt_raw, cw, cb.reshape(1, -1), dtb.reshape(1, -1))
    sb = 1
    xt = xa[:, :SSM_INNER].reshape(nb, SSM_H, SSM_P).transpose(0, 2, 1)
    bmat = xa[:, SSM_INNER:SSM_INNER + SSM_G * SSM_N].reshape(nb, 1, SSM_G * SSM_N)
    cmat = xa[:, SSM_INNER + SSM_G * SSM_N:].reshape(nb, 1, SSM_G * SSM_N)
    state_block = pl.BlockSpec((1, sb, SSM_H, SSM_P, SSM_N), lambda b: (layer, b, 0, 0, 0))
    in_specs = [pl.BlockSpec((sb, SSM_P, SSM_H), lambda b: (b, 0, 0)),
                pl.BlockSpec((sb, 1, SSM_H), lambda b: (b, 0, 0)),
                pl.BlockSpec((1, SSM_H), lambda b: (0, 0)),
                pl.BlockSpec((sb, 1, SSM_G * SSM_N), lambda b: (b, 0, 0)),
                pl.BlockSpec((sb, 1, SSM_G * SSM_N), lambda b: (b, 0, 0)),
                pl.BlockSpec((1, SSM_H), lambda b: (0, 0)),
                state_block]
    args = [xt, dts.reshape(nb, 1, SSM_H), alog.reshape(1, -1), bmat, cmat, dskip.reshape(1, -1), ssm_state]
    aliases = {}
    if state_stack is not None:
        aliases[len(args)] = 1
        in_specs.append(pl.BlockSpec(memory_space=pl.ANY))
        args.append(state_stack)
    yt, h_new = pl.pallas_call(
        _ssd_dec_state_kernel,
        grid=(nb // sb,),
        in_specs=in_specs,
        out_specs=[pl.BlockSpec((sb, SSM_P, SSM_H), lambda b: (b, 0, 0)), state_block],
        out_shape=[jax.ShapeDtypeStruct((nb, SSM_P, SSM_H), F32),
                   jax.ShapeDtypeStruct(ssm_state.shape, F32)],
        input_output_aliases=aliases,
        compiler_params=_cp(("parallel",)),
        name="ssd_decode_state",
    )(*args)
    y = yt.transpose(0, 2, 1).reshape(nb, SSM_INNER)
    ob = pl.pallas_call(
        _ssd_dec_post_kernel,
        grid=(1,),
        in_specs=[full((nb, SSM_INNER)),
                  pl.BlockSpec((nb, SSM_INNER), lambda i: (0, zcol)),
                  full((1, SSM_INNER))],
        out_specs=full((nb, SSM_INNER)),
        out_shape=jax.ShapeDtypeStruct((nb, SSM_INNER), F32),
        compiler_params=_cp(("arbitrary",)),
        name="ssd_decode_post",
    )(y, proj, gn.reshape(1, -1))
    return ob, h_new


def _swa_prompt_kernel(sink_ref, q_ref, kp_ref, kc_ref, vp_ref, vc_ref, o_ref):
    i = pl.program_id(1)
    w = WINDOW
    pair_w = 2 * SW_D
    sinks = sink_ref[...] * LOG2E
    q = q_ref[...] * (SW_D ** -0.5 * LOG2E)
    kk = jnp.concatenate([kp_ref[...], kc_ref[...]], axis=0)
    vv = jnp.concatenate([vp_ref[...], vc_ref[...]], axis=0).astype(BF16)
    low_k = lax.broadcasted_iota(jnp.int32, (2 * w, pair_w), 1) < SW_D
    low_q = lax.broadcasted_iota(jnp.int32, (w, pair_w), 1) < SW_D
    key = lax.broadcasted_iota(jnp.int32, (2 * w, w), 0)
    t = key - lax.broadcasted_iota(jnp.int32, (2 * w, w), 1)
    keymin = jnp.where(i > 0, 0, w)
    bias1 = jnp.where(t >= 1, jnp.where(t <= w, jnp.where(key >= keymin, 0.0, NEG), NEG), NEG)
    bias = jnp.concatenate([bias1] * SW_REP, axis=1)
    outs = []
    for g in range(SW_KV):
        pair = g // 2
        k2 = kk[:, pair * pair_w:(pair + 1) * pair_w]
        k2r = pltpu.roll(k2, SW_D, axis=1)
        kdup = (jnp.where(low_k, k2, k2r) if g % 2 == 0 else jnp.where(low_k, k2r, k2)).astype(BF16)
        qs = []
        for r in range(SW_REP):
            h = g * SW_REP + r
            q2 = q[:, (h // 2) * pair_w:(h // 2 + 1) * pair_w]
            qs.append(jnp.where(low_q, q2, 0.0) if h % 2 == 0 else jnp.where(low_q, 0.0, q2))
        qst = jnp.concatenate(qs, axis=0).astype(BF16)
        s = lax.dot_general(kdup, qst, (((1,), (1,)), ((), ())), preferred_element_type=F32) + bias
        sink = jnp.concatenate([jnp.broadcast_to(sinks[:, g * SW_REP + r:g * SW_REP + r + 1], (1, w))
                                for r in range(SW_REP)], axis=1)
        mx = jnp.maximum(jnp.max(s, axis=0, keepdims=True), sink)
        p = jnp.exp2(s - mx)
        den = jnp.sum(p, axis=0, keepdims=True) + jnp.exp2(sink - mx)
        ot = lax.dot_general(vv[:, pair * pair_w:(pair + 1) * pair_w], p.astype(BF16),
                             (((0,), (0,)), ((), ())), preferred_element_type=F32)
        ot = ot[(g % 2) * SW_D:(g % 2 + 1) * SW_D, :] / den
        for pr in range(SW_REP // 2):
            blk = jnp.concatenate([ot[:, (2 * pr) * w:(2 * pr + 1) * w],
                                   ot[:, (2 * pr + 1) * w:(2 * pr + 2) * w]], axis=0)
            outs.append(blk.T)
    o_ref[...] = jnp.concatenate(outs, axis=1)


def _swa_prompt(proj, sinks, nseq, seq):
    m = proj.shape[0]
    nb = seq // WINDOW
    kcol = SW_Q // SW_KVD
    vcol = kcol + 1
    prev = lambda b, i: (b * nb + jnp.maximum(i - 1, 0), kcol)
    prev_v = lambda b, i: (b * nb + jnp.maximum(i - 1, 0), vcol)
    return pl.pallas_call(
        _swa_prompt_kernel,
        grid=(nseq, nb),
        in_specs=[pl.BlockSpec((1, SW_KV * SW_REP), lambda b, i: (0, 0)),
                  pl.BlockSpec((WINDOW, SW_Q), lambda b, i: (b * nb + i, 0)),
                  pl.BlockSpec((WINDOW, SW_KVD), prev),
                  pl.BlockSpec((WINDOW, SW_KVD), lambda b, i: (b * nb + i, kcol)),
                  pl.BlockSpec((WINDOW, SW_KVD), prev_v),
                  pl.BlockSpec((WINDOW, SW_KVD), lambda b, i: (b * nb + i, vcol))],
        out_specs=pl.BlockSpec((WINDOW, SW_Q), lambda b, i: (b * nb + i, 0)),
        out_shape=jax.ShapeDtypeStruct((m, SW_Q), F32),
        compiler_params=_cp(("parallel", "arbitrary")),
        name="swa_prompt",
    )(sinks.reshape(1, -1), proj, proj, proj, proj, proj)


def _swa_decode_kernel(q_ref, kn_ref, vn_ref, sink_ref, kb_ref, vb_ref, o_ref):
    nh = SW_KV * SW_REP
    rowg = lax.broadcasted_iota(jnp.int32, (nh, SW_KVD), 0) // SW_REP
    colg = lax.broadcasted_iota(jnp.int32, (nh, SW_KVD), 1) // SW_D
    own = rowg == colg
    col = lax.broadcasted_iota(jnp.int32, (nh, WINDOW), 1)
    sink = sink_ref[...]
    for b in range(q_ref.shape[0]):
        q = q_ref[b] * (SW_D ** -0.5)
        q4 = jnp.concatenate([q] * SW_KV, axis=-1)
        qbd = jnp.where(own, q4, 0.0)
        s = _dot(qbd, kb_ref[b])
        s = jnp.where(col >= 1, s, NEG)
        s_new = jnp.sum(qbd * kn_ref[b], axis=-1, keepdims=True)
        mx = jnp.maximum(jnp.maximum(jnp.max(s, axis=-1, keepdims=True), s_new), sink)
        p = jnp.exp(s - mx)
        p_new = jnp.exp(s_new - mx)
        den = jnp.sum(p, axis=-1, keepdims=True) + p_new + jnp.exp(sink - mx)
        o = (_dot_nt(p, vb_ref[b]) + p_new * vn_ref[b]) / den
        o = jnp.where(own, o, 0.0)
        out = o[:, 0:SW_D]
        for g in range(1, SW_KV):
            out = out + o[:, g * SW_D:(g + 1) * SW_D]
        o_ref[b] = out


def _swa_decode(q, k_new, v_new, sinks, win_k, win_v):
    nb = q.shape[0]
    nh = SW_KV * SW_REP
    sb = 4 if nb % 4 == 0 else 1
    out = pl.pallas_call(
        _swa_decode_kernel,
        grid=(nb // sb,),
        in_specs=[pl.BlockSpec((sb, nh, SW_D), lambda b: (b, 0, 0)),
                  pl.BlockSpec((sb, 1, SW_KVD), lambda b: (b, 0, 0)),
                  pl.BlockSpec((sb, 1, SW_KVD), lambda b: (b, 0, 0)),
                  pl.BlockSpec((nh, 1), lambda b: (0, 0)),
                  pl.BlockSpec((sb, SW_KVD, WINDOW), lambda b: (b, 0, 0)),
                  pl.BlockSpec((sb, SW_KVD, WINDOW), lambda b: (b, 0, 0))],
        out_specs=pl.BlockSpec((sb, nh, SW_D), lambda b: (b, 0, 0)),
        out_shape=jax.ShapeDtypeStruct((nb, nh, SW_D), F32),
        compiler_params=_cp(("parallel",)),
        name="swa_decode",
    )(q.reshape(nb, nh, SW_D), k_new.reshape(nb, 1, SW_KVD), v_new.reshape(nb, 1, SW_KVD),
      sinks.reshape(nh, 1), win_k.transpose(0, 2, 3, 1).reshape(nb, SW_KVD, WINDOW),
      win_v.transpose(0, 2, 3, 1).reshape(nb, SW_KVD, WINDOW))
    return out.reshape(nb, SW_Q)


def _trunk(x, mod, w, grp, *, nseq, seq, caches, bf16_w):
    prompt = caches is None
    tm_ffn, tm_in, tm_out = (min(1024, seq), min(512, seq), min(512, seq)) if prompt else (nseq,) * 3
    tf, tn, tk = 512, 512, 512
    g_norm = w['g_norm']
    ffn_w = (w['w_ffn_gate'], w['w_ffn_up'], w['w_ffn_down'])

    def ffn(x, l, s, sub):
        if prompt:
            return _ffn(x, mod, g_norm, None, l, s, sub, grp, tm=tm_ffn, tf=tf, tiles=bf16_w['ffn', l, s])
        x, bf16_w['ffn', l, s] = _ffn(x, mod, g_norm, ffn_w, l, s, sub, grp, tm=tm_ffn, tf=tf)
        return x

    ks, vs, convs, ssms, wks, wvs = [], [], [], [], [], []
    kv_stack = ssm_stack = None
    for l in range(DEPTH):
        j = l // 2
        x = ffn(x, l, 0, 0)
        if l % 2 == 0:
            lam_init = 0.8 - 0.6 * math.exp(-0.3 * l)
            w_in_t = w['w_in_even'].transpose(0, 2, 1)
            if prompt:
                proj, dt_raw, *kv_stack = _inproj_res(x, mod, g_norm, bf16_w['in', l], None, w_in_t, l, j,
                                                      grp, tm=tm_in, n_dt=SSM_H, kv_stack=kv_stack)
            else:
                proj, dt_raw, bf16_w['in', l] = _inproj(x, mod, g_norm, w_in_t, None, l, j, EVEN_MAIN, grp,
                                                        tm=tm_in, tn=tn, transposed=True, n_dt=SSM_H)
                k_rows = proj[:, DA_Q:DA_Q + DA_KVD]
                v_rows = proj[:, DA_Q + DA_KVD:DA_Q + 2 * DA_KVD]
            lp = w['lambda_qk'][j]
            gs = w['g_subln'][j]
            ssm_w = (w['conv_w'][j], w['conv_b'][j], w['dt_bias'][j], w['a_log'][j], w['d_skip'][j],
                     w['g_ssm_norm'][j])
            p3 = proj.reshape(nseq, seq, EVEN_MAIN)
            if prompt:
                oa = _da_prompt(proj, lp, gs, nseq, seq, lam_init)
                ob, ssm_new = _ssd_prompt(proj, dt_raw, *ssm_w, nseq, seq)
                ssms.append(ssm_new)
                conv_new = p3[:, seq - (CONV_W - 1):, EVEN_MAIN - CONV_DIM:]
            else:
                cache_k, cache_v, page_table, conv_state, ssm_state = caches['even']
                oa = _da_decode(proj[:, :DA_Q], proj[:, DA_Q:DA_Q + DA_KVD],
                                proj[:, DA_Q + DA_KVD:DA_Q + 2 * DA_KVD], lp, gs, cache_k, cache_v,
                                page_table, j, lam_init)
                ob, ssm_stack = _ssd_decode(proj, dt_raw, conv_state[j], ssm_state, ssm_stack, j, *ssm_w)
                conv_new = jnp.concatenate([conv_state[j][:, 1:], p3[:, :, EVEN_MAIN - CONV_DIM:]], axis=1)
            if not prompt:
                ks.append(k_rows.reshape(nseq, seq, DA_KV, 2 * DA_D))
                vs.append(v_rows.reshape(nseq, seq, DA_KV, 2 * DA_D))
            convs.append(conv_new)
            ys, w_out, b_out = [oa, ob], w['w_out_even'], None
        else:
            n_odd = SW_Q + 2 * SW_KVD
            b_in = w['b_in_odd'][j].reshape(1, -1)
            if prompt:
                proj = _inproj_res(x, mod, g_norm, bf16_w['in', l], b_in, None, l, j, grp, tm=tm_in)
            else:
                proj, bf16_w['in', l] = _inproj(x, mod, g_norm, w['w_in_odd'], b_in, l, j, n_odd, grp,
                                                tm=tm_in, tn=tn, transposed=False)
            p3 = proj.reshape(nseq, seq, n_odd)
            if prompt:
                y = _swa_prompt(proj, w['attn_sinks'][j], nseq, seq)
                tail = p3[:, seq - WINDOW:]
                wk_new = tail[:, :, SW_Q:SW_Q + SW_KVD].reshape(nseq, WINDOW, SW_KV, SW_D)
                wv_new = tail[:, :, SW_Q + SW_KVD:].reshape(nseq, WINDOW, SW_KV, SW_D)
            else:
                win_k, win_v = caches['odd']
                k_rows = proj[:, SW_Q:SW_Q + SW_KVD]
                v_rows = proj[:, SW_Q + SW_KVD:]
                y = _swa_decode(proj[:, :SW_Q], k_rows, v_rows, w['attn_sinks'][j], win_k[j], win_v[j])
                wk_new = jnp.concatenate([win_k[j][:, 1:], k_rows.reshape(nseq, 1, SW_KV, SW_D)], axis=1)
                wv_new = jnp.concatenate([win_v[j][:, 1:], v_rows.reshape(nseq, 1, SW_KV, SW_D)], axis=1)
            wks.append(wk_new)
            wvs.append(wv_new)
            ys, w_out, b_out = [y], w['w_out_odd'], w['b_out_odd'][j].reshape(1, -1)
        if prompt:
            x = _outproj_res(x, ys, mod, g_norm, bf16_w['out', l], b_out, l, grp, tm=tm_out)
        else:
            x, bf16_w['out', l] = _outproj(x, ys, mod, g_norm, w_out, b_out, l, j, grp, tm=tm_out, tk=tk)
        x = ffn(x, l, 1, 2)
    if prompt:
        k_all, v_all = (s.reshape(s.shape[0], nseq, seq, DA_KV, 2 * DA_D) for s in kv_stack)
        ssm_all = jnp.stack(ssms)
    else:
        k_all, v_all, ssm_all = jnp.stack(ks), jnp.stack(vs), ssm_stack
    return (x, k_all, v_all, jnp.stack(convs), ssm_all, jnp.stack(wks), jnp.stack(wvs))


def kernel(x_prompt, x_sample, cache_k, cache_v, state_conv, state_ssm, cache_win_k, cache_win_v, page_table,
           c_prompt, c_sample, w_mod, b_mod, g_norm, w_ffn_gate, w_ffn_up, w_ffn_down, w_in_even, lambda_qk,
           g_subln, conv_w, conv_b, dt_bias, a_log, d_skip, g_ssm_norm, w_out_even, w_in_odd, b_in_odd,
           attn_sinks, w_out_odd, b_out_odd):
    w = dict(g_norm=g_norm, w_ffn_gate=w_ffn_gate, w_ffn_up=w_ffn_up, w_ffn_down=w_ffn_down,
             w_in_even=w_in_even, lambda_qk=lambda_qk, g_subln=g_subln, conv_w=conv_w, conv_b=conv_b,
             dt_bias=dt_bias, a_log=a_log, d_skip=d_skip, g_ssm_norm=g_ssm_norm, w_out_even=w_out_even,
             w_in_odd=w_in_odd, b_in_odd=b_in_odd, attn_sinks=attn_sinks, w_out_odd=w_out_odd,
             b_out_odd=b_out_odd)
    bp, seq, d = x_prompt.shape
    bs, dec_seq, _ = x_sample.shape
    assert dec_seq == 1
    pad = (-(bs + bp)) % 16
    c_all = jnp.concatenate([c_sample, c_prompt, jnp.zeros((pad, d), F32)], axis=0)
    mod = _modulation(c_all, w_mod, b_mod)
    grp_s = _Group(0, bs, None)
    grp_p = _Group(bs, bp, seq)

    bf16_w = {}
    caches = dict(even=(cache_k, cache_v, page_table, state_conv, state_ssm), odd=(cache_win_k, cache_win_v))
    ys, ksm, vsm, cvs, sss, wks, wvs = _trunk(x_sample.reshape(bs, d), mod, w, grp_s, nseq=bs, seq=1,
                                              caches=caches, bf16_w=bf16_w)
    yp, kp, vp, cvp, ssp, wkp, wvp = _trunk(x_prompt.reshape(bp * seq, d), mod, w, grp_p, nseq=bp, seq=seq,
                                            caches=None, bf16_w=bf16_w)
    return (yp.reshape(bp, seq, d), ys.reshape(bs, 1, d), kp, vp, ksm, vsm, cvp, cvs, ssp, sss,
            wkp, wvp, wks, wvs)
```

```python
import functools
import math

import jax
import jax.numpy as jnp
from jax import lax
from jax.experimental import pallas as pl
from jax.experimental.pallas import tpu as pltpu

F32 = jnp.float32
BF16 = jnp.bfloat16

DEPTH = 4
N_SUB = 3
FFN_RES = 0.5
EPS = 1e-6
NEG = -1e30
LOG2E = math.log2(math.e)
PAGE = 128
DA_KV = 4
DA_REP = 2
DA_D = 64
DA_Q = DA_KV * DA_REP * 2 * DA_D
DA_KVD = DA_KV * 2 * DA_D
SSM_INNER = 1024
SSM_P = 64
SSM_H = 16
SSM_G = 2
SSM_N = 128
CONV_W = 4
CONV_DIM = SSM_INNER + 2 * SSM_G * SSM_N
CHUNK = 128
EVEN_MAIN = DA_Q + 2 * DA_KVD + SSM_INNER + CONV_DIM
SW_D = 64
SW_KV = 4
SW_REP = 8
SW_Q = SW_KV * SW_REP * SW_D
SW_KVD = SW_KV * SW_D
WINDOW = 128

SUBLANES = 8
VMEM_LIMIT = 56 * 1024 * 1024


def _cp(sem, limit=VMEM_LIMIT):
    return pltpu.CompilerParams(dimension_semantics=sem, vmem_limit_bytes=limit)


def _silu(x):
    return x / (1.0 + jnp.exp(-x))


def _softplus(x):
    return jnp.maximum(x, 0.0) + jnp.log(1.0 + jnp.exp(-jnp.abs(x)))


def _rms(x, g):
    return x * lax.rsqrt(jnp.mean(x * x, axis=-1, keepdims=True) + EPS) * g


def _dot(a, b):
    return jnp.dot(a.astype(BF16), b.astype(BF16), preferred_element_type=F32)


def _dot_nt(a, b):
    return lax.dot_general(a.astype(BF16), b.astype(BF16), (((1,), (1,)), ((), ())),
                           preferred_element_type=F32)


def _dot_tn(a, b):
    return lax.dot_general(a.astype(BF16), b.astype(BF16), (((0,), (0,)), ((), ())),
                           preferred_element_type=F32)


def _mod_kernel(c_ref, w_ref, b_ref, o_ref):
    h = _silu(c_ref[...])
    o_ref[0] = _dot(h, w_ref[0]) + b_ref[0]


def _modulation(c_all, w_mod, b_mod, tn=1024):
    rows, d = c_all.shape
    depth, _, n = w_mod.shape
    return pl.pallas_call(
        _mod_kernel,
        grid=(depth, n // tn),
        in_specs=[pl.BlockSpec((rows, d), lambda l, j: (0, 0)),
                  pl.BlockSpec((1, d, tn), lambda l, j: (l, 0, j)),
                  pl.BlockSpec((1, 1, tn), lambda l, j: (l, 0, j))],
        out_specs=pl.BlockSpec((1, rows, tn), lambda l, j: (l, 0, j)),
        out_shape=jax.ShapeDtypeStruct((depth, rows, n), F32),
        compiler_params=_cp(("parallel", "arbitrary")),
        name="modulation",
    )(c_all, w_mod, b_mod.reshape(depth, 1, n))


class _Group:
    def __init__(self, row0, nrows, seq):
        assert row0 % SUBLANES == 0
        self.row0, self.nrows, self.seq = row0, nrows, seq
        self.block_rows = nrows if seq is None else SUBLANES
        assert seq is not None or nrows % SUBLANES == 0
        assert seq is None or nrows <= SUBLANES

    def spec(self, layer, col, d):
        rb = self.row0 // self.block_rows
        return pl.BlockSpec((1, self.block_rows, d), lambda i, j: (layer, rb, col))

    def rows(self, ref, tm):
        if self.seq is None:
            return ref[0]
        return ref[0, pl.ds((pl.program_id(0) * tm) // self.seq, 1), :]


def _gn_spec(layer, d):
    return pl.BlockSpec((1, 2 * N_SUB, d), lambda i, j: (layer, 0, 0))


def _ffn_kernel(*refs, nf, res_w, kpre, grp, tm, emit):
    x_ref, sh_ref, sc_ref, gt_ref, gn_ref, wg_ref, wu_ref, wd_ref, o_ref = refs[:9]
    h_ref = refs[-1]
    f = pl.program_id(1)
    def split(n):
        return tuple((k * tm // n, (k + 1) * tm // n) for k in range(n)) if tm % (256 * n) == 0 else ((0, tm),)

    def mod_rows(ref, lo, hi):
        rows = grp.rows(ref, tm)
        return rows if rows.shape[0] == 1 else rows[lo:hi]

    def step(first, last):
        if emit:
            wg, wu, wd = (r[0, 0].astype(BF16) for r in (wg_ref, wu_ref, wd_ref))
            for out_ref, tile in zip(refs[9:12], (wg, wu, wd)):
                out_ref[0] = tile
        else:
            wg, wu, wd = wg_ref[0], wu_ref[0], wd_ref[0]
        for lo, hi in split(2 if first or last else 1):
            if first:
                xn = _rms(x_ref[lo:hi, :], gn_ref[0, kpre:kpre + 1, :])
                h = (xn * (1 + mod_rows(sc_ref, lo, hi)) + mod_rows(sh_ref, lo, hi)).astype(BF16)
                h_ref[lo:hi, :] = h
            else:
                h = h_ref[lo:hi, :]
            g = jnp.dot(h, wg, preferred_element_type=F32)
            u = jnp.dot(h, wu, preferred_element_type=F32)
            y = jnp.dot((_silu(g) * u).astype(BF16), wd, preferred_element_type=F32)
            if not first:
                y = o_ref[lo:hi, :] + y
            if last:
                yn = _rms(y, gn_ref[0, kpre + 1:kpre + 2, :])
                y = x_ref[lo:hi, :] + (res_w * mod_rows(gt_ref, lo, hi)) * yn
            o_ref[lo:hi, :] = y

    if nf == 1:
        step(True, True)
    else:
        pl.when(f == 0)(lambda: step(True, False))
        pl.when(f == nf - 1)(lambda: step(False, True))
        if nf > 2:
            pl.when(jnp.logical_and(f > 0, f < nf - 1))(lambda: step(False, False))


def _ffn(x, mod, g_norm, weights, l, s, sub, grp, *, tm, tf, tiles=None):
    m, d = x.shape
    emit = tiles is None
    if emit:
        nf = weights[0].shape[-1] // tf
        w_specs = [pl.BlockSpec((1, 1, d, tf), lambda i, j: (l, s, 0, j)),
                   pl.BlockSpec((1, 1, d, tf), lambda i, j: (l, s, 0, j)),
                   pl.BlockSpec((1, 1, tf, d), lambda i, j: (l, s, j, 0))]
        w_args = weights
        x_kw = {}
    else:
        nf = tiles[0].shape[0]
        w_specs = [pl.BlockSpec((1, d, tf), lambda i, j: (j, 0, 0)),
                   pl.BlockSpec((1, d, tf), lambda i, j: (j, 0, 0)),
                   pl.BlockSpec((1, tf, d), lambda i, j: (j, 0, 0))]
        w_args = tiles
        x_kw = {}
    out_specs = [pl.BlockSpec((tm, d), lambda i, j: (i, 0))]
    out_shape = [jax.ShapeDtypeStruct((m, d), F32)]
    if emit:
        assert m == tm
        out_specs += [pl.BlockSpec((1, d, tf), lambda i, j: (j, 0, 0)),
                      pl.BlockSpec((1, d, tf), lambda i, j: (j, 0, 0)),
                      pl.BlockSpec((1, tf, d), lambda i, j: (j, 0, 0))]
        out_shape += [jax.ShapeDtypeStruct((nf, d, tf), BF16), jax.ShapeDtypeStruct((nf, d, tf), BF16),
                      jax.ShapeDtypeStruct((nf, tf, d), BF16)]
    kern = functools.partial(_ffn_kernel, nf=nf, res_w=FFN_RES, kpre=2 * sub, grp=grp, tm=tm, emit=emit)
    out = pl.pallas_call(
        kern,
        grid=(m // tm, nf),
        in_specs=[pl.BlockSpec((tm, d), lambda i, j: (i, 0), **x_kw),
                  grp.spec(l, 3 * sub + 0, d), grp.spec(l, 3 * sub + 1, d), grp.spec(l, 3 * sub + 2, d),
                  _gn_spec(l, d)] + w_specs,
        out_specs=out_specs,
        out_shape=out_shape,
        scratch_shapes=[pltpu.VMEM((tm, d), BF16)],
        compiler_params=_cp(("parallel", "arbitrary")),
        name="ffn_emit" if emit else "ffn",
    )(x, mod, mod, mod, g_norm, *w_args)
    return (out[0], tuple(out[1:])) if emit else out[0]


def _inproj_kernel(*refs, transposed, has_bias, has_dt, grp, tm):
    refs = list(refs)
    x_ref, sh_ref, sc_ref, gn_ref, w_ref = refs[:5]
    rest = refs[5:]
    b_ref = rest.pop(0) if has_bias else None
    wdt_ref = rest.pop(0) if has_dt else None
    o_ref = rest.pop(0)
    dt_ref = rest.pop(0) if has_dt else None
    wbo_ref, h_ref, wb_ref = rest
    i = pl.program_id(0)
    j = pl.program_id(1)

    @pl.when(j == 0)
    def _():
        xn = _rms(x_ref[...], gn_ref[0, 2:3, :])
        h = (xn * (1 + grp.rows(sc_ref, tm)) + grp.rows(sh_ref, tm)).astype(BF16)
        h_ref[...] = h
        if has_dt:
            dt_ref[...] = _dot_nt(h, wdt_ref[0])

    @pl.when(i == 0)
    def _():
        wf = w_ref[0]
        wb_ref[j] = wf.astype(BF16)
        wbo_ref[...] = (wf.T if transposed else wf).astype(BF16)

    w = wb_ref[j]
    if transposed:
        o = lax.dot_general(h_ref[...], w, (((1,), (1,)), ((), ())), preferred_element_type=F32)
    else:
        o = jnp.dot(h_ref[...], w, preferred_element_type=F32)
    o_ref[...] = o + b_ref[...] if has_bias else o


def _inproj(x, mod, g_norm, w, b, l, layer, n, grp, *, tm, tn, transposed, n_dt=0):
    m, d = x.shape
    assert m == tm
    nj = n // tn
    has_bias = b is not None
    has_dt = n_dt > 0
    wcol = lambda i, j: jnp.where(i == 0, j, nj - 1)
    if transposed:
        w_spec = pl.BlockSpec((1, tn, d), lambda i, j: (layer, wcol(i, j), 0))
        wb_shape = (nj, tn, d)
    else:
        w_spec = pl.BlockSpec((1, d, tn), lambda i, j: (layer, 0, wcol(i, j)))
        wb_shape = (nj, d, tn)
    in_specs = [pl.BlockSpec((tm, d), lambda i, j: (i, 0)),
                grp.spec(l, 3, d), grp.spec(l, 4, d), _gn_spec(l, d), w_spec]
    args = [x, mod, mod, g_norm, w]
    if has_bias:
        in_specs.append(pl.BlockSpec((1, tn), lambda i, j: (0, j)))
        args.append(b)
    out_specs = [pl.BlockSpec((tm, tn), lambda i, j: (i, j))]
    out_shape = [jax.ShapeDtypeStruct((m, n), F32)]
    if has_dt:
        assert transposed and n % n_dt == 0
        in_specs.append(pl.BlockSpec((1, n_dt, d), lambda i, j: (layer, n // n_dt, 0)))
        args.append(w)
        out_specs.append(pl.BlockSpec((tm, n_dt), lambda i, j: (i, 0)))
        out_shape.append(jax.ShapeDtypeStruct((m, n_dt), F32))
    out_specs.append(pl.BlockSpec((d, tn), lambda i, j: (0, j)))
    out_shape.append(jax.ShapeDtypeStruct((d, n), BF16))
    return pl.pallas_call(
        functools.partial(_inproj_kernel, transposed=transposed, has_bias=has_bias, has_dt=has_dt,
                          grp=grp, tm=tm),
        grid=(m // tm, nj),
        in_specs=in_specs,
        out_specs=out_specs,
        out_shape=out_shape,
        scratch_shapes=[pltpu.VMEM((tm, d), BF16), pltpu.VMEM(wb_shape, BF16)],
        compiler_params=_cp(("arbitrary", "arbitrary")),
        name="inproj_emit",
    )(*args)


def _inproj_res_kernel(*refs, n, chunk, has_bias, has_dt, n_alias, grp, tm):
    refs = list(refs)
    x_ref, sh_ref, sc_ref, gn_ref, wb_ref = refs[:5]
    rest = refs[5:]
    b_ref = rest.pop(0) if has_bias else None
    wdt_ref = rest.pop(0) if has_dt else None
    del rest[:n_alias]
    o_ref = rest.pop(0)
    scale, shift = grp.rows(sc_ref, tm), grp.rows(sh_ref, tm)
    assert scale.shape[0] == 1
    half = tm // 2
    for lo in (0, half):
        xn = _rms(x_ref[lo:lo + half, :], gn_ref[0, 2:3, :])
        h = (xn * (1 + scale) + shift).astype(BF16)
        if has_dt:
            rest[0][lo:lo + half, :] = _dot_nt(h, wdt_ref[0])
        for c in range(n // chunk):
            cols = slice(c * chunk, (c + 1) * chunk)
            o = jnp.dot(h, wb_ref[:, cols], preferred_element_type=F32)
            o_ref[lo:lo + half, cols] = o + b_ref[:, cols] if has_bias else o
            if has_dt and c * chunk in (DA_Q, DA_Q + DA_KVD):
                kv_ref = rest[1] if c * chunk == DA_Q else rest[2]
                for g in range(DA_KV):
                    kv_ref[0, pl.ds(lo * DA_KV + g, half, stride=DA_KV), :] = o[:, g * 2 * DA_D:(g + 1) * 2 * DA_D]


def _inproj_res(x, mod, g_norm, wb, b, w_dt, l, layer, grp, *, tm, n_dt=0, chunk=512, kv_stack=None):
    assert chunk == DA_KVD
    m, d = x.shape
    n = wb.shape[1]
    has_bias = b is not None
    has_dt = n_dt > 0
    once = dict(pipeline_mode=pl.Buffered(1))
    in_specs = [pl.BlockSpec((tm, d), lambda i, j: (i, 0)),
                grp.spec(l, 3, d), grp.spec(l, 4, d), _gn_spec(l, d),
                pl.BlockSpec((d, n), lambda i, j: (0, 0), **once)]
    args = [x, mod, mod, g_norm, wb]
    if has_bias:
        in_specs.append(pl.BlockSpec((1, n), lambda i, j: (0, 0)))
        args.append(b)
    out_specs = [pl.BlockSpec((tm, n), lambda i, j: (i, 0))]
    out_shape = [jax.ShapeDtypeStruct((m, n), F32)]
    if has_dt:
        in_specs.append(pl.BlockSpec((1, n_dt, d), lambda i, j: (layer, n // n_dt, 0)))
        args.append(w_dt)
        out_specs.append(pl.BlockSpec((tm, n_dt), lambda i, j: (i, 0)))
        out_shape.append(jax.ShapeDtypeStruct((m, n_dt), F32))
        n_layers = w_dt.shape[0]
        for _ in range(2):
            out_specs.append(pl.BlockSpec((1, tm * DA_KV, 2 * DA_D), lambda i, j: (layer, i, 0)))
            out_shape.append(jax.ShapeDtypeStruct((n_layers, m * DA_KV, 2 * DA_D), F32))
    aliases = {}
    if kv_stack is not None:
        for t, stack in enumerate(kv_stack):
            aliases[len(args)] = 2 + t
            in_specs.append(pl.BlockSpec(memory_space=pl.ANY))
            args.append(stack)
    out = pl.pallas_call(
        functools.partial(_inproj_res_kernel, n=n, chunk=chunk, has_bias=has_bias, has_dt=has_dt,
                          n_alias=len(aliases), grp=grp, tm=tm),
        grid=(m // tm, 1),
        in_specs=in_specs,
        out_specs=out_specs,
        out_shape=out_shape,
        input_output_aliases=aliases,
        compiler_params=_cp(("parallel", "arbitrary")),
        name="inproj",
    )(*args)
    return out if has_dt else out[0]


def _outproj_kernel(*refs, bounds, has_bias, grp, tm):
    refs = list(refs)
    x_ref = refs.pop(0)
    y_refs = [refs.pop(0) for _ in bounds]
    gt_ref, gn_ref, w_ref = refs[:3]
    rest = refs[3:]
    b_ref = rest.pop(0) if has_bias else None
    o_ref, wbo_ref, wb_ref = rest
    nk = bounds[-1][1]
    i = pl.program_id(0)
    k = pl.program_id(1)

    @pl.when(i == 0)
    def _():
        wb = w_ref[0].astype(BF16)
        wb_ref[k] = wb
        wbo_ref[...] = wb

    @pl.when(k == 0)
    def _():
        o_ref[...] = jnp.zeros(o_ref.shape, F32)

    for y_ref, (lo, hi) in zip(y_refs, bounds):
        @pl.when(jnp.logical_and(k >= lo, k < hi))
        def _(y_ref=y_ref):
            o_ref[...] += jnp.dot(y_ref[...].astype(BF16), wb_ref[k], preferred_element_type=F32)

    @pl.when(k == nk - 1)
    def _():
        y = o_ref[...] + b_ref[...] if has_bias else o_ref[...]
        o_ref[...] = x_ref[...] + grp.rows(gt_ref, tm) * _rms(y, gn_ref[0, 3:4, :])


def _outproj(x, ys, mod, g_norm, w, b, l, layer, grp, *, tm, tk):
    m, d = x.shape
    has_bias = b is not None
    bounds, lo = [], 0
    for y in ys:
        bounds.append((lo, lo + y.shape[1] // tk))
        lo = bounds[-1][1]
    nk = lo
    assert nk * tk == w.shape[1]

    def y_spec(lo, hi):
        return pl.BlockSpec((tm, tk), lambda i, k: (i, jnp.clip(k - lo, 0, hi - lo - 1)))

    in_specs = ([pl.BlockSpec((tm, d), lambda i, k: (i, 0))]
                + [y_spec(lo, hi) for lo, hi in bounds]
                + [grp.spec(l, 5, d), _gn_spec(l, d),
                   pl.BlockSpec((1, tk, d), lambda i, k: (layer, jnp.where(i == 0, k, nk - 1), 0))])
    args = [x, *ys, mod, g_norm, w]
    if has_bias:
        in_specs.append(pl.BlockSpec((1, d), lambda i, k: (0, 0)))
        args.append(b)
    assert m == tm
    return pl.pallas_call(
        functools.partial(_outproj_kernel, bounds=tuple(bounds), has_bias=has_bias, grp=grp, tm=tm),
        grid=(m // tm, nk),
        in_specs=in_specs,
        out_specs=[pl.BlockSpec((tm, d), lambda i, k: (i, 0)), pl.BlockSpec((tk, d), lambda i, k: (k, 0))],
        out_shape=[jax.ShapeDtypeStruct((m, d), F32), jax.ShapeDtypeStruct((nk * tk, d), BF16)],
        scratch_shapes=[pltpu.VMEM((nk, tk, d), BF16)],
        compiler_params=_cp(("arbitrary", "arbitrary")),
        name="outproj_emit",
    )(*args)


def _outproj_res_kernel(*refs, n_y, has_bias, grp, tm):
    x_ref = refs[0]
    y_refs = refs[1:1 + n_y]
    gt_ref, gn_ref, wb_ref = refs[1 + n_y:4 + n_y]
    b_ref = refs[4 + n_y] if has_bias else None
    o_ref = refs[-1]
    gate = grp.rows(gt_ref, tm)
    assert gate.shape[0] == 1
    half = tm // 2
    for lo in (0, half):
        rows = slice(lo, lo + half)
        ycat = jnp.concatenate([r[rows, :].astype(BF16) for r in y_refs], axis=-1)
        y = jnp.dot(ycat, wb_ref[...], preferred_element_type=F32)
        if has_bias:
            y = y + b_ref[...]
        o_ref[rows, :] = x_ref[rows, :] + gate * _rms(y, gn_ref[0, 3:4, :])


def _outproj_res(x, ys, mod, g_norm, wb, b, l, grp, *, tm):
    m, d = x.shape
    has_bias = b is not None
    in_specs = ([pl.BlockSpec((tm, d), lambda i, k: (i, 0))]
                + [pl.BlockSpec((tm, y.shape[1]), lambda i, k: (i, 0)) for y in ys]
                + [grp.spec(l, 5, d), _gn_spec(l, d),
                   pl.BlockSpec(wb.shape, lambda i, k: (0, 0), pipeline_mode=pl.Buffered(1))])
    args = [x, *ys, mod, g_norm, wb]
    if has_bias:
        in_specs.append(pl.BlockSpec((1, d), lambda i, k: (0, 0)))
        args.append(b)
    return pl.pallas_call(
        functools.partial(_outproj_res_kernel, n_y=len(ys), has_bias=has_bias, grp=grp, tm=tm),
        grid=(m // tm, 1),
        in_specs=in_specs,
        out_specs=pl.BlockSpec((tm, d), lambda i, k: (i, 0)),
        out_shape=jax.ShapeDtypeStruct((m, d), F32),
        compiler_params=_cp(("parallel", "arbitrary")),
        name="outproj",
    )(*args)


def _lambda(lp, lam_init):
    a = jnp.sum(lp[0:1] * lp[1:2], axis=-1, keepdims=True)
    b = jnp.sum(lp[2:3] * lp[3:4], axis=-1, keepdims=True)
    return jnp.exp(a) - jnp.exp(b) + lam_init


def _da_prompt_kernel(q_ref, k_ref, v_ref, lp_ref, gs_ref, o_ref, *, tq, lam_init):
    i = pl.program_id(2)
    lam = _lambda(lp_ref[...], lam_init)
    q = q_ref[...] * (DA_D ** -0.5 * LOG2E)
    low = lax.broadcasted_iota(jnp.int32, (tq, 2 * DA_D), 1) < DA_D
    heads = (q[:, :2 * DA_D], q[:, 2 * DA_D:])
    qp = (jnp.concatenate([jnp.where(low, h, 0.0) for h in heads], axis=0).astype(BF16),
          jnp.concatenate([jnp.where(low, 0.0, h) for h in heads], axis=0).astype(BF16))
    key = lax.broadcasted_iota(jnp.int32, (tq, 2 * tq), 0)
    col = lax.broadcasted_iota(jnp.int32, (tq, 2 * tq), 1)
    causal_bias = jnp.where(key <= jnp.where(col >= tq, col - tq, col), 0.0, NEG)

    def step(j, carry, diagonal):
        off = pl.multiple_of(j * tq, tq)
        kb = k_ref[pl.ds(off, tq), :].astype(BF16)
        vb = v_ref[pl.ds(off, tq), :].astype(BF16)
        out = []
        for mi in range(2):
            m_old, l_old, acc = carry[3 * mi:3 * mi + 3]
            s = lax.dot_general(kb, qp[mi], (((1,), (1,)), ((), ())), preferred_element_type=F32)
            if diagonal:
                s = s + causal_bias
            m_new = jnp.maximum(m_old, jnp.max(s, axis=0, keepdims=True))
            alpha = jnp.exp2(m_old - m_new)
            p = jnp.exp2(s - m_new)
            l_new = alpha * l_old + jnp.sum(p, axis=0, keepdims=True)
            acc = alpha * acc + lax.dot_general(vb, p.astype(BF16), (((0,), (0,)), ((), ())),
                                                preferred_element_type=F32)
            out += [m_new, l_new, acc]
        return tuple(out)

    m0 = jnp.full((1, 2 * tq), NEG, F32)
    l0 = jnp.zeros((1, 2 * tq), F32)
    a0 = jnp.zeros((2 * DA_D, 2 * tq), F32)
    res = lax.fori_loop(0, i, lambda j, c: step(j, c, False), (m0, l0, a0, m0, l0, a0))
    res = step(i, res, True)
    a = res[2] / res[1] - lam * (res[5] / res[4])
    a = a * lax.rsqrt(jnp.mean(a * a, axis=0, keepdims=True) + EPS) * gs_ref[...] * (1 - lam_init)
    a = a.T
    o_ref[:, 0:128] = a[:tq]
    o_ref[:, 128:256] = a[tq:]


def _da_prompt(proj, lp, gs, nseq, seq, lam_init, tq=512):
    m = proj.shape[0]
    tq = min(tq, seq)
    nq = seq // tq
    kcol = DA_Q // 128
    vcol = (DA_Q + DA_KVD) // 128
    return pl.pallas_call(
        functools.partial(_da_prompt_kernel, tq=tq, lam_init=lam_init),
        grid=(nseq, DA_KV, nq),
        in_specs=[pl.BlockSpec((tq, 256), lambda b, g, i: (b * nq + i, g)),
                  pl.BlockSpec((seq, 128), lambda b, g, i: (b, kcol + g)),
                  pl.BlockSpec((seq, 128), lambda b, g, i: (b, vcol + g)),
                  pl.BlockSpec((4, DA_D), lambda b, g, i: (0, 0)),
                  pl.BlockSpec((2 * DA_D, 1), lambda b, g, i: (0, 0))],
        out_specs=pl.BlockSpec((tq, 256), lambda b, g, i: (b * nq + i, g)),
        out_shape=jax.ShapeDtypeStruct((m, DA_Q), F32),
        compiler_params=_cp(("parallel", "parallel", "arbitrary")),
        name="diff_attn_prompt",
    )(proj, proj, proj, lp, gs.reshape(-1, 1))


def _da_decode_kernel(pt_ref, qm_ref, kn_ref, vn_ref, lp_ref, gs_ref, *refs, pc, nc, lam_init):
    del pt_ref
    k_refs = refs[:pc]
    v_refs = refs[pc:2 * pc]
    o_ref, m_ref, l_ref, acc_ref, kb_ref, vb_ref = refs[2 * pc:]
    c = pl.program_id(1)

    @pl.when(c == 0)
    def _():
        m_ref[...] = jnp.full(m_ref.shape, NEG, F32)
        l_ref[...] = jnp.zeros(l_ref.shape, F32)
        acc_ref[...] = jnp.zeros(acc_ref.shape, F32)

    rows = PAGE * DA_KV
    for t in range(pc):
        kb_ref[t * rows:(t + 1) * rows, :] = k_refs[t][0, 0].astype(BF16)
        vb_ref[t * rows:(t + 1) * rows, :] = v_refs[t][0, 0].astype(BF16)
    qm = qm_ref[0] * (DA_D ** -0.5)
    s = _dot_nt(qm, kb_ref[...])
    row_g = (lax.broadcasted_iota(jnp.int32, s.shape, 0) // DA_REP) & (DA_KV - 1)
    col_g = lax.broadcasted_iota(jnp.int32, s.shape, 1) & (DA_KV - 1)
    s = jnp.where(row_g == col_g, s, NEG)
    m_old = m_ref[...]
    m_new = jnp.maximum(m_old, jnp.max(s, axis=-1, keepdims=True))
    alpha = jnp.exp(m_old - m_new)
    p = jnp.exp(s - m_new)
    l_ref[...] = alpha * l_ref[...] + jnp.sum(p, axis=-1, keepdims=True)
    acc_ref[...] = alpha * acc_ref[...] + jnp.dot(p.astype(BF16), vb_ref[...], preferred_element_type=F32)
    m_ref[...] = m_new

    @pl.when(c == nc - 1)
    def _():
        lam = _lambda(lp_ref[...], lam_init)
        s_new = jnp.sum(qm * kn_ref[0], axis=-1, keepdims=True)
        m_old = m_ref[...]
        m_fin = jnp.maximum(m_old, s_new)
        alpha = jnp.exp(m_old - m_fin)
        p_new = jnp.exp(s_new - m_fin)
        l_fin = alpha * l_ref[...] + p_new
        acc = alpha * acc_ref[...] + p_new * vn_ref[0]
        o = acc / l_fin
        a = o[0:8] - lam * o[8:16]
        o_ref[0] = _rms(a, gs_ref[...]) * (1 - lam_init)


def _da_decode(q, k_new, v_new, lp, gs, cache_k, cache_v, page_table, j, lam_init, pc=32):
    nb = q.shape[0]
    n_pages = page_table.shape[1]
    pc = min(pc, n_pages)
    nc = n_pages // pc
    rows = PAGE * DA_KV
    ck = cache_k.reshape(cache_k.shape[0], cache_k.shape[1], rows, 2 * DA_D)
    cv = cache_v.reshape(cache_v.shape[0], cache_v.shape[1], rows, 2 * DA_D)
    qt = q.reshape(nb, DA_KV, DA_REP, 2, DA_D).transpose(0, 3, 1, 2, 4)
    own_m = jnp.eye(2, dtype=bool)[None, :, None, None, :, None]
    qm = jnp.where(own_m, qt[:, :, :, :, None, :], 0.0).reshape(nb, 16, 2 * DA_D)
    rep = lambda a: jnp.broadcast_to(a.reshape(nb, 1, DA_KV, 1, 2 * DA_D),
                                     (nb, 2, DA_KV, DA_REP, 2 * DA_D)).reshape(nb, 16, 2 * DA_D)

    def page_spec(t):
        return pl.BlockSpec((1, 1, rows, 2 * DA_D), lambda b, c, pt: (j, pt[b, c * pc + t], 0, 0))

    row_spec = pl.BlockSpec((1, 16, 2 * DA_D), lambda b, c, pt: (b, 0, 0))
    grid_spec = pltpu.PrefetchScalarGridSpec(
        num_scalar_prefetch=1,
        grid=(nb, nc),
        in_specs=[row_spec, row_spec, row_spec,
                  pl.BlockSpec((4, DA_D), lambda b, c, pt: (0, 0)),
                  pl.BlockSpec((1, 2 * DA_D), lambda b, c, pt: (0, 0))]
                 + [page_spec(t) for t in range(pc)] + [page_spec(t) for t in range(pc)],
        out_specs=pl.BlockSpec((1, 8, 128), lambda b, c, pt: (b, 0, 0)),
        scratch_shapes=[pltpu.VMEM((16, 1), F32), pltpu.VMEM((16, 1), F32),
                        pltpu.VMEM((16, 2 * DA_D), F32),
                        pltpu.VMEM((pc * rows, 2 * DA_D), BF16), pltpu.VMEM((pc * rows, 2 * DA_D), BF16)])
    out = pl.pallas_call(
        functools.partial(_da_decode_kernel, pc=pc, nc=nc, lam_init=lam_init),
        grid_spec=grid_spec,
        out_shape=jax.ShapeDtypeStruct((nb, 8, 128), F32),
        compiler_params=_cp(("parallel", "arbitrary")),
        name="diff_attn_decode",
    )(page_table, qm, rep(k_new), rep(v_new), lp, gs.reshape(1, -1), *([ck] * pc), *([cv] * pc))
    return out.reshape(nb, DA_Q)


def _split3(x):
    x1 = x.astype(BF16)
    r1 = x - x1.astype(F32)
    x2 = r1.astype(BF16)
    x3 = (r1 - x2.astype(F32)).astype(BF16)
    return x1, x2, x3


def _group_norm_gate(y, z, gn):
    yz = y * _silu(z)
    w = SSM_INNER // SSM_G
    parts = []
    for g in range(SSM_G):
        seg = yz[:, g * w:(g + 1) * w]
        parts.append(seg * lax.rsqrt(jnp.mean(seg * seg, axis=-1, keepdims=True) + EPS))
    return jnp.concatenate(parts, axis=-1) * gn


def _ssd_prompt_kernel(xbc_ref, z_ref, dt_ref, dtt_ref, cw_ref, cb_ref, dtb_ref, dtbt_ref,
                       alog_ref, alogt_ref, dskipt_ref, gn_ref, y_ref, st_ref, ext_ref, h_ref, *, nchunk):
    c = pl.program_id(1)
    q = CHUNK

    @pl.when(c == 0)
    def _():
        ext_ref[0:8, :] = jnp.zeros((8, CONV_DIM), F32)
        h_ref[...] = jnp.zeros(h_ref.shape, F32)

    @pl.when(c > 0)
    def _():
        ext_ref[0:8, :] = ext_ref[q:q + 8, :]

    ext_ref[8:q + 8, :] = xbc_ref[...]
    cw = cw_ref[...]
    acc = ext_ref[5:q + 5, :] * cw[0:1]
    for kk in range(1, CONV_W):
        acc = acc + ext_ref[5 + kk:q + 5 + kk, :] * cw[kk:kk + 1]
    xa = _silu(acc + cb_ref[...])
    xs = xa[:, :SSM_INNER]
    bm = xa[:, SSM_INNER:SSM_INNER + SSM_G * SSM_N].astype(BF16)
    cm = xa[:, SSM_INNER + SSM_G * SSM_N:].astype(BF16)

    dt = _softplus(dt_ref[...] + dtb_ref[...])
    dtt = _softplus(dtt_ref[...] + dtbt_ref[...])
    a = dt * (-jnp.exp(alog_ref[...]))
    at = dtt * (-jnp.exp(alogt_ref[...]))
    ri = lax.broadcasted_iota(jnp.int32, (q, q), 0)
    ci = lax.broadcasted_iota(jnp.int32, (q, q), 1)
    lower = ri >= ci
    tril = jnp.where(lower, 1.0, 0.0).astype(BF16)
    triu = jnp.where(ri <= ci, 1.0, 0.0).astype(BF16)
    cs = sum(jnp.dot(tril, part, preferred_element_type=F32) for part in _split3(a))
    cst = sum(jnp.dot(part, triu, preferred_element_type=F32) for part in _split3(at))
    tot = cst[:, q - 1:q]
    e_cst = jnp.exp(cst)
    e_dect = jnp.exp(tot - cst)
    e_tot = jnp.exp(tot)
    dskip = dskipt_ref[...]
    xst = xs.T
    upper = ri <= ci

    yts = []
    for g in range(SSM_G):
        bg = bm[:, g * SSM_N:(g + 1) * SSM_N]
        cg = cm[:, g * SSM_N:(g + 1) * SSM_N]
        cbt = lax.dot_general(bg, cg, (((1,), (1,)), ((), ())), preferred_element_type=F32)
        for r in range(SSM_H // SSM_G):
            h = g * (SSM_H // SSM_G) + r
            lmat_t = jnp.exp(jnp.where(upper, cst[h:h + 1, :] - cs[:, h:h + 1], NEG))
            xt_h = xst[h * SSM_P:(h + 1) * SSM_P, :]
            xdt_t = xt_h * dtt[h:h + 1, :]
            y_diag = _dot(xdt_t, cbt * lmat_t)
            hp = h_ref[h]
            y_off = lax.dot_general(hp.astype(BF16), cg, (((1,), (1,)), ((), ())),
                                    preferred_element_type=F32) * e_cst[h:h + 1, :]
            upd = _dot(xdt_t * e_dect[h:h + 1, :], bg)
            h_ref[h] = e_tot[h:h + 1, :] * hp + upd
            yts.append(y_diag + y_off + dskip[h:h + 1, :] * xt_h)
    y = jnp.concatenate(yts, axis=0).T
    y_ref[...] = _group_norm_gate(y, z_ref[...], gn_ref[...])

    @pl.when(c == nchunk - 1)
    def _():
        st_ref[0] = h_ref[...]


def _ssd_prompt(proj, dt_raw, cw, cb, dtb, alog, dskip, gn, nseq, seq):
    m = proj.shape[0]
    nchunk = seq // CHUNK
    zcol = (DA_Q + 2 * DA_KVD) // SSM_INNER
    xcol = (DA_Q + 2 * DA_KVD + SSM_INNER) // CONV_DIM
    const = lambda b, c: (0, 0)
    y, st = pl.pallas_call(
        functools.partial(_ssd_prompt_kernel, nchunk=nchunk),
        grid=(nseq, nchunk),
        in_specs=[pl.BlockSpec((CHUNK, CONV_DIM), lambda b, c: (b * nchunk + c, xcol)),
                  pl.BlockSpec((CHUNK, SSM_INNER), lambda b, c: (b * nchunk + c, zcol)),
                  pl.BlockSpec((CHUNK, SSM_H), lambda b, c: (b * nchunk + c, 0)),
                  pl.BlockSpec((SSM_H, CHUNK), lambda b, c: (0, b * nchunk + c)),
                  pl.BlockSpec((CONV_W, CONV_DIM), const),
                  pl.BlockSpec((1, CONV_DIM), const),
                  pl.BlockSpec((1, SSM_H), const),
                  pl.BlockSpec((SSM_H, 1), const),
                  pl.BlockSpec((1, SSM_H), const),
                  pl.BlockSpec((SSM_H, 1), const),
                  pl.BlockSpec((SSM_H, 1), const),
                  pl.BlockSpec((1, SSM_INNER), const)],
        out_specs=[pl.BlockSpec((CHUNK, SSM_INNER), lambda b, c: (b * nchunk + c, 0)),
                   pl.BlockSpec((1, SSM_H, SSM_P, SSM_N), lambda b, c: (b, 0, 0, 0))],
        out_shape=[jax.ShapeDtypeStruct((m, SSM_INNER), F32),
                   jax.ShapeDtypeStruct((nseq, SSM_H, SSM_P, SSM_N), F32)],
        scratch_shapes=[pltpu.VMEM((CHUNK + 8, CONV_DIM), F32),
                        pltpu.VMEM((SSM_H, SSM_P, SSM_N), F32)],
        compiler_params=_cp(("parallel", "arbitrary")),
        name="ssd_prompt",
    )(proj, proj, dt_raw, dt_raw.T, cw, cb.reshape(1, -1), dtb.reshape(1, -1), dtb.reshape(-1, 1),
      alog.reshape(1, -1), alog.reshape(-1, 1), dskip.reshape(-1, 1), gn.reshape(1, -1))
    return y, st


def _ssd_dec_pre_kernel(xbc_ref, cst_ref, dt_ref, cw_ref, cb_ref, dtb_ref, xa_ref, dts_ref):
    cw = cw_ref[...]
    acc = cst_ref[0] * cw[0:1]
    for kk in range(1, CONV_W - 1):
        acc = acc + cst_ref[kk] * cw[kk:kk + 1]
    acc = acc + xbc_ref[...] * cw[CONV_W - 1:CONV_W]
    xa_ref[...] = _silu(acc + cb_ref[...])
    dts_ref[...] = _softplus(dt_ref[...] + dtb_ref[...])


def _ssd_dec_state_kernel(xt_ref, dts_ref, alog_ref, b_ref, c_ref, dskip_ref, h_ref, *rest):
    yt_ref, hn_ref = rest[-2:]
    dskip = dskip_ref[...]
    for s in range(xt_ref.shape[0]):
        xt = xt_ref[s]
        dts = dts_ref[s]
        a = dts * (-jnp.exp(alog_ref[...]))
        e_a = jnp.exp(a)
        cols = []
        for h in range(SSM_H):
            g = h // (SSM_H // SSM_G)
            xcol = xt[:, h:h + 1]
            brow = b_ref[s][:, g * SSM_N:(g + 1) * SSM_N]
            crow = c_ref[s][:, g * SSM_N:(g + 1) * SSM_N]
            hn = e_a[:, h:h + 1] * h_ref[0, s, h] + (xcol * dts[:, h:h + 1]) * brow
            hn_ref[0, s, h] = hn
            cols.append(jnp.sum(hn * crow, axis=-1, keepdims=True) + dskip[:, h:h + 1] * xcol)
        yt_ref[s] = jnp.concatenate(cols, axis=-1)


def _ssd_dec_post_kernel(y_ref, z_ref, gn_ref, o_ref):
    o_ref[...] = _group_norm_gate(y_ref[...], z_ref[...], gn_ref[...])


def _ssd_decode(proj, dt_raw, conv_state, ssm_state, state_stack, layer, cw, cb, dtb, alog, dskip, gn):
    nb = proj.shape[0]
    zcol = (DA_Q + 2 * DA_KVD) // SSM_INNER
    xcol = (DA_Q + 2 * DA_KVD + SSM_INNER) // CONV_DIM
    full = lambda shape: pl.BlockSpec(shape, lambda i: (0,) * len(shape))
    xa, dts = pl.pallas_call(
        _ssd_dec_pre_kernel,
        grid=(1,),
        in_specs=[pl.BlockSpec((nb, CONV_DIM), lambda i: (0, xcol)),
                  full((CONV_W - 1, nb, CONV_DIM)),
                  full((nb, SSM_H)),
                  full((CONV_W, CONV_DIM)),
                  full((1, CONV_DIM)),
                  full((1, SSM_H))],
        out_specs=[full((nb, CONV_DIM)), full((nb, SSM_H))],
        out_shape=[jax.ShapeDtypeStruct((nb, CONV_DIM), F32), jax.ShapeDtypeStruct((nb, SSM_H), F32)],
        compiler_params=_cp(("arbitrary",)),
        name="ssd_decode_pre",
    )(proj, conv_state.transpose(1, 0, 2), dt_raw, cw, cb.reshape(1, -1), dtb.reshape(1, -1))
    sb = 1
    xt = xa[:, :SSM_INNER].reshape(nb, SSM_H, SSM_P).transpose(0, 2, 1)
    bmat = xa[:, SSM_INNER:SSM_INNER + SSM_G * SSM_N].reshape(nb, 1, SSM_G * SSM_N)
    cmat = xa[:, SSM_INNER + SSM_G * SSM_N:].reshape(nb, 1, SSM_G * SSM_N)
    state_block = pl.BlockSpec((1, sb, SSM_H, SSM_P, SSM_N), lambda b: (layer, b, 0, 0, 0))
    in_specs = [pl.BlockSpec((sb, SSM_P, SSM_H), lambda b: (b, 0, 0)),
                pl.BlockSpec((sb, 1, SSM_H), lambda b: (b, 0, 0)),
                pl.BlockSpec((1, SSM_H), lambda b: (0, 0)),
                pl.BlockSpec((sb, 1, SSM_G * SSM_N), lambda b: (b, 0, 0)),
                pl.BlockSpec((sb, 1, SSM_G * SSM_N), lambda b: (b, 0, 0)),
                pl.BlockSpec((1, SSM_H), lambda b: (0, 0)),
                state_block]
    args = [xt, dts.reshape(nb, 1, SSM_H), alog.reshape(1, -1), bmat, cmat, dskip.reshape(1, -1), ssm_state]
    aliases = {}
    if state_stack is not None:
        aliases[len(args)] = 1
        in_specs.append(pl.BlockSpec(memory_space=pl.ANY))
        args.append(state_stack)
    yt, h_new = pl.pallas_call(
        _ssd_dec_state_kernel,
        grid=(nb // sb,),
        in_specs=in_specs,
        out_specs=[pl.BlockSpec((sb, SSM_P, SSM_H), lambda b: (b, 0, 0)), state_block],
        out_shape=[jax.ShapeDtypeStruct((nb, SSM_P, SSM_H), F32),
                   jax.ShapeDtypeStruct(ssm_state.shape, F32)],
        input_output_aliases=aliases,
        compiler_params=_cp(("parallel",)),
        name="ssd_decode_state",
    )(*args)
    y = yt.transpose(0, 2, 1).reshape(nb, SSM_INNER)
    ob = pl.pallas_call(
        _ssd_dec_post_kernel,
        grid=(1,),
        in_specs=[full((nb, SSM_INNER)),
                  pl.BlockSpec((nb, SSM_INNER), lambda i: (0, zcol)),
                  full((1, SSM_INNER))],
        out_specs=full((nb, SSM_INNER)),
        out_shape=jax.ShapeDtypeStruct((nb, SSM_INNER), F32),
        compiler_params=_cp(("arbitrary",)),
        name="ssd_decode_post",
    )(y, proj, gn.reshape(1, -1))
    return ob, h_new


def _swa_prompt_kernel(sink_ref, q_ref, kp_ref, kc_ref, vp_ref, vc_ref, o_ref):
    i = pl.program_id(1)
    w = WINDOW
    pair_w = 2 * SW_D
    sinks = sink_ref[...] * LOG2E
    q = q_ref[...] * (SW_D ** -0.5 * LOG2E)
    kk = jnp.concatenate([kp_ref[...], kc_ref[...]], axis=0)
    vv = jnp.concatenate([vp_ref[...], vc_ref[...]], axis=0).astype(BF16)
    low_k = lax.broadcasted_iota(jnp.int32, (2 * w, pair_w), 1) < SW_D
    low_q = lax.broadcasted_iota(jnp.int32, (w, pair_w), 1) < SW_D
    key = lax.broadcasted_iota(jnp.int32, (2 * w, w), 0)
    t = key - lax.broadcasted_iota(jnp.int32, (2 * w, w), 1)
    keymin = jnp.where(i > 0, 0, w)
    bias1 = jnp.where(t >= 1, jnp.where(t <= w, jnp.where(key >= keymin, 0.0, NEG), NEG), NEG)
    bias = jnp.concatenate([bias1] * SW_REP, axis=1)
    outs = []
    for g in range(SW_KV):
        pair = g // 2
        k2 = kk[:, pair * pair_w:(pair + 1) * pair_w]
        k2r = pltpu.roll(k2, SW_D, axis=1)
        kdup = (jnp.where(low_k, k2, k2r) if g % 2 == 0 else jnp.where(low_k, k2r, k2)).astype(BF16)
        qs = []
        for r in range(SW_REP):
            h = g * SW_REP + r
            q2 = q[:, (h // 2) * pair_w:(h // 2 + 1) * pair_w]
            qs.append(jnp.where(low_q, q2, 0.0) if h % 2 == 0 else jnp.where(low_q, 0.0, q2))
        qst = jnp.concatenate(qs, axis=0).astype(BF16)
        s = lax.dot_general(kdup, qst, (((1,), (1,)), ((), ())), preferred_element_type=F32) + bias
        sink = jnp.concatenate([jnp.broadcast_to(sinks[:, g * SW_REP + r:g * SW_REP + r + 1], (1, w))
                                for r in range(SW_REP)], axis=1)
        mx = jnp.maximum(jnp.max(s, axis=0, keepdims=True), sink)
        p = jnp.exp2(s - mx)
        den = jnp.sum(p, axis=0, keepdims=True) + jnp.exp2(sink - mx)
        ot = lax.dot_general(vv[:, pair * pair_w:(pair + 1) * pair_w], p.astype(BF16),
                             (((0,), (0,)), ((), ())), preferred_element_type=F32)
        ot = ot[(g % 2) * SW_D:(g % 2 + 1) * SW_D, :] / den
        for pr in range(SW_REP // 2):
            blk = jnp.concatenate([ot[:, (2 * pr) * w:(2 * pr + 1) * w],
                                   ot[:, (2 * pr + 1) * w:(2 * pr + 2) * w]], axis=0)
            outs.append(blk.T)
    o_ref[...] = jnp.concatenate(outs, axis=1)


def _swa_prompt(proj, sinks, nseq, seq):
    m = proj.shape[0]
    nb = seq // WINDOW
    kcol = SW_Q // SW_KVD
    vcol = kcol + 1
    prev = lambda b, i: (b * nb + jnp.maximum(i - 1, 0), kcol)
    prev_v = lambda b, i: (b * nb + jnp.maximum(i - 1, 0), vcol)
    return pl.pallas_call(
        _swa_prompt_kernel,
        grid=(nseq, nb),
        in_specs=[pl.BlockSpec((1, SW_KV * SW_REP), lambda b, i: (0, 0)),
                  pl.BlockSpec((WINDOW, SW_Q), lambda b, i: (b * nb + i, 0)),
                  pl.BlockSpec((WINDOW, SW_KVD), prev),
                  pl.BlockSpec((WINDOW, SW_KVD), lambda b, i: (b * nb + i, kcol)),
                  pl.BlockSpec((WINDOW, SW_KVD), prev_v),
                  pl.BlockSpec((WINDOW, SW_KVD), lambda b, i: (b * nb + i, vcol))],
        out_specs=pl.BlockSpec((WINDOW, SW_Q), lambda b, i: (b * nb + i, 0)),
        out_shape=jax.ShapeDtypeStruct((m, SW_Q), F32),
        compiler_params=_cp(("parallel", "arbitrary")),
        name="swa_prompt",
    )(sinks.reshape(1, -1), proj, proj, proj, proj, proj)


def _swa_decode_kernel(q_ref, kn_ref, vn_ref, knc_ref, vnc_ref, sink_ref, kb_ref, vb_ref, *rest):
    o_ref, kbo_ref, vbo_ref = rest[-3:]
    last = lax.broadcasted_iota(jnp.int32, (SW_KVD, WINDOW), 1) == WINDOW - 1
    nh = SW_KV * SW_REP
    rowg = lax.broadcasted_iota(jnp.int32, (nh, SW_KVD), 0) // SW_REP
    colg = lax.broadcasted_iota(jnp.int32, (nh, SW_KVD), 1) // SW_D
    own = rowg == colg
    col = lax.broadcasted_iota(jnp.int32, (nh, WINDOW), 1)
    sink = sink_ref[...]
    for b in range(q_ref.shape[0]):
        q = q_ref[b] * (SW_D ** -0.5)
        q4 = jnp.concatenate([q] * SW_KV, axis=-1)
        qbd = jnp.where(own, q4, 0.0)
        s = _dot(qbd, kb_ref[b])
        s = jnp.where(col >= 1, s, NEG)
        s_new = jnp.sum(qbd * kn_ref[b], axis=-1, keepdims=True)
        mx = jnp.maximum(jnp.maximum(jnp.max(s, axis=-1, keepdims=True), s_new), sink)
        p = jnp.exp(s - mx)
        p_new = jnp.exp(s_new - mx)
        den = jnp.sum(p, axis=-1, keepdims=True) + p_new + jnp.exp(sink - mx)
        o = (_dot_nt(p, vb_ref[b]) + p_new * vn_ref[b]) / den
        o = jnp.where(own, o, 0.0)
        out = o[:, 0:SW_D]
        for g in range(1, SW_KV):
            out = out + o[:, g * SW_D:(g + 1) * SW_D]
        o_ref[b] = out
        kbo_ref[0, b] = jnp.where(last, knc_ref[b], pltpu.roll(kb_ref[b], WINDOW - 1, axis=1))
        vbo_ref[0, b] = jnp.where(last, vnc_ref[b], pltpu.roll(vb_ref[b], WINDOW - 1, axis=1))


def _swa_decode(q, k_new, v_new, sinks, win_k, win_v, layer, n_layers, buf_stacks=None):
    nb = q.shape[0]
    nh = SW_KV * SW_REP
    sb = 4 if nb % 4 == 0 else 1
    buf_block = pl.BlockSpec((1, sb, SW_KVD, WINDOW), lambda b: (layer, b, 0, 0))
    in_specs = [pl.BlockSpec((sb, nh, SW_D), lambda b: (b, 0, 0)),
                pl.BlockSpec((sb, 1, SW_KVD), lambda b: (b, 0, 0)),
                pl.BlockSpec((sb, 1, SW_KVD), lambda b: (b, 0, 0)),
                pl.BlockSpec((sb, SW_KVD, 1), lambda b: (b, 0, 0)),
                pl.BlockSpec((sb, SW_KVD, 1), lambda b: (b, 0, 0)),
                pl.BlockSpec((nh, 1), lambda b: (0, 0)),
                pl.BlockSpec((sb, SW_KVD, WINDOW), lambda b: (b, 0, 0)),
                pl.BlockSpec((sb, SW_KVD, WINDOW), lambda b: (b, 0, 0))]
    args = [q.reshape(nb, nh, SW_D), k_new.reshape(nb, 1, SW_KVD), v_new.reshape(nb, 1, SW_KVD),
            k_new.reshape(nb, SW_KVD, 1), v_new.reshape(nb, SW_KVD, 1), sinks.reshape(nh, 1),
            win_k.transpose(0, 2, 3, 1).reshape(nb, SW_KVD, WINDOW),
            win_v.transpose(0, 2, 3, 1).reshape(nb, SW_KVD, WINDOW)]
    aliases = {}
    if buf_stacks is not None:
        for t, stack in enumerate(buf_stacks):
            aliases[len(args)] = 1 + t
            in_specs.append(pl.BlockSpec(memory_space=pl.ANY))
            args.append(stack)
    stack_shape = jax.ShapeDtypeStruct((n_layers, nb, SW_KVD, WINDOW), F32)
    out, kb_new, vb_new = pl.pallas_call(
        _swa_decode_kernel,
        grid=(nb // sb,),
        in_specs=in_specs,
        out_specs=[pl.BlockSpec((sb, nh, SW_D), lambda b: (b, 0, 0)), buf_block, buf_block],
        out_shape=[jax.ShapeDtypeStruct((nb, nh, SW_D), F32), stack_shape, stack_shape],
        input_output_aliases=aliases,
        compiler_params=_cp(("parallel",)),
        name="swa_decode",
    )(*args)
    return out.reshape(nb, SW_Q), (kb_new, vb_new)


def _trunk(x, mod, w, grp, *, nseq, seq, caches, bf16_w):
    prompt = caches is None
    tm_ffn, tm_in, tm_out = (min(1024, seq), min(512, seq), min(512, seq)) if prompt else (nseq,) * 3
    tf, tn, tk = 512, 512, 512
    g_norm = w['g_norm']
    ffn_w = (w['w_ffn_gate'], w['w_ffn_up'], w['w_ffn_down'])

    def ffn(x, l, s, sub):
        if prompt:
            return _ffn(x, mod, g_norm, None, l, s, sub, grp, tm=tm_ffn, tf=tf, tiles=bf16_w['ffn', l, s])
        x, bf16_w['ffn', l, s] = _ffn(x, mod, g_norm, ffn_w, l, s, sub, grp, tm=tm_ffn, tf=tf)
        return x

    ks, vs, convs, ssms, wks, wvs = [], [], [], [], [], []
    kv_stack = ssm_stack = win_stacks = None
    for l in range(DEPTH):
        j = l // 2
        x = ffn(x, l, 0, 0)
        if l % 2 == 0:
            lam_init = 0.8 - 0.6 * math.exp(-0.3 * l)
            w_in_t = w['w_in_even'].transpose(0, 2, 1)
            if prompt:
                proj, dt_raw, *kv_stack = _inproj_res(x, mod, g_norm, bf16_w['in', l], None, w_in_t, l, j,
                                                      grp, tm=tm_in, n_dt=SSM_H, kv_stack=kv_stack)
            else:
                proj, dt_raw, bf16_w['in', l] = _inproj(x, mod, g_norm, w_in_t, None, l, j, EVEN_MAIN, grp,
                                                        tm=tm_in, tn=tn, transposed=True, n_dt=SSM_H)
                k_rows = proj[:, DA_Q:DA_Q + DA_KVD]
                v_rows = proj[:, DA_Q + DA_KVD:DA_Q + 2 * DA_KVD]
            lp = w['lambda_qk'][j]
            gs = w['g_subln'][j]
            ssm_w = (w['conv_w'][j], w['conv_b'][j], w['dt_bias'][j], w['a_log'][j], w['d_skip'][j],
                     w['g_ssm_norm'][j])
            p3 = proj.reshape(nseq, seq, EVEN_MAIN)
            if prompt:
                oa = _da_prompt(proj, lp, gs, nseq, seq, lam_init)
                ob, ssm_new = _ssd_prompt(proj, dt_raw, *ssm_w, nseq, seq)
                ssms.append(ssm_new)
                conv_new = p3[:, seq - (CONV_W - 1):, EVEN_MAIN - CONV_DIM:]
            else:
                cache_k, cache_v, page_table, conv_state, ssm_state = caches['even']
                oa = _da_decode(proj[:, :DA_Q], proj[:, DA_Q:DA_Q + DA_KVD],
                                proj[:, DA_Q + DA_KVD:DA_Q + 2 * DA_KVD], lp, gs, cache_k, cache_v,
                                page_table, j, lam_init)
                ob, ssm_stack = _ssd_decode(proj, dt_raw, conv_state[j], ssm_state, ssm_stack, j, *ssm_w)
                conv_new = jnp.concatenate([conv_state[j][:, 1:], p3[:, :, EVEN_MAIN - CONV_DIM:]], axis=1)
            if not prompt:
                ks.append(k_rows.reshape(nseq, seq, DA_KV, 2 * DA_D))
                vs.append(v_rows.reshape(nseq, seq, DA_KV, 2 * DA_D))
            convs.append(conv_new)
            ys, w_out, b_out = [oa, ob], w['w_out_even'], None
        else:
            n_odd = SW_Q + 2 * SW_KVD
            b_in = w['b_in_odd'][j].reshape(1, -1)
            if prompt:
                proj = _inproj_res(x, mod, g_norm, bf16_w['in', l], b_in, None, l, j, grp, tm=tm_in)
            else:
                proj, bf16_w['in', l] = _inproj(x, mod, g_norm, w['w_in_odd'], b_in, l, j, n_odd, grp,
                                                tm=tm_in, tn=tn, transposed=False)
            p3 = proj.reshape(nseq, seq, n_odd)
            if prompt:
                y = _swa_prompt(proj, w['attn_sinks'][j], nseq, seq)
                tail = p3[:, seq - WINDOW:]
                wks.append(tail[:, :, SW_Q:SW_Q + SW_KVD].reshape(nseq, WINDOW, SW_KV, SW_D))
                wvs.append(tail[:, :, SW_Q + SW_KVD:].reshape(nseq, WINDOW, SW_KV, SW_D))
            else:
                win_k, win_v = caches['odd']
                y, win_stacks = _swa_decode(proj[:, :SW_Q], proj[:, SW_Q:SW_Q + SW_KVD], proj[:, SW_Q + SW_KVD:],
                                            w['attn_sinks'][j], win_k[j], win_v[j], j, win_k.shape[0],
                                            win_stacks)
            ys, w_out, b_out = [y], w['w_out_odd'], w['b_out_odd'][j].reshape(1, -1)
        if prompt:
            x = _outproj_res(x, ys, mod, g_norm, bf16_w['out', l], b_out, l, grp, tm=tm_out)
        else:
            x, bf16_w['out', l] = _outproj(x, ys, mod, g_norm, w_out, b_out, l, j, grp, tm=tm_out, tk=tk)
        x = ffn(x, l, 1, 2)
    if prompt:
        k_all, v_all = (s.reshape(s.shape[0], nseq, seq, DA_KV, 2 * DA_D) for s in kv_stack)
        ssm_all, wk_all, wv_all = jnp.stack(ssms), jnp.stack(wks), jnp.stack(wvs)
    else:
        k_all, v_all, ssm_all = jnp.stack(ks), jnp.stack(vs), ssm_stack
        wk_all, wv_all = (s.reshape(s.shape[0], nseq, SW_KV, SW_D, WINDOW).transpose(0, 1, 4, 2, 3)
                          for s in win_stacks)
    return (x, k_all, v_all, jnp.stack(convs), ssm_all, wk_all, wv_all)


def kernel(x_prompt, x_sample, cache_k, cache_v, state_conv, state_ssm, cache_win_k, cache_win_v, page_table,
           c_prompt, c_sample, w_mod, b_mod, g_norm, w_ffn_gate, w_ffn_up, w_ffn_down, w_in_even, lambda_qk,
           g_subln, conv_w, conv_b, dt_bias, a_log, d_skip, g_ssm_norm, w_out_even, w_in_odd, b_in_odd,
           attn_sinks, w_out_odd, b_out_odd):
    w = dict(g_norm=g_norm, w_ffn_gate=w_ffn_gate, w_ffn_up=w_ffn_up, w_ffn_down=w_ffn_down,
             w_in_even=w_in_even, lambda_qk=lambda_qk, g_subln=g_subln, conv_w=conv_w, conv_b=conv_b,
             dt_bias=dt_bias, a_log=a_log, d_skip=d_skip, g_ssm_norm=g_ssm_norm, w_out_even=w_out_even,
             w_in_odd=w_in_odd, b_in_odd=b_in_odd, attn_sinks=attn_sinks, w_out_odd=w_out_odd,
             b_out_odd=b_out_odd)
    bp, seq, d = x_prompt.shape
    bs, dec_seq, _ = x_sample.shape
    assert dec_seq == 1
    pad = (-(bs + bp)) % 16
    c_all = jnp.concatenate([c_sample, c_prompt, jnp.zeros((pad, d), F32)], axis=0)
    mod = _modulation(c_all, w_mod, b_mod)
    grp_s = _Group(0, bs, None)
    grp_p = _Group(bs, bp, seq)

    bf16_w = {}
    caches = dict(even=(cache_k, cache_v, page_table, state_conv, state_ssm), odd=(cache_win_k, cache_win_v))
    ys, ksm, vsm, cvs, sss, wks, wvs = _trunk(x_sample.reshape(bs, d), mod, w, grp_s, nseq=bs, seq=1,
                                              caches=caches, bf16_w=bf16_w)
    yp, kp, vp, cvp, ssp, wkp, wvp = _trunk(x_prompt.reshape(bp * seq, d), mod, w, grp_p, nseq=bp, seq=seq,
                                            caches=None, bf16_w=bf16_w)
    return (yp.reshape(bp, seq, d), ys.reshape(bs, 1, d), kp, vp, ksm, vsm, cvp, cvs, ssp, sss,
            wkp, wvp, wks, wvs)
```
